```python
import jax, jax.numpy as jnp
from jax import lax
import numpy as np

D_MODEL = 2048
BATCH = 8
SEQ = 2048
DEPTH = 1

RET_HEADS = 8
RET_HEAD_DIM = 256
D_RET = RET_HEADS * RET_HEAD_DIM
CONV_GROUPS = 8
D_CONV = D_MODEL
CONV_WIDTH = 3
D_FF = 5632
CHUNK = 128
ROPE_BASE = 10000.0
EPS = 1e-6
N_BRANCHES = 2
IN_COLS = 4 * D_RET + 3 * D_CONV + N_BRANCHES * D_MODEL

kernel_name = "hybrid_retention_shortconv_convffn_adaln"


def _rmsnorm(x, g):
    xf = x.astype(jnp.float32)
    r = xf * lax.rsqrt(jnp.mean(xf * xf, axis=-1, keepdims=True) + EPS)
    return (r * g.astype(jnp.float32)).astype(x.dtype)


def _causal_dwconv(u, w, b=None):
    s = u.shape[1]
    up = jnp.pad(u, ((0, 0), (CONV_WIDTH - 1, 0), (0, 0)))
    y = sum(w[k] * up[:, k:k + s, :] for k in range(CONV_WIDTH))
    if b is not None:
        y = y + b
    return y


def _rotary(x, positions):
    half = x.shape[-1] // 2
    inv_freq = ROPE_BASE ** (-jnp.arange(half, dtype=jnp.float32) / half)
    ang = positions.astype(jnp.float32)[..., None] * inv_freq
    cos = jnp.cos(ang)[:, :, None, :].astype(x.dtype)
    sin = jnp.sin(ang)[:, :, None, :].astype(x.dtype)
    x1, x2 = x[..., :half], x[..., half:]
    return jnp.concatenate([x1 * cos - x2 * sin, x2 * cos + x1 * sin], axis=-1)


def _retention_chunkwise(q, k, v):
    b, s, h, dh = q.shape
    n = s // CHUNK
    dt = q.dtype
    log_gamma = jnp.log(1.0 - 2.0 ** (-5.0 - jnp.arange(h, dtype=jnp.float32)))

    def chunks(t):
        return t.astype(jnp.float32).reshape(b, n, CHUNK, h, dh).transpose(1, 0, 3, 2, 4)

    qc = chunks(q) * (dh ** -0.5)
    kc, vc = chunks(k), chunks(v)

    idx = jnp.arange(CHUNK, dtype=jnp.float32)
    diff = idx[:, None] - idx[None, :]
    decay = jnp.where(diff >= 0,
                      jnp.exp(log_gamma[:, None, None] * jnp.maximum(diff, 0.0)),
                      0.0)
    scores = jnp.einsum('nbhid,nbhjd->nbhij', qc, kc) * decay
    y_inner = jnp.einsum('nbhij,nbhjd->nbhid', scores, vc)

    q_dec = jnp.exp(log_gamma[:, None] * (idx + 1.0))[None, :, :, None]
    k_dec = jnp.exp(log_gamma[:, None] * (CHUNK - 1.0 - idx))[None, :, :, None]
    chunk_dec = jnp.exp(log_gamma * CHUNK)[None, :, None, None]

    def step(state, xs):
        qn, kn, vn = xs
        y_cross = jnp.einsum('bhid,bhde->bhie', qn * q_dec, state)
        state = chunk_dec * state + jnp.einsum('bhjd,bhje->bhde', kn * k_dec, vn)
        return state, y_cross

    state0 = jnp.zeros((b, h, dh, dh), jnp.float32)
    _, y_cross = lax.scan(step, state0, (qc, kc, vc))
    y = (y_inner + y_cross).transpose(1, 0, 3, 2, 4).reshape(b, s, h, dh)
    y = y * lax.rsqrt(jnp.mean(y * y, axis=-1, keepdims=True) + EPS)
    return y.astype(dt)


def setup_inputs(seed: int = 0) -> dict:
    key = jax.random.key(seed)
    ks = jax.random.split(key, 20)
    f32 = jnp.float32

    def nrm(k, shape, fan_in):
        return jax.random.normal(k, shape, f32) * (fan_in ** -0.5)

    x = jax.random.normal(ks[0], (BATCH, SEQ, D_MODEL), f32)
    c = jax.random.normal(ks[1], (BATCH, D_MODEL), f32)
    positions = jnp.broadcast_to(jnp.arange(SEQ, dtype=jnp.int32)[None, :], (BATCH, SEQ))
    return {
        "x": x,
        "c": c,
        "positions": positions,
        "norm1_g": 1.0 + 0.02 * jax.random.normal(ks[2], (DEPTH, D_MODEL), f32),
        "norm2_g": 1.0 + 0.02 * jax.random.normal(ks[3], (DEPTH, D_MODEL), f32),
        "w_ada": nrm(ks[4], (DEPTH, D_MODEL, 6 * D_MODEL), D_MODEL),
        "b_ada": 0.02 * jax.random.normal(ks[5], (DEPTH, 6 * D_MODEL), f32),
        "w_in": nrm(ks[6], (DEPTH, D_MODEL, IN_COLS), D_MODEL),
        "b_gate": 0.02 * jax.random.normal(ks[7], (DEPTH, N_BRANCHES * D_MODEL), f32),
        "w_sc": nrm(ks[8], (DEPTH, CONV_WIDTH, D_CONV), CONV_WIDTH),
        "w_ret_o": nrm(ks[9], (DEPTH, D_RET, D_MODEL), D_RET),
        "w_conv_o": nrm(ks[10], (DEPTH, D_CONV, D_MODEL), D_CONV),
        "w_mix_o": nrm(ks[11], (DEPTH, D_MODEL, D_MODEL), D_MODEL),
        "w_up": nrm(ks[12], (DEPTH, D_MODEL, D_FF), D_MODEL),
        "w_gate": nrm(ks[13], (DEPTH, D_MODEL, D_FF), D_MODEL),
        "w_ffconv": nrm(ks[14], (DEPTH, CONV_WIDTH, D_FF), CONV_WIDTH),
        "b_ffconv": 0.02 * jax.random.normal(ks[15], (DEPTH, D_FF), f32),
        "w_down": nrm(ks[16], (DEPTH, D_FF, D_MODEL), D_FF),
        "final_g": 1.0 + 0.02 * jax.random.normal(ks[17], (D_MODEL,), f32),
    }


def reference(x, c, positions, norm1_g, norm2_g, w_ada, b_ada, w_in, b_gate, w_sc,
              w_ret_o, w_conv_o, w_mix_o, w_up, w_gate, w_ffconv, b_ffconv, w_down,
              final_g):
    b, s, _ = x.shape
    h = x
    c_act = jax.nn.silu(c)
    for l in range(DEPTH):
        mod = c_act @ w_ada[l] + b_ada[l]
        sh1, sc1, g1, sh2, sc2, g2 = [m[:, None, :] for m in jnp.split(mod, 6, axis=-1)]

        xn = _rmsnorm(h, norm1_g[l]) * (1.0 + sc1) + sh1
        proj = xn @ w_in[l]
        splits = np.cumsum([D_RET, D_RET, D_RET, D_RET, D_CONV, D_CONV, D_CONV])
        q, k, v, g_ret, bg, cg, xc, gate_logits = jnp.split(proj, splits, axis=-1)

        q = _rotary(q.reshape(b, s, RET_HEADS, RET_HEAD_DIM), positions)
        k = _rotary(k.reshape(b, s, RET_HEADS, RET_HEAD_DIM), positions)
        v = v.reshape(b, s, RET_HEADS, RET_HEAD_DIM)
        y_ret = _retention_chunkwise(q, k, v).reshape(b, s, D_RET)
        y_ret = (y_ret * jax.nn.silu(g_ret)) @ w_ret_o[l]

        y_conv = bg * _causal_dwconv(cg * xc, w_sc[l])
        y_conv = y_conv @ w_conv_o[l]

        gates = jax.nn.sigmoid(gate_logits + b_gate[l])
        gate_a, gate_b = jnp.split(gates, N_BRANCHES, axis=-1)
        mixed = (gate_a * y_ret + gate_b * y_conv) @ w_mix_o[l]
        h = h + g1 * mixed

        xn2 = _rmsnorm(h, norm2_g[l]) * (1.0 + sc2) + sh2
        up = xn2 @ w_up[l]
        gt = _causal_dwconv(xn2 @ w_gate[l], w_ffconv[l], b_ffconv[l])
        ff = (jax.nn.silu(gt) * up) @ w_down[l]
        h = h + g2 * ff
    return _rmsnorm(h, final_g)
```

```python
import functools

import jax
import jax.numpy as jnp
from jax import lax
from jax.experimental import pallas as pl
from jax.experimental.pallas import tpu as pltpu

F32 = jnp.float32
BF16 = jnp.bfloat16

RET_HEADS = 8
RET_HEAD_DIM = 256
ROT_HALF = RET_HEAD_DIM // 2
CONV_WIDTH = 3
ROPE_BASE = 10000.0
EPS = 1e-6
RET_CHUNK = 256
HALO_ROWS = 8

V7X_VMEM_LIMIT_BYTES = 56 * 1024 * 1024


def _params(*semantics):
    return pltpu.CompilerParams(dimension_semantics=semantics,
                                vmem_limit_bytes=V7X_VMEM_LIMIT_BYTES)


def _silu(v):
    return v * jax.nn.sigmoid(v)


def _causal_conv3(u, prev, w0, w1, w2):
    row = lax.broadcasted_iota(jnp.int32, u.shape, 0)
    p_last = prev[HALO_ROWS - 1:HALO_ROWS, :]
    p_2nd = prev[HALO_ROWS - 2:HALO_ROWS - 1, :]
    u1 = jnp.where(row == 0, p_last, pltpu.roll(u, 1, 0))
    u2 = jnp.where(row == 0, p_2nd, jnp.where(row == 1, p_last, pltpu.roll(u, 2, 0)))
    return w0 * u2 + w1 * u1 + w2 * u


def _ada_kernel(c_ref, w_ref, b_ref, o_ref):
    ca = _silu(c_ref[...]).astype(BF16)
    o_ref[...] = jnp.dot(ca, w_ref[...].astype(BF16), preferred_element_type=F32) + b_ref[...]


def _ada(c, w_ada, b_ada, tn=1024):
    b, d = c.shape
    n = w_ada.shape[1]
    return pl.pallas_call(
        _ada_kernel,
        grid=(n // tn,),
        in_specs=[pl.BlockSpec((b, d), lambda j: (0, 0)),
                  pl.BlockSpec((d, tn), lambda j: (0, j)),
                  pl.BlockSpec((1, tn), lambda j: (0, j))],
        out_specs=pl.BlockSpec((b, tn), lambda j: (0, j)),
        out_shape=jax.ShapeDtypeStruct((b, n), F32),
        compiler_params=_params("arbitrary"),
        name="ada",
    )(c, w_ada, b_ada.reshape(1, n))


def _rope_kernel(pos_ref, inv_ref, cos_ref, sin_ref):
    ang = pos_ref[...].astype(F32) * inv_ref[...]
    cos_ref[...] = jnp.cos(ang)
    sin_ref[...] = jnp.sin(ang)


def _rope_tables(positions, tr=1024):
    t = positions.size
    inv_freq = ROPE_BASE ** (-jnp.arange(ROT_HALF, dtype=F32) / ROT_HALF)
    return pl.pallas_call(
        _rope_kernel,
        grid=(t // tr,),
        in_specs=[pl.BlockSpec((tr, 1), lambda i: (i, 0)),
                  pl.BlockSpec((1, ROT_HALF), lambda i: (0, 0))],
        out_specs=[pl.BlockSpec((tr, ROT_HALF), lambda i: (i, 0))] * 2,
        out_shape=[jax.ShapeDtypeStruct((t, ROT_HALF), F32)] * 2,
        compiler_params=_params("arbitrary"),
        name="rope",
    )(positions.reshape(t, 1), inv_freq.reshape(1, ROT_HALF))


def _inproj_kernel(x_ref, g_ref, sc_ref, sh_ref, w_ref, cos_ref, sin_ref, o_ref, xn_ref, *,
                   q_tiles, rot_tiles):
    j = pl.program_id(1)

    @pl.when(j == 0)
    def _():
        x = x_ref[...]
        r = lax.rsqrt(jnp.mean(x * x, axis=-1, keepdims=True) + EPS)
        xn_ref[...] = ((x * r) * g_ref[...] * (1.0 + sc_ref[...]) + sh_ref[...]).astype(BF16)

    acc = jnp.dot(xn_ref[...], w_ref[...], preferred_element_type=F32)

    @pl.when(j < rot_tiles)
    def _():
        scale = jnp.where(j < q_tiles, RET_HEAD_DIM ** -0.5, 1.0).astype(F32)
        cos = cos_ref[...] * scale
        sin = sin_ref[...] * scale
        for hh in range(acc.shape[1] // RET_HEAD_DIM):
            lo = hh * RET_HEAD_DIM
            x1 = acc[:, lo:lo + ROT_HALF]
            x2 = acc[:, lo + ROT_HALF:lo + RET_HEAD_DIM]
            o_ref[:, lo:lo + ROT_HALF] = (x1 * cos - x2 * sin).astype(BF16)
            o_ref[:, lo + ROT_HALF:lo + RET_HEAD_DIM] = (x2 * cos + x1 * sin).astype(BF16)

    @pl.when(j >= rot_tiles)
    def _():
        o_ref[...] = acc.astype(BF16)


def _inproj(x2d, norm_g, mod3, w_in_bf, cos, sin, seq, tm=1024, tn=1024):
    t, d = x2d.shape
    n = w_in_bf.shape[1]
    d_ret = RET_HEADS * RET_HEAD_DIM
    tps = seq // tm
    kern = functools.partial(_inproj_kernel, q_tiles=d_ret // tn, rot_tiles=2 * d_ret // tn)
    return pl.pallas_call(
        kern,
        grid=(t // tm, n // tn),
        in_specs=[pl.BlockSpec((tm, d), lambda i, j: (i, 0)),
                  pl.BlockSpec((1, d), lambda i, j: (0, 0)),
                  pl.BlockSpec((None, 1, d), lambda i, j: (i // tps, 0, 1)),
                  pl.BlockSpec((None, 1, d), lambda i, j: (i // tps, 0, 0)),
                  pl.BlockSpec((d, tn), lambda i, j: (0, j)),
                  pl.BlockSpec((tm, ROT_HALF), lambda i, j: (i, 0)),
                  pl.BlockSpec((tm, ROT_HALF), lambda i, j: (i, 0))],
        out_specs=pl.BlockSpec((tm, tn), lambda i, j: (i, j)),
        out_shape=jax.ShapeDtypeStruct((t, n), BF16),
        scratch_shapes=[pltpu.VMEM((tm, d), BF16)],
        compiler_params=_params("arbitrary", "arbitrary"),
        name="inproj",
    )(x2d, norm_g.reshape(1, d), mod3, mod3, w_in_bf, cos, sin)


def _ret_kernel(q_ref, k_ref, v_ref, g_ref, o_ref):
    c = RET_CHUNK
    n_chunks = q_ref.shape[0] // c
    head = jnp.full((1, RET_HEAD_DIM), pl.program_id(1), jnp.int32).astype(F32)
    log_gamma = jnp.log(1.0 - jnp.exp2(-5.0 - head))
    row = lax.broadcasted_iota(jnp.int32, (c, c), 0).astype(F32)
    col = lax.broadcasted_iota(jnp.int32, (c, c), 1).astype(F32)
    diff = row - col
    decay = jnp.where(diff >= 0, jnp.exp(log_gamma * jnp.maximum(diff, 0.0)), 0.0)
    q_dec = jnp.exp(log_gamma * (row + 1.0))
    k_dec = jnp.exp(log_gamma * (c - 1.0 - row))
    chunk_dec = jnp.exp(log_gamma * float(c))

    state = None
    for n in range(n_chunks):
        rows = pl.ds(n * c, c)
        q = q_ref[rows, :]
        k = k_ref[rows, :]
        v = v_ref[rows, :]
        scores = lax.dot_general(q, k, (((1,), (1,)), ((), ())), preferred_element_type=F32) * decay
        y = jnp.dot(scores.astype(BF16), v, preferred_element_type=F32)
        if state is not None:
            qd = (q.astype(F32) * q_dec).astype(BF16)
            y = y + jnp.dot(qd, state.astype(BF16), preferred_element_type=F32)
        if n < n_chunks - 1:
            kd_t = (k.astype(F32) * k_dec).T.astype(BF16)
            kv = jnp.dot(kd_t, v, preferred_element_type=F32)
            state = kv if state is None else chunk_dec * state + kv
        yn = y * lax.rsqrt(jnp.mean(y * y, axis=-1, keepdims=True) + EPS)
        o_ref[rows, :] = (yn * _silu(g_ref[rows, :].astype(F32))).astype(BF16)


def _retention(proj, batch, seq):
    t = proj.shape[0]
    hb = RET_HEADS
    spec = lambda sec: pl.BlockSpec((seq, RET_HEAD_DIM), lambda b, h: (b, sec * hb + h))
    return pl.pallas_call(
        _ret_kernel,
        grid=(batch, RET_HEADS),
        in_specs=[spec(0), spec(1), spec(2), spec(3)],
        out_specs=pl.BlockSpec((seq, RET_HEAD_DIM), lambda b, h: (b, h)),
        out_shape=jax.ShapeDtypeStruct((t, RET_HEADS * RET_HEAD_DIM), BF16),
        compiler_params=_params("arbitrary", "arbitrary"),
        name="ret",
    )(proj, proj, proj, proj)


def _merge_kernel(yr_ref, bg_ref, cg_ref, xc_ref, la_ref, lb_ref, ba_ref, bb_ref, wsc_ref,
                  wr_ref, wc_ref, o_ref, yc_ref, halo_ref, *, tiles_per_seq):
    i = pl.program_id(0)
    j = pl.program_id(1)

    @pl.when(jnp.logical_and(i == 0, j == 0))
    def _():
        halo_ref[...] = jnp.zeros_like(halo_ref)

    @pl.when(j == 0)
    def _():
        u = cg_ref[...].astype(F32) * xc_ref[...].astype(F32)
        prev = jnp.where(i % tiles_per_seq == 0, 0.0, halo_ref[...])
        w = wsc_ref[...]
        conv = _causal_conv3(u, prev, w[0:1, :], w[1:2, :], w[2:3, :])
        halo_ref[...] = u[u.shape[0] - HALO_ROWS:, :]
        yc_ref[...] = (bg_ref[...].astype(F32) * conv).astype(BF16)

    ya = jnp.dot(yr_ref[...], wr_ref[...], preferred_element_type=F32)
    yb = jnp.dot(yc_ref[...], wc_ref[...], preferred_element_type=F32)
    ga = jax.nn.sigmoid(la_ref[...].astype(F32) + ba_ref[...])
    gb = jax.nn.sigmoid(lb_ref[...].astype(F32) + bb_ref[...])
    o_ref[...] = (ga * ya + gb * yb).astype(BF16)


def _merge(yr, proj, b_gate, w_sc, w_ret_o_bf, w_conv_o_bf, seq, tm=512, tn=1024):
    t, d = yr.shape
    nt = d // tn
    conv_blk = 4 * RET_HEADS * RET_HEAD_DIM // d
    gate_blk = (4 * RET_HEADS * RET_HEAD_DIM + 3 * d) // tn
    kern = functools.partial(_merge_kernel, tiles_per_seq=seq // tm)
    bgate = b_gate.reshape(1, 2 * d)
    return pl.pallas_call(
        kern,
        grid=(t // tm, nt),
        in_specs=[pl.BlockSpec((tm, d), lambda i, j: (i, 0)),
                  pl.BlockSpec((tm, d), lambda i, j: (i, conv_blk)),
                  pl.BlockSpec((tm, d), lambda i, j: (i, conv_blk + 1)),
                  pl.BlockSpec((tm, d), lambda i, j: (i, conv_blk + 2)),
                  pl.BlockSpec((tm, tn), lambda i, j: (i, gate_blk + j)),
                  pl.BlockSpec((tm, tn), lambda i, j: (i, gate_blk + nt + j)),
                  pl.BlockSpec((1, tn), lambda i, j: (0, j)),
                  pl.BlockSpec((1, tn), lambda i, j: (0, nt + j)),
                  pl.BlockSpec((CONV_WIDTH, d), lambda i, j: (0, 0)),
                  pl.BlockSpec((d, tn), lambda i, j: (0, j)),
                  pl.BlockSpec((d, tn), lambda i, j: (0, j))],
        out_specs=pl.BlockSpec((tm, tn), lambda i, j: (i, j)),
        out_shape=jax.ShapeDtypeStruct((t, d), BF16),
        scratch_shapes=[pltpu.VMEM((tm, d), BF16), pltpu.VMEM((HALO_ROWS, d), F32)],
        compiler_params=_params("arbitrary", "arbitrary"),
        name="merge",
    )(yr, proj, proj, proj, proj, proj, bgate, bgate, w_sc, w_ret_o_bf, w_conv_o_bf)


def _mixout_kernel(m_ref, w_ref, x_ref, g1_ref, g_ref, sc_ref, sh_ref, h_ref, xn_ref):
    mixed = jnp.dot(m_ref[...], w_ref[...], preferred_element_type=F32)
    h = x_ref[...] + g1_ref[...] * mixed
    h_ref[...] = h
    r = lax.rsqrt(jnp.mean(h * h, axis=-1, keepdims=True) + EPS)
    xn_ref[...] = ((h * r) * g_ref[...] * (1.0 + sc_ref[...]) + sh_ref[...]).astype(BF16)


def _mixout(m, w_mix_o_bf, x2d, mod3, norm_g, seq, tm=512):
    t, d = x2d.shape
    tps = seq // tm
    mod_spec = lambda k: pl.BlockSpec((None, 1, d), lambda i: (i // tps, 0, k))
    return pl.pallas_call(
        _mixout_kernel,
        grid=(t // tm,),
        in_specs=[pl.BlockSpec((tm, d), lambda i: (i, 0)),
                  pl.BlockSpec((d, d), lambda i: (0, 0)),
                  pl.BlockSpec((tm, d), lambda i: (i, 0)),
                  mod_spec(2), pl.BlockSpec((1, d), lambda i: (0, 0)), mod_spec(4), mod_spec(3)],
        out_specs=[pl.BlockSpec((tm, d), lambda i: (i, 0)), pl.BlockSpec((tm, d), lambda i: (i, 0))],
        out_shape=[jax.ShapeDtypeStruct((t, d), F32), jax.ShapeDtypeStruct((t, d), BF16)],
        compiler_params=_params("arbitrary"),
        name="mixout",
    )(m, w_mix_o_bf, x2d, mod3, norm_g.reshape(1, d), mod3, mod3)


def _ffn_kernel(xn_ref, wu_ref, wg_ref, wfc_ref, bfc_ref, wd_ref, h_ref, g2_ref, fg_ref,
                o_ref, halo_ref, *, tiles_per_seq):
    i = pl.program_id(0)
    f = pl.program_id(1)

    @pl.when(jnp.logical_and(i == 0, f == 0))
    def _():
        halo_ref[...] = jnp.zeros_like(halo_ref)

    xn = xn_ref[...]
    up = jnp.dot(xn, wu_ref[...], preferred_element_type=F32)
    gpre = jnp.dot(xn, wg_ref[...], preferred_element_type=F32)
    prev = jnp.where(i % tiles_per_seq == 0, 0.0, halo_ref[f])
    w = wfc_ref[...]
    gt = _causal_conv3(gpre, prev, w[0:1, :], w[1:2, :], w[2:3, :]) + bfc_ref[...]
    halo_ref[f] = gpre[gpre.shape[0] - HALO_ROWS:, :]
    act = (_silu(gt) * up).astype(BF16)
    part = jnp.dot(act, wd_ref[...], preferred_element_type=F32)

    @pl.when(f == 0)
    def _():
        o_ref[...] = part

    @pl.when(f > 0)
    def _():
        o_ref[...] += part

    @pl.when(f == pl.num_programs(1) - 1)
    def _():
        h = h_ref[...] + g2_ref[...] * o_ref[...]
        r = lax.rsqrt(jnp.mean(h * h, axis=-1, keepdims=True) + EPS)
        o_ref[...] = (h * r) * fg_ref[...]


def _ffn(xn2, w_up_bf, w_gate_bf, w_ffconv, b_ffconv, w_down_bf, h1, mod3, final_g, seq,
         tm=512, tf=512):
    t, d = xn2.shape
    dff = w_up_bf.shape[1]
    nf = dff // tf
    tps = seq // tm
    kern = functools.partial(_ffn_kernel, tiles_per_seq=tps)
    return pl.pallas_call(
        kern,
        grid=(t // tm, nf),
        in_specs=[pl.BlockSpec((tm, d), lambda i, f: (i, 0)),
                  pl.BlockSpec((d, tf), lambda i, f: (0, f)),
                  pl.BlockSpec((d, tf), lambda i, f: (0, f)),
                  pl.BlockSpec((CONV_WIDTH, tf), lambda i, f: (0, f)),
                  pl.BlockSpec((1, tf), lambda i, f: (0, f)),
                  pl.BlockSpec((tf, d), lambda i, f: (f, 0)),
                  pl.BlockSpec((tm, d), lambda i, f: (i, 0)),
                  pl.BlockSpec((None, 1, d), lambda i, f: (i // tps, 0, 5)),
                  pl.BlockSpec((1, d), lambda i, f: (0, 0))],
        out_specs=pl.BlockSpec((tm, d), lambda i, f: (i, 0)),
        out_shape=jax.ShapeDtypeStruct((t, d), F32),
        scratch_shapes=[pltpu.VMEM((nf, HALO_ROWS, tf), F32)],
        compiler_params=_params("arbitrary", "arbitrary"),
        name="ffn",
    )(xn2, w_up_bf, w_gate_bf, w_ffconv, b_ffconv.reshape(1, dff), w_down_bf, h1, mod3,
      final_g.reshape(1, d))


def kernel(x, c, positions, norm1_g, norm2_g, w_ada, b_ada, w_in, b_gate, w_sc, w_ret_o, w_conv_o,
           w_mix_o, w_up, w_gate, w_ffconv, b_ffconv, w_down, final_g):
    batch, seq, d = x.shape
    depth = w_in.shape[0]
    t = batch * seq
    h = x.reshape(t, d)
    cos, sin = _rope_tables(positions)
    for l in range(depth):
        mod3 = _ada(c, w_ada[l], b_ada[l]).reshape(batch, 1, 6 * d)
        proj = _inproj(h, norm1_g[l], mod3, w_in[l].astype(BF16), cos, sin, seq)
        yr = _retention(proj, batch, seq)
        merged = _merge(yr, proj, b_gate[l], w_sc[l], w_ret_o[l].astype(BF16),
                        w_conv_o[l].astype(BF16), seq)
        h1, xn2 = _mixout(merged, w_mix_o[l].astype(BF16), h, mod3, norm2_g[l], seq)
        assert depth == 1, "final-norm fusion assumes a single layer"
        h = _ffn(xn2, w_up[l].astype(BF16), w_gate[l].astype(BF16), w_ffconv[l], b_ffconv[l],
                 w_down[l].astype(BF16), h1, mod3, final_g, seq)
    return h.reshape(batch, seq, d)
```

```python
import functools

import jax
import jax.numpy as jnp
from jax import lax
from jax.experimental import pallas as pl
from jax.experimental.pallas import tpu as pltpu

F32 = jnp.float32
BF16 = jnp.bfloat16

RET_HEADS = 8
RET_HEAD_DIM = 256
D_RET = RET_HEADS * RET_HEAD_DIM
ROT_HALF = RET_HEAD_DIM // 2
CONV_WIDTH = 3
ROPE_BASE = 10000.0
EPS = 1e-6
RET_CHUNK = 256
HALO_ROWS = 8

V7X_VMEM_LIMIT_BYTES = 56 * 1024 * 1024


def _params(*semantics):
    return pltpu.CompilerParams(dimension_semantics=semantics,
                                vmem_limit_bytes=V7X_VMEM_LIMIT_BYTES)


def _silu(v):
    return v * jax.nn.sigmoid(v)


def _causal_conv3(u, prev, w):
    row = lax.broadcasted_iota(jnp.int32, u.shape, 0)
    p_last = prev[HALO_ROWS - 1:HALO_ROWS, :]
    p_2nd = prev[HALO_ROWS - 2:HALO_ROWS - 1, :]
    u1 = jnp.where(row == 0, p_last, pltpu.roll(u, 1, 0))
    u2 = jnp.where(row == 0, p_2nd, jnp.where(row == 1, p_last, pltpu.roll(u, 2, 0)))
    return w[0:1, :] * u2 + w[1:2, :] * u1 + w[2:3, :] * u


def _ada_kernel(c_ref, w_ref, b_ref, o_ref):
    ca = _silu(c_ref[...]).astype(BF16)
    o_ref[...] = jnp.dot(ca, w_ref[...].astype(BF16), preferred_element_type=F32) + b_ref[...]


def _ada(c, w_ada, b_ada, tn=1024):
    b, d = c.shape
    n = w_ada.shape[1]
    return pl.pallas_call(
        _ada_kernel,
        grid=(n // tn,),
        in_specs=[pl.BlockSpec((b, d), lambda j: (0, 0)),
                  pl.BlockSpec((d, tn), lambda j: (0, j)),
                  pl.BlockSpec((1, tn), lambda j: (0, j))],
        out_specs=pl.BlockSpec((b, tn), lambda j: (0, j)),
        out_shape=jax.ShapeDtypeStruct((b, n), F32),
        compiler_params=_params("arbitrary"),
        name="ada",
    )(c, w_ada, b_ada.reshape(1, n))


def _rope_kernel(pos_ref, inv_ref, cos_ref, sin_ref):
    ang = pos_ref[...].astype(F32) * inv_ref[...]
    cos_ref[...] = jnp.cos(ang)
    sin_ref[...] = jnp.sin(ang)


def _rope_tables(positions, tr=1024):
    t = positions.size
    inv_freq = ROPE_BASE ** (-jnp.arange(ROT_HALF, dtype=F32) / ROT_HALF)
    return pl.pallas_call(
        _rope_kernel,
        grid=(t // tr,),
        in_specs=[pl.BlockSpec((tr, 1), lambda i: (i, 0)),
                  pl.BlockSpec((1, ROT_HALF), lambda i: (0, 0))],
        out_specs=[pl.BlockSpec((tr, ROT_HALF), lambda i: (i, 0))] * 2,
        out_shape=[jax.ShapeDtypeStruct((t, ROT_HALF), F32)] * 2,
        compiler_params=_params("arbitrary"),
        name="rope",
    )(positions.reshape(t, 1), inv_freq.reshape(1, ROT_HALF))


def _inproj_a_kernel(x_ref, g_ref, sc_ref, sh_ref, w_ref, cos_ref, sin_ref, o_ref, xn_ref, *,
                     q_tiles, rot_tiles):
    j = pl.program_id(1)

    @pl.when(j == 0)
    def _():
        x = x_ref[...]
        r = lax.rsqrt(jnp.mean(x * x, axis=-1, keepdims=True) + EPS)
        xn_ref[...] = ((x * r) * g_ref[...] * (1.0 + sc_ref[...]) + sh_ref[...]).astype(BF16)

    @pl.when(j < rot_tiles)
    def _():
        acc = jnp.dot(xn_ref[...], w_ref[...], preferred_element_type=F32)
        scale = jnp.where(j < q_tiles, RET_HEAD_DIM ** -0.5, 1.0).astype(F32)
        cos = cos_ref[...] * scale
        sin = sin_ref[...] * scale
        for hh in range(acc.shape[1] // RET_HEAD_DIM):
            lo = hh * RET_HEAD_DIM
            x1 = acc[:, lo:lo + ROT_HALF]
            x2 = acc[:, lo + ROT_HALF:lo + RET_HEAD_DIM]
            o_ref[:, lo:lo + ROT_HALF] = (x1 * cos - x2 * sin).astype(BF16)
            o_ref[:, lo + ROT_HALF:lo + RET_HEAD_DIM] = (x2 * cos + x1 * sin).astype(BF16)

    @pl.when(j >= rot_tiles)
    def _():
        o_ref[...] = jnp.dot(xn_ref[...], w_ref[...], preferred_element_type=F32).astype(BF16)


def _inproj_a(x2d, norm_g, mod3, w_in_bf, cos, sin, seq, tm=1024, tn=1024):
    t, d = x2d.shape
    n = 4 * D_RET
    tps = seq // tm
    kern = functools.partial(_inproj_a_kernel, q_tiles=D_RET // tn, rot_tiles=2 * D_RET // tn)
    return pl.pallas_call(
        kern,
        grid=(t // tm, n // tn),
        in_specs=[pl.BlockSpec((tm, d), lambda i, j: (i, 0)),
                  pl.BlockSpec((1, d), lambda i, j: (0, 0)),
                  pl.BlockSpec((None, 1, d), lambda i, j: (i // tps, 0, 1)),
                  pl.BlockSpec((None, 1, d), lambda i, j: (i // tps, 0, 0)),
                  pl.BlockSpec((d, tn), lambda i, j: (0, j)),
                  pl.BlockSpec((tm, ROT_HALF), lambda i, j: (i, 0)),
                  pl.BlockSpec((tm, ROT_HALF), lambda i, j: (i, 0))],
        out_specs=[pl.BlockSpec((tm, tn), lambda i, j: (i, j)),
                   pl.BlockSpec((tm, d), lambda i, j: (i, 0))],
        out_shape=[jax.ShapeDtypeStruct((t, n), BF16), jax.ShapeDtypeStruct((t, d), BF16)],
        compiler_params=_params("arbitrary", "arbitrary"),
        name="inproj_a",
    )(x2d, norm_g.reshape(1, d), mod3, mod3, w_in_bf, cos, sin)


def _inproj_c_kernel(xn_ref, wb_ref, wc_ref, wx_ref, wsc_ref, o_ref, halo_ref, *, tiles_per_seq):
    i = pl.program_id(0)
    j = pl.program_id(1)

    @pl.when(jnp.logical_and(i == 0, j == 0))
    def _():
        halo_ref[...] = jnp.zeros_like(halo_ref)

    xn = xn_ref[...]
    cg = jnp.dot(xn, wc_ref[...], preferred_element_type=F32)
    xc = jnp.dot(xn, wx_ref[...], preferred_element_type=F32)
    u = cg * xc
    prev = jnp.where(i % tiles_per_seq == 0, 0.0, halo_ref[j])
    conv = _causal_conv3(u, prev, wsc_ref[...])
    halo_ref[j] = u[u.shape[0] - HALO_ROWS:, :]
    bg = jnp.dot(xn, wb_ref[...], preferred_element_type=F32)
    o_ref[...] = (bg * conv).astype(BF16)


def _inproj_c(xn, w_in_bf, w_sc, seq, tm=1024, tc=512):
    t, d = xn.shape
    nc = d // tc
    first = 4 * D_RET // tc
    kern = functools.partial(_inproj_c_kernel, tiles_per_seq=seq // tm)
    wspec = lambda sec: pl.BlockSpec((d, tc), lambda i, j: (0, first + sec * nc + j))
    return pl.pallas_call(
        kern,
        grid=(t // tm, nc),
        in_specs=[pl.BlockSpec((tm, d), lambda i, j: (i, 0)),
                  wspec(0), wspec(1), wspec(2),
                  pl.BlockSpec((CONV_WIDTH, tc), lambda i, j: (0, j))],
        out_specs=pl.BlockSpec((tm, tc), lambda i, j: (i, j)),
        out_shape=jax.ShapeDtypeStruct((t, d), BF16),
        scratch_shapes=[pltpu.VMEM((nc, HALO_ROWS, tc), F32)],
        compiler_params=_params("arbitrary", "arbitrary"),
        name="inproj_c",
    )(xn, w_in_bf, w_in_bf, w_in_bf, w_sc)


def _inproj_g_kernel(xn_ref, w_ref, b_ref, o_ref):
    logits = jnp.dot(xn_ref[...], w_ref[...], preferred_element_type=F32) + b_ref[...]
    o_ref[...] = jax.nn.sigmoid(logits).astype(BF16)


def _inproj_g(xn, w_in_bf, b_gate, tm=1024, tn=1024):
    t, d = xn.shape
    n = b_gate.shape[0]
    first = (w_in_bf.shape[1] - n) // tn
    return pl.pallas_call(
        _inproj_g_kernel,
        grid=(t // tm, n // tn),
        in_specs=[pl.BlockSpec((tm, d), lambda i, j: (i, 0)),
                  pl.BlockSpec((d, tn), lambda i, j: (0, first + j)),
                  pl.BlockSpec((1, tn), lambda i, j: (0, j))],
        out_specs=pl.BlockSpec((tm, tn), lambda i, j: (i, j)),
        out_shape=jax.ShapeDtypeStruct((t, n), BF16),
        compiler_params=_params("arbitrary", "arbitrary"),
        name="inproj_g",
    )(xn, w_in_bf, b_gate.reshape(1, n))


def _ret_kernel(q_ref, k_ref, v_ref, g_ref, o_ref):
    c = RET_CHUNK
    n_chunks = q_ref.shape[0] // c
    head = jnp.full((1, RET_HEAD_DIM), pl.program_id(1), jnp.int32).astype(F32)
    log_gamma = jnp.log(1.0 - jnp.exp2(-5.0 - head))
    row = lax.broadcasted_iota(jnp.int32, (c, c), 0).astype(F32)
    col = lax.broadcasted_iota(jnp.int32, (c, c), 1).astype(F32)
    diff = row - col
    decay = jnp.where(diff >= 0, jnp.exp(log_gamma * jnp.maximum(diff, 0.0)), 0.0)
    q_dec = jnp.exp(log_gamma * (row + 1.0))
    k_dec = jnp.exp(log_gamma * (c - 1.0 - row))
    chunk_dec = jnp.exp(log_gamma * float(c))

    state = None
    for n in range(n_chunks):
        rows = pl.ds(n * c, c)
        q = q_ref[rows, :]
        k = k_ref[rows, :]
        v = v_ref[rows, :]
        scores = lax.dot_general(q, k, (((1,), (1,)), ((), ())), preferred_element_type=F32) * decay
        y = jnp.dot(scores.astype(BF16), v, preferred_element_type=F32)
        if state is not None:
            qd = (q.astype(F32) * q_dec).astype(BF16)
            y = y + jnp.dot(qd, state.astype(BF16), preferred_element_type=F32)
        if n < n_chunks - 1:
            kd_t = (k.astype(F32) * k_dec).T.astype(BF16)
            kv = jnp.dot(kd_t, v, preferred_element_type=F32)
            state = kv if state is None else chunk_dec * state + kv
        yn = y * lax.rsqrt(jnp.mean(y * y, axis=-1, keepdims=True) + EPS)
        o_ref[rows, :] = (yn * _silu(g_ref[rows, :].astype(F32))).astype(BF16)


def _retention(qkvg, batch, seq):
    t = qkvg.shape[0]
    spec = lambda sec: pl.BlockSpec((seq, RET_HEAD_DIM), lambda b, h: (b, sec * RET_HEADS + h))
    return pl.pallas_call(
        _ret_kernel,
        grid=(batch, RET_HEADS),
        in_specs=[spec(0), spec(1), spec(2), spec(3)],
        out_specs=pl.BlockSpec((seq, RET_HEAD_DIM), lambda b, h: (b, h)),
        out_shape=jax.ShapeDtypeStruct((t, D_RET), BF16),
        compiler_params=_params("arbitrary", "arbitrary"),
        name="ret",
    )(qkvg, qkvg, qkvg, qkvg)


def _merge_kernel(yr_ref, yc_ref, ga_ref, gb_ref, wr_ref, wc_ref, o_ref):
    ya = jnp.dot(yr_ref[...], wr_ref[...], preferred_element_type=F32)
    yb = jnp.dot(yc_ref[...], wc_ref[...], preferred_element_type=F32)
    o_ref[...] = (ga_ref[...].astype(F32) * ya + gb_ref[...].astype(F32) * yb).astype(BF16)


def _merge(yr, yc, gates, w_ret_o_bf, w_conv_o_bf, tm=512, tn=1024):
    t, d = yr.shape
    nt = d // tn
    return pl.pallas_call(
        _merge_kernel,
        grid=(t // tm, nt),
        in_specs=[pl.BlockSpec((tm, d), lambda i, j: (i, 0)),
                  pl.BlockSpec((tm, d), lambda i, j: (i, 0)),
                  pl.BlockSpec((tm, tn), lambda i, j: (i, j)),
                  pl.BlockSpec((tm, tn), lambda i, j: (i, nt + j)),
                  pl.BlockSpec((d, tn), lambda i, j: (0, j)),
                  pl.BlockSpec((d, tn), lambda i, j: (0, j))],
        out_specs=pl.BlockSpec((tm, tn), lambda i, j: (i, j)),
        out_shape=jax.ShapeDtypeStruct((t, d), BF16),
        compiler_params=_params("arbitrary", "arbitrary"),
        name="merge",
    )(yr, yc, gates, gates, w_ret_o_bf, w_conv_o_bf)


def _mixout_kernel(m_ref, w_ref, x_ref, g1_ref, g_ref, sc_ref, sh_ref, h_ref, xn_ref):
    mixed = jnp.dot(m_ref[...], w_ref[...], preferred_element_type=F32)
    h = x_ref[...] + g1_ref[...] * mixed
    h_ref[...] = h
    r = lax.rsqrt(jnp.mean(h * h, axis=-1, keepdims=True) + EPS)
    xn_ref[...] = ((h * r) * g_ref[...] * (1.0 + sc_ref[...]) + sh_ref[...]).astype(BF16)


def _mixout(m, w_mix_o_bf, x2d, mod3, norm_g, seq, tm=512):
    t, d = x2d.shape
    tps = seq // tm
    mod_spec = lambda k: pl.BlockSpec((None, 1, d), lambda i: (i // tps, 0, k))
    return pl.pallas_call(
        _mixout_kernel,
        grid=(t // tm,),
        in_specs=[pl.BlockSpec((tm, d), lambda i: (i, 0)),
                  pl.BlockSpec((d, d), lambda i: (0, 0)),
                  pl.BlockSpec((tm, d), lambda i: (i, 0)),
                  mod_spec(2), pl.BlockSpec((1, d), lambda i: (0, 0)), mod_spec(4), mod_spec(3)],
        out_specs=[pl.BlockSpec((tm, d), lambda i: (i, 0)), pl.BlockSpec((tm, d), lambda i: (i, 0))],
        out_shape=[jax.ShapeDtypeStruct((t, d), F32), jax.ShapeDtypeStruct((t, d), BF16)],
        compiler_params=_params("arbitrary"),
        name="mixout",
    )(m, w_mix_o_bf, x2d, mod3, norm_g.reshape(1, d), mod3, mod3)


def _ffn_kernel(xn_ref, wu_ref, wg_ref, wfc_ref, bfc_ref, wd_ref, h_ref, g2_ref, fg_ref,
                o_ref, halo_ref, *, tiles_per_seq, n_sub):
    i = pl.program_id(0)
    f = pl.program_id(1)

    @pl.when(jnp.logical_and(i == 0, f == 0))
    def _():
        halo_ref[...] = jnp.zeros_like(halo_ref)

    @pl.when(f == 0)
    def _():
        o_ref[...] = jnp.zeros_like(o_ref)

    xn = xn_ref[...]
    ts = wu_ref.shape[1] // n_sub
    not_first = i % tiles_per_seq != 0
    part = None
    for s in range(n_sub):
        cols = pl.ds(s * ts, ts)
        up = jnp.dot(xn, wu_ref[:, cols], preferred_element_type=F32)
        gpre = jnp.dot(xn, wg_ref[:, cols], preferred_element_type=F32)
        prev = jnp.where(not_first, halo_ref[f, :, cols], 0.0)
        gt = _causal_conv3(gpre, prev, wfc_ref[:, cols]) + bfc_ref[:, cols]
        halo_ref[f, :, cols] = gpre[gpre.shape[0] - HALO_ROWS:, :]
        act = (_silu(gt) * up).astype(BF16)
        down = jnp.dot(act, wd_ref[cols, :], preferred_element_type=F32)
        part = down if part is None else part + down
    o_ref[...] += part

    @pl.when(f == pl.num_programs(1) - 1)
    def _():
        h = h_ref[...] + g2_ref[...] * o_ref[...]
        r = lax.rsqrt(jnp.mean(h * h, axis=-1, keepdims=True) + EPS)
        o_ref[...] = (h * r) * fg_ref[...]


def _ffn(xn2, w_up_bf, w_gate_bf, w_ffconv, b_ffconv, w_down_bf, h1, mod3, final_g, seq,
         tm=512, tf=512, n_sub=2):
    t, d = xn2.shape
    dff = w_up_bf.shape[1]
    nf = dff // tf
    tps = seq // tm
    kern = functools.partial(_ffn_kernel, tiles_per_seq=tps, n_sub=n_sub)
    return pl.pallas_call(
        kern,
        grid=(t // tm, nf),
        in_specs=[pl.BlockSpec((tm, d), lambda i, f: (i, 0)),
                  pl.BlockSpec((d, tf), lambda i, f: (0, f)),
                  pl.BlockSpec((d, tf), lambda i, f: (0, f)),
                  pl.BlockSpec((CONV_WIDTH, tf), lambda i, f: (0, f)),
                  pl.BlockSpec((1, tf), lambda i, f: (0, f)),
                  pl.BlockSpec((tf, d), lambda i, f: (f, 0)),
                  pl.BlockSpec((tm, d), lambda i, f: (i, 0)),
                  pl.BlockSpec((None, 1, d), lambda i, f: (i // tps, 0, 5)),
                  pl.BlockSpec((1, d), lambda i, f: (0, 0))],
        out_specs=pl.BlockSpec((tm, d), lambda i, f: (i, 0)),
        out_shape=jax.ShapeDtypeStruct((t, d), F32),
        scratch_shapes=[pltpu.VMEM((nf, HALO_ROWS, tf), F32)],
        compiler_params=_params("arbitrary", "arbitrary"),
        name="ffn",
    )(xn2, w_up_bf, w_gate_bf, w_ffconv, b_ffconv.reshape(1, dff), w_down_bf, h1, mod3,
      final_g.reshape(1, d))


def kernel(x, c, positions, norm1_g, norm2_g, w_ada, b_ada, w_in, b_gate, w_sc, w_ret_o, w_conv_o,
           w_mix_o, w_up, w_gate, w_ffconv, b_ffconv, w_down, final_g):
    batch, seq, d = x.shape
    depth = w_in.shape[0]
    assert depth == 1, "final-norm fusion assumes a single layer"
    t = batch * seq
    h = x.reshape(t, d)
    cos, sin = _rope_tables(positions)
    for l in range(depth):
        mod3 = _ada(c, w_ada[l], b_ada[l]).reshape(batch, 1, 6 * d)
        w_in_bf = w_in[l].astype(BF16)
        qkvg, xn = _inproj_a(h, norm1_g[l], mod3, w_in_bf, cos, sin, seq)
        yc = _inproj_c(xn, w_in_bf, w_sc[l], seq)
        gates = _inproj_g(xn, w_in_bf, b_gate[l])
        yr = _retention(qkvg, batch, seq)
        merged = _merge(yr, yc, gates, w_ret_o[l].astype(BF16), w_conv_o[l].astype(BF16))
        h1, xn2 = _mixout(merged, w_mix_o[l].astype(BF16), h, mod3, norm2_g[l], seq)
        h = _ffn(xn2, w_up[l].astype(BF16), w_gate[l].astype(BF16), w_ffconv[l], b_ffconv[l],
                 w_down[l].astype(BF16), h1, mod3, final_g, seq)
    return h.reshape(batch, seq, d)
```

```python
import functools

import jax
import jax.numpy as jnp
from jax import lax
from jax.experimental import pallas as pl
from jax.experimental.pallas import tpu as pltpu

F32 = jnp.float32
BF16 = jnp.bfloat16

RET_HEADS = 8
RET_HEAD_DIM = 256
D_RET = RET_HEADS * RET_HEAD_DIM
ROT_HALF = RET_HEAD_DIM // 2
CONV_WIDTH = 3
ROPE_BASE = 10000.0
EPS = 1e-6
RET_CHUNK = 256
HALO_ROWS = 8

V7X_VMEM_LIMIT_BYTES = 56 * 1024 * 1024


def _params(*semantics):
    return pltpu.CompilerParams(dimension_semantics=semantics,
                                vmem_limit_bytes=V7X_VMEM_LIMIT_BYTES)


def _silu(v):
    return v * jax.nn.sigmoid(v)


def _mod_rmsnorm(x, gain, scale, shift):
    r = lax.rsqrt(jnp.mean(x * x, axis=-1, keepdims=True) + EPS)
    return (x * r) * gain * (1.0 + scale) + shift


def _causal_conv3(u, prev, w):
    row = lax.broadcasted_iota(jnp.int32, u.shape, 0)
    p_last = prev[HALO_ROWS - 1:HALO_ROWS, :]
    p_2nd = prev[HALO_ROWS - 2:HALO_ROWS - 1, :]
    u1 = jnp.where(row == 0, p_last, pltpu.roll(u, 1, 0))
    u2 = jnp.where(row == 0, p_2nd, jnp.where(row == 1, p_last, pltpu.roll(u, 2, 0)))
    return w[0:1, :] * u2 + w[1:2, :] * u1 + w[2:3, :] * u


def _ada_kernel(c_ref, w_ref, b_ref, o_ref):
    ca = _silu(c_ref[...]).astype(BF16)
    o_ref[...] = jnp.dot(ca, w_ref[...].astype(BF16), preferred_element_type=F32) + b_ref[...]


def _ada(c, w_ada, b_ada, tn=1024):
    b, d = c.shape
    n = w_ada.shape[1]
    return pl.pallas_call(
        _ada_kernel,
        grid=(n // tn,),
        in_specs=[pl.BlockSpec((b, d), lambda j: (0, 0)),
                  pl.BlockSpec((d, tn), lambda j: (0, j)),
                  pl.BlockSpec((1, tn), lambda j: (0, j))],
        out_specs=pl.BlockSpec((b, tn), lambda j: (0, j)),
        out_shape=jax.ShapeDtypeStruct((b, n), F32),
        compiler_params=_params("arbitrary"),
        name="ada",
    )(c, w_ada, b_ada.reshape(1, n))


def _rope_kernel(pos_ref, inv_ref, cos_ref, sin_ref):
    ang = pos_ref[...].astype(F32) * inv_ref[...]
    cos_ref[...] = jnp.cos(ang)
    sin_ref[...] = jnp.sin(ang)


def _rope_tables(positions, tr=1024):
    t = positions.size
    inv_freq = ROPE_BASE ** (-jnp.arange(ROT_HALF, dtype=F32) / ROT_HALF)
    return pl.pallas_call(
        _rope_kernel,
        grid=(t // tr,),
        in_specs=[pl.BlockSpec((tr, 1), lambda i: (i, 0)),
                  pl.BlockSpec((1, ROT_HALF), lambda i: (0, 0))],
        out_specs=[pl.BlockSpec((tr, ROT_HALF), lambda i: (i, 0))] * 2,
        out_shape=[jax.ShapeDtypeStruct((t, ROT_HALF), F32)] * 2,
        compiler_params=_params("arbitrary"),
        name="rope",
    )(positions.reshape(t, 1), inv_freq.reshape(1, ROT_HALF))


def _norm1_kernel(x_ref, g_ref, sc_ref, sh_ref, o_ref):
    o_ref[...] = _mod_rmsnorm(x_ref[...], g_ref[...], sc_ref[...], sh_ref[...]).astype(BF16)


def _norm1(x2d, norm_g, mod3, seq, tm=1024):
    t, d = x2d.shape
    tps = seq // tm
    return pl.pallas_call(
        _norm1_kernel,
        grid=(t // tm,),
        in_specs=[pl.BlockSpec((tm, d), lambda i: (i, 0)),
                  pl.BlockSpec((1, d), lambda i: (0, 0)),
                  pl.BlockSpec((None, 1, d), lambda i: (i // tps, 0, 1)),
                  pl.BlockSpec((None, 1, d), lambda i: (i // tps, 0, 0))],
        out_specs=pl.BlockSpec((tm, d), lambda i: (i, 0)),
        out_shape=jax.ShapeDtypeStruct((t, d), BF16),
        compiler_params=_params("arbitrary"),
        name="norm1",
    )(x2d, norm_g.reshape(1, d), mod3, mod3)


def _inproj_a_kernel(xn_ref, w_ref, cos_ref, sin_ref, o_ref, wbf_ref, *, q_tiles, rot_tiles, gate_tile0):
    j = pl.program_id(0)

    @pl.when(pl.program_id(1) == 0)
    def _():
        wbf_ref[...] = w_ref[...].astype(BF16)

    def proj():
        return jnp.dot(xn_ref[...], wbf_ref[...], preferred_element_type=F32)

    @pl.when(j < rot_tiles)
    def _():
        acc = proj()
        scale = jnp.where(j < q_tiles, RET_HEAD_DIM ** -0.5, 1.0).astype(F32)
        cos = cos_ref[...] * scale
        sin = sin_ref[...] * scale
        for hh in range(acc.shape[1] // RET_HEAD_DIM):
            lo = hh * RET_HEAD_DIM
            x1 = acc[:, lo:lo + ROT_HALF]
            x2 = acc[:, lo + ROT_HALF:lo + RET_HEAD_DIM]
            o_ref[:, lo:lo + ROT_HALF] = (x1 * cos - x2 * sin).astype(BF16)
            o_ref[:, lo + ROT_HALF:lo + RET_HEAD_DIM] = (x2 * cos + x1 * sin).astype(BF16)

    @pl.when(jnp.logical_and(j >= rot_tiles, j < gate_tile0))
    def _():
        o_ref[...] = proj().astype(BF16)

    @pl.when(j >= gate_tile0)
    def _():
        o_ref[...] = _silu(proj()).astype(BF16)


def _inproj_a(xn, w_in, cos, sin, tm=1024, tn=1024):
    t, d = xn.shape
    n = 4 * D_RET
    kern = functools.partial(_inproj_a_kernel, q_tiles=D_RET // tn, rot_tiles=2 * D_RET // tn,
                             gate_tile0=3 * D_RET // tn)
    return pl.pallas_call(
        kern,
        grid=(n // tn, t // tm),
        in_specs=[pl.BlockSpec((tm, d), lambda j, i: (i, 0)),
                  pl.BlockSpec((d, tn), lambda j, i: (0, j)),
                  pl.BlockSpec((tm, ROT_HALF), lambda j, i: (i, 0)),
                  pl.BlockSpec((tm, ROT_HALF), lambda j, i: (i, 0))],
        out_specs=pl.BlockSpec((tm, tn), lambda j, i: (i, j)),
        out_shape=jax.ShapeDtypeStruct((t, n), BF16),
        scratch_shapes=[pltpu.VMEM((d, tn), BF16)],
        compiler_params=_params("arbitrary", "arbitrary"),
        name="inproj_a",
    )(xn, w_in, cos, sin)


def _inproj_c_kernel(xn_ref, wb_ref, wc_ref, wx_ref, wsc_ref, o_ref, wb_bf, wc_bf, wx_bf, halo_ref, *,
                     tiles_per_seq):
    j = pl.program_id(0)
    i = pl.program_id(1)

    @pl.when(jnp.logical_and(i == 0, j == 0))
    def _():
        halo_ref[...] = jnp.zeros_like(halo_ref)

    @pl.when(i == 0)
    def _():
        wb_bf[...] = wb_ref[...].astype(BF16)
        wc_bf[...] = wc_ref[...].astype(BF16)
        wx_bf[...] = wx_ref[...].astype(BF16)

    xn = xn_ref[...]
    cg = jnp.dot(xn, wc_bf[...], preferred_element_type=F32)
    xc = jnp.dot(xn, wx_bf[...], preferred_element_type=F32)
    u = cg * xc
    prev = jnp.where(i % tiles_per_seq == 0, 0.0, halo_ref[...])
    conv = _causal_conv3(u, prev, wsc_ref[...])
    halo_ref[...] = u[u.shape[0] - HALO_ROWS:, :]
    bg = jnp.dot(xn, wb_bf[...], preferred_element_type=F32)
    o_ref[...] = (bg * conv).astype(BF16)


def _inproj_c(xn, w_in, w_sc, seq, tm=1024, tc=512):
    t, d = xn.shape
    nc = d // tc
    first = 4 * D_RET // tc
    kern = functools.partial(_inproj_c_kernel, tiles_per_seq=seq // tm)
    wspec = lambda sec: pl.BlockSpec((d, tc), lambda j, i: (0, first + sec * nc + j))
    return pl.pallas_call(
        kern,
        grid=(nc, t // tm),
        in_specs=[pl.BlockSpec((tm, d), lambda j, i: (i, 0)),
                  wspec(0), wspec(1), wspec(2),
                  pl.BlockSpec((CONV_WIDTH, tc), lambda j, i: (0, j))],
        out_specs=pl.BlockSpec((tm, tc), lambda j, i: (i, j)),
        out_shape=jax.ShapeDtypeStruct((t, d), BF16),
        scratch_shapes=[pltpu.VMEM((d, tc), BF16)] * 3 + [pltpu.VMEM((HALO_ROWS, tc), F32)],
        compiler_params=_params("arbitrary", "arbitrary"),
        name="inproj_c",
    )(xn, w_in, w_in, w_in, w_sc)


def _inproj_g_kernel(xn_ref, w_ref, b_ref, o_ref, wbf_ref):
    @pl.when(pl.program_id(1) == 0)
    def _():
        wbf_ref[...] = w_ref[...].astype(BF16)

    logits = jnp.dot(xn_ref[...], wbf_ref[...], preferred_element_type=F32) + b_ref[...]
    o_ref[...] = jax.nn.sigmoid(logits).astype(BF16)


def _inproj_g(xn, w_in, b_gate, tm=1024, tn=1024):
    t, d = xn.shape
    n = b_gate.shape[0]
    first = (w_in.shape[1] - n) // tn
    return pl.pallas_call(
        _inproj_g_kernel,
        grid=(n // tn, t // tm),
        in_specs=[pl.BlockSpec((tm, d), lambda j, i: (i, 0)),
                  pl.BlockSpec((d, tn), lambda j, i: (0, first + j)),
                  pl.BlockSpec((1, tn), lambda j, i: (0, j))],
        out_specs=pl.BlockSpec((tm, tn), lambda j, i: (i, j)),
        out_shape=jax.ShapeDtypeStruct((t, n), BF16),
        scratch_shapes=[pltpu.VMEM((d, tn), BF16)],
        compiler_params=_params("arbitrary", "arbitrary"),
        name="inproj_g",
    )(xn, w_in, b_gate.reshape(1, n))


def _ret_kernel(q_ref, k_ref, v_ref, gs_ref, o_ref):
    c = RET_CHUNK
    n_chunks = q_ref.shape[0] // c
    head = jnp.full((1, RET_HEAD_DIM), pl.program_id(1), jnp.int32).astype(F32)
    log_gamma = jnp.log(1.0 - jnp.exp2(-5.0 - head))
    row = lax.broadcasted_iota(jnp.int32, (c, c), 0).astype(F32)
    col = lax.broadcasted_iota(jnp.int32, (c, c), 1).astype(F32)
    diff = row - col
    decay = jnp.where(diff >= 0, jnp.exp(log_gamma * jnp.maximum(diff, 0.0)), 0.0)
    q_dec = jnp.exp(log_gamma * (row + 1.0))
    k_dec = jnp.exp(log_gamma * (c - 1.0 - row))
    chunk_dec = jnp.exp(log_gamma * float(c))

    states = [None]
    for n in range(n_chunks - 1):
        rows = pl.ds(n * c, c)
        kd = (k_ref[rows, :].astype(F32) * k_dec).astype(BF16)
        kv = lax.dot_general(kd, v_ref[rows, :], (((0,), (0,)), ((), ())), preferred_element_type=F32)
        states.append(kv if states[-1] is None else chunk_dec * states[-1] + kv)

    for n in range(n_chunks):
        rows = pl.ds(n * c, c)
        q = q_ref[rows, :]
        scores = lax.dot_general(q, k_ref[rows, :], (((1,), (1,)), ((), ())),
                                 preferred_element_type=F32) * decay
        y = jnp.dot(scores.astype(BF16), v_ref[rows, :], preferred_element_type=F32)
        if states[n] is not None:
            y = y + q_dec * jnp.dot(q, states[n].astype(BF16), preferred_element_type=F32)
        yn = y * lax.rsqrt(jnp.mean(y * y, axis=-1, keepdims=True) + EPS)
        o_ref[rows, :] = (yn * gs_ref[rows, :].astype(F32)).astype(BF16)


def _retention(qkvg, batch, seq):
    t = qkvg.shape[0]
    spec = lambda sec: pl.BlockSpec((seq, RET_HEAD_DIM), lambda b, h: (b, sec * RET_HEADS + h))
    return pl.pallas_call(
        _ret_kernel,
        grid=(batch, RET_HEADS),
        in_specs=[spec(0), spec(1), spec(2), spec(3)],
        out_specs=pl.BlockSpec((seq, RET_HEAD_DIM), lambda b, h: (b, h)),
        out_shape=jax.ShapeDtypeStruct((t, D_RET), BF16),
        compiler_params=_params("arbitrary", "arbitrary"),
        name="ret",
    )(qkvg, qkvg, qkvg, qkvg)


def _merge_kernel(yr_ref, yc_ref, ga_ref, gb_ref, wr_ref, wc_ref, o_ref):
    ya = jnp.dot(yr_ref[...], wr_ref[...], preferred_element_type=F32)
    yb = jnp.dot(yc_ref[...], wc_ref[...], preferred_element_type=F32)
    o_ref[...] = (ga_ref[...].astype(F32) * ya + gb_ref[...].astype(F32) * yb).astype(BF16)


def _merge(yr, yc, gates, w_ret_o_bf, w_conv_o_bf, tm=512):
    t, d = yr.shape
    resident = lambda: pl.BlockSpec((d, d), lambda i: (0, 0), pipeline_mode=pl.Buffered(1))
    return pl.pallas_call(
        _merge_kernel,
        grid=(t // tm,),
        in_specs=[pl.BlockSpec((tm, d), lambda i: (i, 0)),
                  pl.BlockSpec((tm, d), lambda i: (i, 0)),
                  pl.BlockSpec((tm, d), lambda i: (i, 0)),
                  pl.BlockSpec((tm, d), lambda i: (i, 1)),
                  resident(), resident()],
        out_specs=pl.BlockSpec((tm, d), lambda i: (i, 0)),
        out_shape=jax.ShapeDtypeStruct((t, d), BF16),
        compiler_params=_params("arbitrary"),
        name="merge",
    )(yr, yc, gates, gates, w_ret_o_bf, w_conv_o_bf)


def _mixout_kernel(m_ref, w_ref, x_ref, g1_ref, g_ref, sc_ref, sh_ref, h_ref, xn_ref, *, n_sub):
    tr = m_ref.shape[0] // n_sub
    for s in range(n_sub):
        rows = pl.ds(s * tr, tr)
        mixed = jnp.dot(m_ref[rows, :], w_ref[...], preferred_element_type=F32)
        h = x_ref[rows, :] + g1_ref[...] * mixed
        h_ref[rows, :] = h
        xn_ref[rows, :] = _mod_rmsnorm(h, g_ref[...], sc_ref[...], sh_ref[...]).astype(BF16)


def _mixout(m, w_mix_o_bf, x2d, mod3, norm_g, seq, tm=512, n_sub=2):
    t, d = x2d.shape
    tps = seq // tm
    mod_spec = lambda k: pl.BlockSpec((None, 1, d), lambda i: (i // tps, 0, k))
    return pl.pallas_call(
        functools.partial(_mixout_kernel, n_sub=n_sub),
        grid=(t // tm,),
        in_specs=[pl.BlockSpec((tm, d), lambda i: (i, 0)),
                  pl.BlockSpec((d, d), lambda i: (0, 0), pipeline_mode=pl.Buffered(1)),
                  pl.BlockSpec((tm, d), lambda i: (i, 0)),
                  mod_spec(2), pl.BlockSpec((1, d), lambda i: (0, 0)), mod_spec(4), mod_spec(3)],
        out_specs=[pl.BlockSpec((tm, d), lambda i: (i, 0)), pl.BlockSpec((tm, d), lambda i: (i, 0))],
        out_shape=[jax.ShapeDtypeStruct((t, d), F32), jax.ShapeDtypeStruct((t, d), BF16)],
        compiler_params=_params("arbitrary"),
        name="mixout",
    )(m, w_mix_o_bf, x2d, mod3, norm_g.reshape(1, d), mod3, mod3)


def _ffn_kernel(xn_ref, wu_ref, wg_ref, wfc_ref, bfc_ref, wd_ref, h_hbm, g2_ref, fg_ref,
                o_ref, halo_ref, h_sem, *, tiles_per_seq, n_sub):
    i = pl.program_id(0)
    f = pl.program_id(1)
    tm = o_ref.shape[0]

    @pl.when(jnp.logical_and(i == 0, f == 0))
    def _():
        halo_ref[...] = jnp.zeros_like(halo_ref)

    def h1_copy():
        return pltpu.make_async_copy(h_hbm.at[pl.ds(i * tm, tm), :], o_ref, h_sem)

    def step(first):
        if first:
            h1_copy().start()
        xn = xn_ref[...]
        ts = wu_ref.shape[1] // n_sub
        not_seq_start = i % tiles_per_seq != 0
        part = None
        for s in range(n_sub):
            cols = pl.ds(s * ts, ts)
            up = jnp.dot(xn, wu_ref[:, cols], preferred_element_type=F32)
            gpre = jnp.dot(xn, wg_ref[:, cols], preferred_element_type=F32)
            prev = jnp.where(not_seq_start, halo_ref[f, :, cols], 0.0)
            gt = _causal_conv3(gpre, prev, wfc_ref[:, cols]) + bfc_ref[:, cols]
            halo_ref[f, :, cols] = gpre[gpre.shape[0] - HALO_ROWS:, :]
            act = (_silu(gt) * up).astype(BF16)
            down = jnp.dot(act, wd_ref[cols, :], preferred_element_type=F32)
            part = down if part is None else part + down
        if first:
            h1_copy().wait()
        o_ref[...] += g2_ref[...] * part

    pl.when(f == 0)(functools.partial(step, True))
    pl.when(f > 0)(functools.partial(step, False))

    @pl.when(f == pl.num_programs(1) - 1)
    def _():
        h = o_ref[...]
        r = lax.rsqrt(jnp.mean(h * h, axis=-1, keepdims=True) + EPS)
        o_ref[...] = (h * r) * fg_ref[...]


def _ffn(xn2, w_up_bf, w_gate_bf, w_ffconv, b_ffconv, w_down_bf, h1, mod3, final_g, seq,
         tm=1024, tf=512, n_sub=2):
    t, d = xn2.shape
    dff = w_up_bf.shape[1]
    nf = dff // tf
    tps = seq // tm
    kern = functools.partial(_ffn_kernel, tiles_per_seq=tps, n_sub=n_sub)
    return pl.pallas_call(
        kern,
        grid=(t // tm, nf),
        in_specs=[pl.BlockSpec((tm, d), lambda i, f: (i, 0)),
                  pl.BlockSpec((d, tf), lambda i, f: (0, f)),
                  pl.BlockSpec((d, tf), lambda i, f: (0, f)),
                  pl.BlockSpec((CONV_WIDTH, tf), lambda i, f: (0, f)),
                  pl.BlockSpec((1, tf), lambda i, f: (0, f)),
                  pl.BlockSpec((tf, d), lambda i, f: (f, 0)),
                  pl.BlockSpec(memory_space=pl.ANY),
                  pl.BlockSpec((None, 1, d), lambda i, f: (i // tps, 0, 5)),
                  pl.BlockSpec((1, d), lambda i, f: (0, 0))],
        out_specs=pl.BlockSpec((tm, d), lambda i, f: (i, 0)),
        out_shape=jax.ShapeDtypeStruct((t, d), F32),
        scratch_shapes=[pltpu.VMEM((nf, HALO_ROWS, tf), F32), pltpu.SemaphoreType.DMA(())],
        compiler_params=_params("arbitrary", "arbitrary"),
        name="ffn",
    )(xn2, w_up_bf, w_gate_bf, w_ffconv, b_ffconv.reshape(1, dff), w_down_bf, h1, mod3,
      final_g.reshape(1, d))


def kernel(x, c, positions, norm1_g, norm2_g, w_ada, b_ada, w_in, b_gate, w_sc, w_ret_o, w_conv_o,
           w_mix_o, w_up, w_gate, w_ffconv, b_ffconv, w_down, final_g):
    batch, seq, d = x.shape
    depth = w_in.shape[0]
    assert depth == 1, "final-norm fusion assumes a single layer"
    t = batch * seq
    h = x.reshape(t, d)
    cos, sin = _rope_tables(positions)
    for l in range(depth):
        mod3 = _ada(c, w_ada[l], b_ada[l]).reshape(batch, 1, 6 * d)
        xn = _norm1(h, norm1_g[l], mod3, seq)
        qkvg = _inproj_a(xn, w_in[l], cos, sin)
        yc = _inproj_c(xn, w_in[l], w_sc[l], seq)
        gates = _inproj_g(xn, w_in[l], b_gate[l])
        yr = _retention(qkvg, batch, seq)
        merged = _merge(yr, yc, gates, w_ret_o[l].astype(BF16), w_conv_o[l].astype(BF16))
        h1, xn2 = _mixout(merged, w_mix_o[l].astype(BF16), h, mod3, norm2_g[l], seq)
        h = _ffn(xn2, w_up[l].astype(BF16), w_gate[l].astype(BF16), w_ffconv[l], b_ffconv[l],
                 w_down[l].astype(BF16), h1, mod3, final_g, seq)
    return h.reshape(batch, seq, d)
```

```python
import functools

import jax
import jax.numpy as jnp
from jax import lax
from jax.experimental import pallas as pl
from jax.experimental.pallas import tpu as pltpu

F32 = jnp.float32
BF16 = jnp.bfloat16

RET_HEADS = 8
RET_HEAD_DIM = 256
D_RET = RET_HEADS * RET_HEAD_DIM
ROT_HALF = RET_HEAD_DIM // 2
CONV_WIDTH = 3
ROPE_BASE = 10000.0
EPS = 1e-6
RET_CHUNK = 256
HALO_ROWS = 8

V7X_VMEM_LIMIT_BYTES = 56 * 1024 * 1024


def _params(*semantics):
    return pltpu.CompilerParams(dimension_semantics=semantics,
                                vmem_limit_bytes=V7X_VMEM_LIMIT_BYTES)


def _silu(v):
    return v * jax.nn.sigmoid(v)


def _mod_rmsnorm(x, gain, scale, shift):
    r = lax.rsqrt(jnp.mean(x * x, axis=-1, keepdims=True) + EPS)
    return (x * r) * gain * (1.0 + scale) + shift


def _causal_conv3(u, prev, w):
    def taps(a):
        return w[0:1, :] * pltpu.roll(a, 2, 0) + w[1:2, :] * pltpu.roll(a, 1, 0) + w[2:3, :] * a

    head = taps(jnp.concatenate([prev, u[:HALO_ROWS, :]], axis=0))[HALO_ROWS:, :]
    return jnp.concatenate([head, taps(u)[HALO_ROWS:, :]], axis=0)


def _ada_kernel(c_ref, w_ref, b_ref, o_ref):
    ca = _silu(c_ref[...]).astype(BF16)
    o_ref[...] = jnp.dot(ca, w_ref[...].astype(BF16), preferred_element_type=F32) + b_ref[...]


def _ada(c, w_ada, b_ada, tn=1024):
    b, d = c.shape
    n = w_ada.shape[1]
    return pl.pallas_call(
        _ada_kernel,
        grid=(n // tn,),
        in_specs=[pl.BlockSpec((b, d), lambda j: (0, 0)),
                  pl.BlockSpec((d, tn), lambda j: (0, j)),
                  pl.BlockSpec((1, tn), lambda j: (0, j))],
        out_specs=pl.BlockSpec((b, tn), lambda j: (0, j)),
        out_shape=jax.ShapeDtypeStruct((b, n), F32),
        compiler_params=_params("arbitrary"),
        name="ada",
    )(c, w_ada, b_ada.reshape(1, n))


def _rope_kernel(pos_ref, inv_ref, cos_ref, sin_ref):
    ang = pos_ref[...].astype(F32) * inv_ref[...]
    cos_ref[...] = jnp.cos(ang)
    sin_ref[...] = jnp.sin(ang)


def _rope_tables(positions, tr=1024):
    t = positions.size
    inv_freq = ROPE_BASE ** (-jnp.arange(ROT_HALF, dtype=F32) / ROT_HALF)
    return pl.pallas_call(
        _rope_kernel,
        grid=(t // tr,),
        in_specs=[pl.BlockSpec((tr, 1), lambda i: (i, 0)),
                  pl.BlockSpec((1, ROT_HALF), lambda i: (0, 0))],
        out_specs=[pl.BlockSpec((tr, ROT_HALF), lambda i: (i, 0))] * 2,
        out_shape=[jax.ShapeDtypeStruct((t, ROT_HALF), F32)] * 2,
        compiler_params=_params("arbitrary"),
        name="rope",
    )(positions.reshape(t, 1), inv_freq.reshape(1, ROT_HALF))


def _norm1_kernel(x_ref, g_ref, sc_ref, sh_ref, o_ref):
    o_ref[...] = _mod_rmsnorm(x_ref[...], g_ref[...], sc_ref[...], sh_ref[...]).astype(BF16)


def _norm1(x2d, norm_g, mod3, seq, tm=1024):
    t, d = x2d.shape
    tps = seq // tm
    return pl.pallas_call(
        _norm1_kernel,
        grid=(t // tm,),
        in_specs=[pl.BlockSpec((tm, d), lambda i: (i, 0)),
                  pl.BlockSpec((1, d), lambda i: (0, 0)),
                  pl.BlockSpec((None, 1, d), lambda i: (i // tps, 0, 1)),
                  pl.BlockSpec((None, 1, d), lambda i: (i // tps, 0, 0))],
        out_specs=pl.BlockSpec((tm, d), lambda i: (i, 0)),
        out_shape=jax.ShapeDtypeStruct((t, d), BF16),
        compiler_params=_params("arbitrary"),
        name="norm1",
    )(x2d, norm_g.reshape(1, d), mod3, mod3)


def _inproj_a_kernel(xn_ref, w_ref, cos_ref, sin_ref, o_ref, wbf_ref, *, q_tiles, rot_tiles, gate_tile0,
                     n_sub):
    j = pl.program_id(0)

    @pl.when(pl.program_id(1) == 0)
    def _():
        wbf_ref[...] = w_ref[...].astype(BF16)

    heads = o_ref.shape[0]
    tr = xn_ref.shape[0] // n_sub

    def proj(rows):
        return jnp.dot(xn_ref[rows, :], wbf_ref[...], preferred_element_type=F32)

    @pl.when(j < rot_tiles)
    def _():
        scale = jnp.where(j < q_tiles, RET_HEAD_DIM ** -0.5, 1.0).astype(F32)
        for s in range(n_sub):
            rows = pl.ds(s * tr, tr)
            acc = proj(rows)
            cos = cos_ref[rows, :] * scale
            sin = sin_ref[rows, :] * scale
            for hh in range(heads):
                lo = hh * RET_HEAD_DIM
                x1 = acc[:, lo:lo + ROT_HALF]
                x2 = acc[:, lo + ROT_HALF:lo + RET_HEAD_DIM]
                o_ref[hh, rows, 0:ROT_HALF] = (x1 * cos - x2 * sin).astype(BF16)
                o_ref[hh, rows, ROT_HALF:RET_HEAD_DIM] = (x2 * cos + x1 * sin).astype(BF16)

    def plain(fn):
        for s in range(n_sub):
            rows = pl.ds(s * tr, tr)
            res = fn(proj(rows)).astype(BF16)
            for hh in range(heads):
                o_ref[hh, rows, :] = res[:, hh * RET_HEAD_DIM:(hh + 1) * RET_HEAD_DIM]

    pl.when(jnp.logical_and(j >= rot_tiles, j < gate_tile0))(functools.partial(plain, lambda a: a))
    pl.when(j >= gate_tile0)(functools.partial(plain, _silu))


def _inproj_a(xn, w_in, cos, sin, tm=1024, tn=1024, n_sub=2):
    t, d = xn.shape
    n = 4 * D_RET
    heads = tn // RET_HEAD_DIM
    kern = functools.partial(_inproj_a_kernel, q_tiles=D_RET // tn, rot_tiles=2 * D_RET // tn,
                             gate_tile0=3 * D_RET // tn, n_sub=n_sub)
    return pl.pallas_call(
        kern,
        grid=(n // tn, t // tm),
        in_specs=[pl.BlockSpec((tm, d), lambda j, i: (i, 0)),
                  pl.BlockSpec((d, tn), lambda j, i: (0, j)),
                  pl.BlockSpec((tm, ROT_HALF), lambda j, i: (i, 0)),
                  pl.BlockSpec((tm, ROT_HALF), lambda j, i: (i, 0))],
        out_specs=pl.BlockSpec((heads, tm, RET_HEAD_DIM), lambda j, i: (j, i, 0)),
        out_shape=jax.ShapeDtypeStruct((n // RET_HEAD_DIM, t, RET_HEAD_DIM), BF16),
        scratch_shapes=[pltpu.VMEM((d, tn), BF16)],
        compiler_params=_params("arbitrary", "arbitrary"),
        name="inproj_a",
    )(xn, w_in, cos, sin)


def _inproj_c_kernel(xn_ref, wb_ref, wc_ref, wx_ref, wsc_ref, o_ref, wb_bf, wc_bf, wx_bf, halo_ref, *,
                     tiles_per_seq):
    j = pl.program_id(0)
    i = pl.program_id(1)

    @pl.when(jnp.logical_and(i == 0, j == 0))
    def _():
        halo_ref[...] = jnp.zeros_like(halo_ref)

    @pl.when(i == 0)
    def _():
        wb_bf[...] = wb_ref[...].astype(BF16)
        wc_bf[...] = wc_ref[...].astype(BF16)
        wx_bf[...] = wx_ref[...].astype(BF16)

    xn = xn_ref[...]
    cg = jnp.dot(xn, wc_bf[...], preferred_element_type=F32)
    xc = jnp.dot(xn, wx_bf[...], preferred_element_type=F32)
    u = cg * xc
    prev = jnp.where(i % tiles_per_seq == 0, 0.0, halo_ref[...])
    conv = _causal_conv3(u, prev, wsc_ref[...])
    halo_ref[...] = u[u.shape[0] - HALO_ROWS:, :]
    bg = jnp.dot(xn, wb_bf[...], preferred_element_type=F32)
    o_ref[...] = (bg * conv).astype(BF16)


def _inproj_c(xn, w_in, w_sc, seq, tm=1024, tc=512):
    t, d = xn.shape
    nc = d // tc
    first = 4 * D_RET // tc
    kern = functools.partial(_inproj_c_kernel, tiles_per_seq=seq // tm)
    wspec = lambda sec: pl.BlockSpec((d, tc), lambda j, i: (0, first + sec * nc + j))
    return pl.pallas_call(
        kern,
        grid=(nc, t // tm),
        in_specs=[pl.BlockSpec((tm, d), lambda j, i: (i, 0)),
                  wspec(0), wspec(1), wspec(2),
                  pl.BlockSpec((CONV_WIDTH, tc), lambda j, i: (0, j))],
        out_specs=pl.BlockSpec((tm, tc), lambda j, i: (i, j)),
        out_shape=jax.ShapeDtypeStruct((t, d), BF16),
        scratch_shapes=[pltpu.VMEM((d, tc), BF16)] * 3 + [pltpu.VMEM((HALO_ROWS, tc), F32)],
        compiler_params=_params("arbitrary", "arbitrary"),
        name="inproj_c",
    )(xn, w_in, w_in, w_in, w_sc)


def _inproj_g_kernel(xn_ref, w_ref, b_ref, o_ref, wbf_ref, *, n_sub):
    @pl.when(pl.program_id(1) == 0)
    def _():
        wbf_ref[...] = w_ref[...].astype(BF16)

    tr = xn_ref.shape[0] // n_sub
    for s in range(n_sub):
        rows = pl.ds(s * tr, tr)
        logits = jnp.dot(xn_ref[rows, :], wbf_ref[...], preferred_element_type=F32) + b_ref[...]
        o_ref[rows, :] = jax.nn.sigmoid(logits).astype(BF16)


def _inproj_g(xn, w_in, b_gate, tm=1024, tn=1024, n_sub=2):
    t, d = xn.shape
    n = b_gate.shape[0]
    first = (w_in.shape[1] - n) // tn
    return pl.pallas_call(
        functools.partial(_inproj_g_kernel, n_sub=n_sub),
        grid=(n // tn, t // tm),
        in_specs=[pl.BlockSpec((tm, d), lambda j, i: (i, 0)),
                  pl.BlockSpec((d, tn), lambda j, i: (0, first + j)),
                  pl.BlockSpec((1, tn), lambda j, i: (0, j))],
        out_specs=pl.BlockSpec((tm, tn), lambda j, i: (i, j)),
        out_shape=jax.ShapeDtypeStruct((t, n), BF16),
        scratch_shapes=[pltpu.VMEM((d, tn), BF16)],
        compiler_params=_params("arbitrary", "arbitrary"),
        name="inproj_g",
    )(xn, w_in, b_gate.reshape(1, n))


def _ret_kernel(q_ref, k_ref, v_ref, gs_ref, o_ref):
    c = RET_CHUNK
    n_chunks = q_ref.shape[0] // c
    head = jnp.full((1, RET_HEAD_DIM), pl.program_id(1), jnp.int32).astype(F32)
    log_gamma = jnp.log(1.0 - jnp.exp2(-5.0 - head))
    row = lax.broadcasted_iota(jnp.int32, (c, c), 0).astype(F32)
    col = lax.broadcasted_iota(jnp.int32, (c, c), 1).astype(F32)
    diff = row - col
    decay = jnp.where(diff >= 0, jnp.exp(log_gamma * jnp.maximum(diff, 0.0)), 0.0)
    q_dec = jnp.exp(log_gamma * (row + 1.0))
    k_dec = jnp.exp(log_gamma * (c - 1.0 - row))
    chunk_dec = jnp.exp(log_gamma * float(c))

    states = [None]
    for n in range(n_chunks - 1):
        rows = pl.ds(n * c, c)
        kd = (k_ref[rows, :].astype(F32) * k_dec).astype(BF16)
        kv = lax.dot_general(kd, v_ref[rows, :], (((0,), (0,)), ((), ())), preferred_element_type=F32)
        states.append(kv if states[-1] is None else chunk_dec * states[-1] + kv)

    for n in range(n_chunks):
        rows = pl.ds(n * c, c)
        q = q_ref[rows, :]
        scores = lax.dot_general(q, k_ref[rows, :], (((1,), (1,)), ((), ())),
                                 preferred_element_type=F32) * decay
        y = jnp.dot(scores.astype(BF16), v_ref[rows, :], preferred_element_type=F32)
        if states[n] is not None:
            y = y + q_dec * jnp.dot(q, states[n].astype(BF16), preferred_element_type=F32)
        yn = y * lax.rsqrt(jnp.mean(y * y, axis=-1, keepdims=True) + EPS)
        o_ref[rows, :] = (yn * gs_ref[rows, :].astype(F32)).astype(BF16)


def _retention(qkvg, batch, seq):
    t = qkvg.shape[1]
    spec = lambda sec: pl.BlockSpec((None, seq, RET_HEAD_DIM), lambda b, h: (sec * RET_HEADS + h, b, 0))
    return pl.pallas_call(
        _ret_kernel,
        grid=(batch, RET_HEADS),
        in_specs=[spec(0), spec(1), spec(2), spec(3)],
        out_specs=pl.BlockSpec((seq, RET_HEAD_DIM), lambda b, h: (b, h)),
        out_shape=jax.ShapeDtypeStruct((t, D_RET), BF16),
        compiler_params=_params("arbitrary", "arbitrary"),
        name="ret",
    )(qkvg, qkvg, qkvg, qkvg)


def _merge_kernel(yr_ref, yc_ref, ga_ref, gb_ref, wr_ref, wc_ref, o_ref):
    ya = jnp.dot(yr_ref[...], wr_ref[...], preferred_element_type=F32)
    yb = jnp.dot(yc_ref[...], wc_ref[...], preferred_element_type=F32)
    o_ref[...] = (ga_ref[...].astype(F32) * ya + gb_ref[...].astype(F32) * yb).astype(BF16)


def _merge(yr, yc, gates, w_ret_o_bf, w_conv_o_bf, tm=512):
    t, d = yr.shape
    resident = lambda: pl.BlockSpec((d, d), lambda i: (0, 0), pipeline_mode=pl.Buffered(1))
    return pl.pallas_call(
        _merge_kernel,
        grid=(t // tm,),
        in_specs=[pl.BlockSpec((tm, d), lambda i: (i, 0)),
                  pl.BlockSpec((tm, d), lambda i: (i, 0)),
                  pl.BlockSpec((tm, d), lambda i: (i, 0)),
                  pl.BlockSpec((tm, d), lambda i: (i, 1)),
                  resident(), resident()],
        out_specs=pl.BlockSpec((tm, d), lambda i: (i, 0)),
        out_shape=jax.ShapeDtypeStruct((t, d), BF16),
        compiler_params=_params("arbitrary"),
        name="merge",
    )(yr, yc, gates, gates, w_ret_o_bf, w_conv_o_bf)


def _mixout_kernel(m_ref, w_ref, x_ref, g1_ref, g_ref, sc_ref, sh_ref, h_ref, xn_ref, *, n_sub):
    tr = m_ref.shape[0] // n_sub
    for s in range(n_sub):
        rows = pl.ds(s * tr, tr)
        mixed = jnp.dot(m_ref[rows, :], w_ref[...], preferred_element_type=F32)
        h = x_ref[rows, :] + g1_ref[...] * mixed
        h_ref[rows, :] = h
        xn_ref[rows, :] = _mod_rmsnorm(h, g_ref[...], sc_ref[...], sh_ref[...]).astype(BF16)


def _mixout(m, w_mix_o_bf, x2d, mod3, norm_g, seq, tm=512, n_sub=2):
    t, d = x2d.shape
    tps = seq // tm
    mod_spec = lambda k: pl.BlockSpec((None, 1, d), lambda i: (i // tps, 0, k))
    return pl.pallas_call(
        functools.partial(_mixout_kernel, n_sub=n_sub),
        grid=(t // tm,),
        in_specs=[pl.BlockSpec((tm, d), lambda i: (i, 0)),
                  pl.BlockSpec((d, d), lambda i: (0, 0), pipeline_mode=pl.Buffered(1)),
                  pl.BlockSpec((tm, d), lambda i: (i, 0)),
                  mod_spec(2), pl.BlockSpec((1, d), lambda i: (0, 0)), mod_spec(4), mod_spec(3)],
        out_specs=[pl.BlockSpec((tm, d), lambda i: (i, 0)), pl.BlockSpec((tm, d), lambda i: (i, 0))],
        out_shape=[jax.ShapeDtypeStruct((t, d), F32), jax.ShapeDtypeStruct((t, d), BF16)],
        compiler_params=_params("arbitrary"),
        name="mixout",
    )(m, w_mix_o_bf, x2d, mod3, norm_g.reshape(1, d), mod3, mod3)


def _ffn_kernel(xn_ref, wu_ref, wg_ref, wfc_ref, bfc_ref, wd_ref, h_hbm, g2_ref, fg_ref,
                o_ref, halo_ref, h_sem, *, tiles_per_seq, n_sub):
    i = pl.program_id(0)
    f = pl.program_id(1)
    tm = o_ref.shape[0]

    @pl.when(jnp.logical_and(i == 0, f == 0))
    def _():
        halo_ref[...] = jnp.zeros_like(halo_ref)

    def h1_copy():
        return pltpu.make_async_copy(h_hbm.at[pl.ds(i * tm, tm), :], o_ref, h_sem)

    def step(first):
        if first:
            h1_copy().start()
        xn = xn_ref[...]
        ts = wu_ref.shape[1] // n_sub
        not_seq_start = i % tiles_per_seq != 0
        col_blocks = [pl.ds(s * ts, ts) for s in range(n_sub)]
        up_gate = [(jnp.dot(xn, wu_ref[:, cols], preferred_element_type=F32),
                    jnp.dot(xn, wg_ref[:, cols], preferred_element_type=F32)) for cols in col_blocks]
        part = None
        for cols, (up, gpre) in zip(col_blocks, up_gate):
            prev = jnp.where(not_seq_start, halo_ref[f, :, cols], 0.0)
            gt = _causal_conv3(gpre, prev, wfc_ref[:, cols]) + bfc_ref[:, cols]
            halo_ref[f, :, cols] = gpre[gpre.shape[0] - HALO_ROWS:, :]
            act = (_silu(gt) * up).astype(BF16)
            down = jnp.dot(act, wd_ref[cols, :], preferred_element_type=F32)
            part = down if part is None else part + down
        if first:
            h1_copy().wait()
        o_ref[...] += g2_ref[...] * part

    pl.when(f == 0)(functools.partial(step, True))
    pl.when(f > 0)(functools.partial(step, False))

    @pl.when(f == pl.num_programs(1) - 1)
    def _():
        h = o_ref[...]
        r = lax.rsqrt(jnp.mean(h * h, axis=-1, keepdims=True) + EPS)
        o_ref[...] = (h * r) * fg_ref[...]


def _ffn(xn2, w_up_bf, w_gate_bf, w_ffconv, b_ffconv, w_down_bf, h1, mod3, final_g, seq,
         tm=1024, tf=512, n_sub=2):
    t, d = xn2.shape
    dff = w_up_bf.shape[1]
    nf = dff // tf
    tps = seq // tm
    kern = functools.partial(_ffn_kernel, tiles_per_seq=tps, n_sub=n_sub)
    return pl.pallas_call(
        kern,
        grid=(t // tm, nf),
        in_specs=[pl.BlockSpec((tm, d), lambda i, f: (i, 0)),
                  pl.BlockSpec((d, tf), lambda i, f: (0, f)),
                  pl.BlockSpec((d, tf), lambda i, f: (0, f)),
                  pl.BlockSpec((CONV_WIDTH, tf), lambda i, f: (0, f)),
                  pl.BlockSpec((1, tf), lambda i, f: (0, f)),
                  pl.BlockSpec((tf, d), lambda i, f: (f, 0)),
                  pl.BlockSpec(memory_space=pl.ANY),
                  pl.BlockSpec((None, 1, d), lambda i, f: (i // tps, 0, 5)),
                  pl.BlockSpec((1, d), lambda i, f: (0, 0))],
        out_specs=pl.BlockSpec((tm, d), lambda i, f: (i, 0)),
        out_shape=jax.ShapeDtypeStruct((t, d), F32),
        scratch_shapes=[pltpu.VMEM((nf, HALO_ROWS, tf), F32), pltpu.SemaphoreType.DMA(())],
        compiler_params=_params("arbitrary", "arbitrary"),
        name="ffn",
    )(xn2, w_up_bf, w_gate_bf, w_ffconv, b_ffconv.reshape(1, dff), w_down_bf, h1, mod3,
      final_g.reshape(1, d))


def kernel(x, c, positions, norm1_g, norm2_g, w_ada, b_ada, w_in, b_gate, w_sc, w_ret_o, w_conv_o,
           w_mix_o, w_up, w_gate, w_ffconv, b_ffconv, w_down, final_g):
    batch, seq, d = x.shape
    depth = w_in.shape[0]
    assert depth == 1, "final-norm fusion assumes a single layer"
    t = batch * seq
    h = x.reshape(t, d)
    cos, sin = _rope_tables(positions)
    for l in range(depth):
        mod3 = _ada(c, w_ada[l], b_ada[l]).reshape(batch, 1, 6 * d)
        xn = _norm1(h, norm1_g[l], mod3, seq)
        qkvg = _inproj_a(xn, w_in[l], cos, sin)
        yc = _inproj_c(xn, w_in[l], w_sc[l], seq)
        gates = _inproj_g(xn, w_in[l], b_gate[l])
        yr = _retention(qkvg, batch, seq)
        merged = _merge(yr, yc, gates, w_ret_o[l].astype(BF16), w_conv_o[l].astype(BF16))
        h1, xn2 = _mixout(merged, w_mix_o[l].astype(BF16), h, mod3, norm2_g[l], seq)
        h = _ffn(xn2, w_up[l].astype(BF16), w_gate[l].astype(BF16), w_ffconv[l], b_ffconv[l],
                 w_down[l].astype(BF16), h1, mod3, final_g, seq)
    return h.reshape(batch, seq, d)
```

```python
import functools

import jax
import jax.numpy as jnp
from jax import lax
from jax.experimental import pallas as pl
from jax.experimental.pallas import tpu as pltpu

F32 = jnp.float32
BF16 = jnp.bfloat16

RET_HEADS = 8
RET_HEAD_DIM = 256
D_RET = RET_HEADS * RET_HEAD_DIM
ROT_HALF = RET_HEAD_DIM // 2
CONV_WIDTH = 3
ROPE_BASE = 10000.0
EPS = 1e-6
RET_CHUNK = 256
HALO_ROWS = 8

V7X_VMEM_LIMIT_BYTES = 56 * 1024 * 1024


def _params(*semantics):
    return pltpu.CompilerParams(dimension_semantics=semantics,
                                vmem_limit_bytes=V7X_VMEM_LIMIT_BYTES)


def _silu(v):
    return v * jax.nn.sigmoid(v)


def _mod_rmsnorm(src_ref, rows, gain, scale, shift):
    x = src_ref[rows, :]
    r = lax.rsqrt(jnp.mean(x * x, axis=-1, keepdims=True) + EPS)
    return (src_ref[rows, :] * r) * (gain * (1.0 + scale)) + shift


def _causal_conv3(u, prev, w):
    def taps(a):
        return w[0:1, :] * pltpu.roll(a, 2, 0) + w[1:2, :] * pltpu.roll(a, 1, 0) + w[2:3, :] * a

    head = taps(jnp.concatenate([prev, u[:HALO_ROWS, :]], axis=0))[HALO_ROWS:, :]
    return jnp.concatenate([head, taps(u)[HALO_ROWS:, :]], axis=0)


def _side_cast_specs(weights, steps, step_index):
    specs = [pl.BlockSpec((w.shape[0] // steps, w.shape[1]), lambda *g: (step_index(*g), 0)) for w in weights]
    shapes = [jax.ShapeDtypeStruct(w.shape, BF16) for w in weights]
    return specs, shapes


def _side_cast(in_refs, out_refs):
    for src, dst in zip(in_refs, out_refs):
        dst[...] = src[...].astype(BF16)


def _ada_rope_kernel(c_ref, w_ref, b_ref, pos_ref, inv_ref, mod_ref, cos_ref, sin_ref):
    ca = _silu(c_ref[...]).astype(BF16)
    mod_ref[...] = jnp.dot(ca, w_ref[...].astype(BF16), preferred_element_type=F32) + b_ref[...]
    ang = pos_ref[...].astype(F32) * inv_ref[...]
    cos_ref[...] = jnp.cos(ang)
    sin_ref[...] = jnp.sin(ang)


def _ada_rope(c, w_ada, b_ada, positions, steps=16):
    b, d = c.shape
    n = w_ada.shape[1]
    t = positions.size
    tn, tr = n // steps, t // steps
    inv_freq = ROPE_BASE ** (-jnp.arange(ROT_HALF, dtype=F32) / ROT_HALF)
    return pl.pallas_call(
        _ada_rope_kernel,
        grid=(steps,),
        in_specs=[pl.BlockSpec((b, d), lambda j: (0, 0)),
                  pl.BlockSpec((d, tn), lambda j: (0, j)),
                  pl.BlockSpec((1, tn), lambda j: (0, j)),
                  pl.BlockSpec((tr, 1), lambda j: (j, 0)),
                  pl.BlockSpec((1, ROT_HALF), lambda j: (0, 0))],
        out_specs=[pl.BlockSpec((b, tn), lambda j: (0, j)),
                   pl.BlockSpec((tr, ROT_HALF), lambda j: (j, 0)),
                   pl.BlockSpec((tr, ROT_HALF), lambda j: (j, 0))],
        out_shape=[jax.ShapeDtypeStruct((b, n), F32),
                   jax.ShapeDtypeStruct((t, ROT_HALF), F32),
                   jax.ShapeDtypeStruct((t, ROT_HALF), F32)],
        compiler_params=_params("arbitrary"),
        name="ada_rope",
    )(c, w_ada, b_ada.reshape(1, n), positions.reshape(t, 1), inv_freq.reshape(1, ROT_HALF))


def _norm1_kernel(x_ref, g_ref, sc_ref, sh_ref, o_ref):
    o_ref[...] = _mod_rmsnorm(x_ref, slice(None), g_ref[...], sc_ref[...], sh_ref[...]).astype(BF16)


def _norm1(x2d, norm_g, mod3, seq, tm=1024):
    t, d = x2d.shape
    tps = seq // tm
    return pl.pallas_call(
        _norm1_kernel,
        grid=(t // tm,),
        in_specs=[pl.BlockSpec((tm, d), lambda i: (i, 0)),
                  pl.BlockSpec((1, d), lambda i: (0, 0)),
                  pl.BlockSpec((None, 1, d), lambda i: (i // tps, 0, 1)),
                  pl.BlockSpec((None, 1, d), lambda i: (i // tps, 0, 0))],
        out_specs=pl.BlockSpec((tm, d), lambda i: (i, 0)),
        out_shape=jax.ShapeDtypeStruct((t, d), BF16),
        compiler_params=_params("arbitrary"),
        name="norm1",
    )(x2d, norm_g.reshape(1, d), mod3, mod3)


def _inproj_a_kernel(xn_ref, w_ref, cos_ref, sin_ref, *rest, q_tiles, rot_tiles, gate_tile0, n_sub, n_side):
    side_in, (o_ref, *side_out, wbf_ref) = rest[:n_side], rest[n_side:]
    j = pl.program_id(0)

    @pl.when(pl.program_id(1) == 0)
    def _():
        wbf_ref[...] = w_ref[...].astype(BF16)

    _side_cast(side_in, side_out)
    heads = o_ref.shape[0]
    tr = xn_ref.shape[0] // n_sub

    def proj(rows):
        return jnp.dot(xn_ref[rows, :], wbf_ref[...], preferred_element_type=F32)

    @pl.when(j < rot_tiles)
    def _():
        scale = jnp.where(j < q_tiles, RET_HEAD_DIM ** -0.5, 1.0).astype(F32)
        for s in range(n_sub):
            rows = pl.ds(s * tr, tr)
            acc = proj(rows)
            cos = cos_ref[rows, :] * scale
            sin = sin_ref[rows, :] * scale
            for hh in range(heads):
                lo = hh * RET_HEAD_DIM
                x1 = acc[:, lo:lo + ROT_HALF]
                x2 = acc[:, lo + ROT_HALF:lo + RET_HEAD_DIM]
                o_ref[hh, rows, 0:ROT_HALF] = (x1 * cos - x2 * sin).astype(BF16)
                o_ref[hh, rows, ROT_HALF:RET_HEAD_DIM] = (x2 * cos + x1 * sin).astype(BF16)

    def plain(fn):
        for s in range(n_sub):
            rows = pl.ds(s * tr, tr)
            res = fn(proj(rows)).astype(BF16)
            for hh in range(heads):
                o_ref[hh, rows, :] = res[:, hh * RET_HEAD_DIM:(hh + 1) * RET_HEAD_DIM]

    pl.when(jnp.logical_and(j >= rot_tiles, j < gate_tile0))(functools.partial(plain, lambda a: a))
    pl.when(j >= gate_tile0)(functools.partial(plain, _silu))


def _inproj_a(xn, w_in, cos, sin, side_weights, tm=2048, tn=1024, n_sub=4):
    t, d = xn.shape
    n = 4 * D_RET
    heads = tn // RET_HEAD_DIM
    n_i = t // tm
    side_specs, side_shapes = _side_cast_specs(side_weights, (n // tn) * n_i, lambda j, i: j * n_i + i)
    kern = functools.partial(_inproj_a_kernel, q_tiles=D_RET // tn, rot_tiles=2 * D_RET // tn,
                             gate_tile0=3 * D_RET // tn, n_sub=n_sub, n_side=len(side_weights))
    return pl.pallas_call(
        kern,
        grid=(n // tn, n_i),
        in_specs=[pl.BlockSpec((tm, d), lambda j, i: (i, 0)),
                  pl.BlockSpec((d, tn), lambda j, i: (0, j)),
                  pl.BlockSpec((tm, ROT_HALF), lambda j, i: (i, 0)),
                  pl.BlockSpec((tm, ROT_HALF), lambda j, i: (i, 0))] + side_specs,
        out_specs=[pl.BlockSpec((heads, tm, RET_HEAD_DIM), lambda j, i: (j, i, 0))] + side_specs,
        out_shape=[jax.ShapeDtypeStruct((n // RET_HEAD_DIM, t, RET_HEAD_DIM), BF16)] + side_shapes,
        scratch_shapes=[pltpu.VMEM((d, tn), BF16)],
        compiler_params=_params("arbitrary", "arbitrary"),
        name="inproj_a",
    )(xn, w_in, cos, sin, *side_weights)


def _inproj_c_kernel(xn_ref, wb_ref, wc_ref, wx_ref, wsc_ref, side_ref, o_ref, side_out_ref,
                     wb_bf, wc_bf, wx_bf, halo_ref, *, tiles_per_seq):
    j = pl.program_id(0)
    i = pl.program_id(1)

    @pl.when(jnp.logical_and(i == 0, j == 0))
    def _():
        halo_ref[...] = jnp.zeros_like(halo_ref)

    @pl.when(i == 0)
    def _():
        wb_bf[...] = wb_ref[...].astype(BF16)
        wc_bf[...] = wc_ref[...].astype(BF16)
        wx_bf[...] = wx_ref[...].astype(BF16)

    _side_cast([side_ref], [side_out_ref])
    xn = xn_ref[...]
    cg = jnp.dot(xn, wc_bf[...], preferred_element_type=F32)
    xc = jnp.dot(xn, wx_bf[...], preferred_element_type=F32)
    u = cg * xc
    prev = jnp.where(i % tiles_per_seq == 0, 0.0, halo_ref[...])
    conv = _causal_conv3(u, prev, wsc_ref[...])
    halo_ref[...] = u[u.shape[0] - HALO_ROWS:, :]
    bg = jnp.dot(xn, wb_bf[...], preferred_element_type=F32)
    o_ref[...] = (bg * conv).astype(BF16)


def _inproj_c(xn, w_in, w_sc, side_weight, seq, tm=1024, tc=512):
    t, d = xn.shape
    nc = d // tc
    n_i = t // tm
    first = 4 * D_RET // tc
    kern = functools.partial(_inproj_c_kernel, tiles_per_seq=seq // tm)
    wspec = lambda sec: pl.BlockSpec((d, tc), lambda j, i: (0, first + sec * nc + j))
    side_specs, side_shapes = _side_cast_specs([side_weight], nc * n_i, lambda j, i: j * n_i + i)
    return pl.pallas_call(
        kern,
        grid=(nc, n_i),
        in_specs=[pl.BlockSpec((tm, d), lambda j, i: (i, 0)),
                  wspec(0), wspec(1), wspec(2),
                  pl.BlockSpec((CONV_WIDTH, tc), lambda j, i: (0, j))] + side_specs,
        out_specs=[pl.BlockSpec((tm, tc), lambda j, i: (i, j))] + side_specs,
        out_shape=[jax.ShapeDtypeStruct((t, d), BF16)] + side_shapes,
        scratch_shapes=[pltpu.VMEM((d, tc), BF16)] * 3 + [pltpu.VMEM((HALO_ROWS, tc), F32)],
        compiler_params=_params("arbitrary", "arbitrary"),
        name="inproj_c",
    )(xn, w_in, w_in, w_in, w_sc, side_weight)


def _inproj_g_kernel(xn_ref, w_ref, b_ref, side_ref, o_ref, side_out_ref, wbf_ref, *, n_sub):
    @pl.when(pl.program_id(1) == 0)
    def _():
        wbf_ref[...] = w_ref[...].astype(BF16)

    _side_cast([side_ref], [side_out_ref])
    tr = xn_ref.shape[0] // n_sub
    for s in range(n_sub):
        rows = pl.ds(s * tr, tr)
        logits = jnp.dot(xn_ref[rows, :], wbf_ref[...], preferred_element_type=F32) + b_ref[...]
        o_ref[rows, :] = jax.nn.sigmoid(logits).astype(BF16)


def _inproj_g(xn, w_in, b_gate, side_weight, tm=2048, tn=1024, n_sub=4):
    t, d = xn.shape
    n = b_gate.shape[0]
    n_i = t // tm
    first = (w_in.shape[1] - n) // tn
    side_specs, side_shapes = _side_cast_specs([side_weight], (n // tn) * n_i, lambda j, i: j * n_i + i)
    return pl.pallas_call(
        functools.partial(_inproj_g_kernel, n_sub=n_sub),
        grid=(n // tn, n_i),
        in_specs=[pl.BlockSpec((tm, d), lambda j, i: (i, 0)),
                  pl.BlockSpec((d, tn), lambda j, i: (0, first + j)),
                  pl.BlockSpec((1, tn), lambda j, i: (0, j))] + side_specs,
        out_specs=[pl.BlockSpec((tm, tn), lambda j, i: (i, j))] + side_specs,
        out_shape=[jax.ShapeDtypeStruct((t, n), BF16)] + side_shapes,
        scratch_shapes=[pltpu.VMEM((d, tn), BF16)],
        compiler_params=_params("arbitrary", "arbitrary"),
        name="inproj_g",
    )(xn, w_in, b_gate.reshape(1, n), side_weight)


def _ret_kernel(q_ref, k_ref, v_ref, gs_ref, o_ref):
    heads = q_ref.shape[0]
    first = pl.program_id(1) * heads
    prepared = [_ret_scores_and_states(first + hh, q_ref.at[hh], k_ref.at[hh], v_ref.at[hh])
                for hh in range(heads)]
    for hh, prep in enumerate(prepared):
        cols = pl.ds(hh * RET_HEAD_DIM, RET_HEAD_DIM)
        _ret_outputs(prep, q_ref.at[hh], v_ref.at[hh], gs_ref.at[hh], o_ref.at[:, cols])


def _ret_scores_and_states(head_index, q_ref, k_ref, v_ref):
    c = RET_CHUNK
    n_chunks = q_ref.shape[0] // c
    head = jnp.full((1, RET_HEAD_DIM), head_index, jnp.int32).astype(F32)
    log_gamma = jnp.log(1.0 - jnp.exp2(-5.0 - head))
    row = lax.broadcasted_iota(jnp.int32, (c, c), 0).astype(F32)
    col = lax.broadcasted_iota(jnp.int32, (c, c), 1).astype(F32)
    diff = row - col
    decay = jnp.where(diff >= 0, jnp.exp(log_gamma * jnp.maximum(diff, 0.0)), 0.0)
    q_dec = jnp.exp(log_gamma * (row + 1.0))
    k_dec = jnp.exp(log_gamma * (c - 1.0 - row))
    chunk_dec = jnp.exp(log_gamma * float(c))
    chunks = [pl.ds(n * c, c) for n in range(n_chunks)]

    kvs = []
    for rows in chunks[:-1]:
        kd = (k_ref[rows, :].astype(F32) * k_dec).astype(BF16)
        kvs.append(lax.dot_general(kd, v_ref[rows, :], (((0,), (0,)), ((), ())),
                                   preferred_element_type=F32))
    scores = [lax.dot_general(q_ref[rows, :], k_ref[rows, :], (((1,), (1,)), ((), ())),
                              preferred_element_type=F32) for rows in chunks]
    states = [None]
    for kv in kvs:
        states.append(kv if states[-1] is None else chunk_dec * states[-1] + kv)
    return chunks, scores, states, decay, q_dec


def _ret_outputs(prep, q_ref, v_ref, gs_ref, o_ref):
    chunks, scores, states, decay, q_dec = prep
    for rows, s, state in zip(chunks, scores, states):
        y = jnp.dot((s * decay).astype(BF16), v_ref[rows, :], preferred_element_type=F32)
        if state is not None:
            y = y + q_dec * jnp.dot(q_ref[rows, :], state.astype(BF16), preferred_element_type=F32)
        yn = y * lax.rsqrt(jnp.mean(y * y, axis=-1, keepdims=True) + EPS)
        o_ref[rows, :] = (yn * gs_ref[rows, :].astype(F32)).astype(BF16)


def _retention(qkvg, batch, seq, heads_per_step=2):
    t = qkvg.shape[1]
    groups = RET_HEADS // heads_per_step
    spec = lambda sec: pl.BlockSpec((heads_per_step, seq, RET_HEAD_DIM),
                                    lambda b, g: (sec * groups + g, b, 0))
    return pl.pallas_call(
        _ret_kernel,
        grid=(batch, groups),
        in_specs=[spec(0), spec(1), spec(2), spec(3)],
        out_specs=pl.BlockSpec((seq, heads_per_step * RET_HEAD_DIM), lambda b, g: (b, g)),
        out_shape=jax.ShapeDtypeStruct((t, D_RET), BF16),
        compiler_params=_params("arbitrary", "arbitrary"),
        name="ret",
    )(qkvg, qkvg, qkvg, qkvg)


def _merge_kernel(yr_ref, yc_ref, ga_ref, gb_ref, wr_ref, wc_ref, side_ref, o_ref, side_out_ref):
    _side_cast([side_ref], [side_out_ref])
    ya = jnp.dot(yr_ref[...], wr_ref[...], preferred_element_type=F32)
    yb = jnp.dot(yc_ref[...], wc_ref[...], preferred_element_type=F32)
    o_ref[...] = (ga_ref[...].astype(F32) * ya + gb_ref[...].astype(F32) * yb).astype(BF16)


def _merge(yr, yc, gates, w_ret_o_bf, w_conv_o_bf, side_weight, tm=512):
    t, d = yr.shape
    resident = lambda: pl.BlockSpec((d, d), lambda i: (0, 0), pipeline_mode=pl.Buffered(1))
    side_specs, side_shapes = _side_cast_specs([side_weight], t // tm, lambda i: i)
    return pl.pallas_call(
        _merge_kernel,
        grid=(t // tm,),
        in_specs=[pl.BlockSpec((tm, d), lambda i: (i, 0)),
                  pl.BlockSpec((tm, d), lambda i: (i, 0)),
                  pl.BlockSpec((tm, d), lambda i: (i, 0)),
                  pl.BlockSpec((tm, d), lambda i: (i, 1)),
                  resident(), resident()] + side_specs,
        out_specs=[pl.BlockSpec((tm, d), lambda i: (i, 0))] + side_specs,
        out_shape=[jax.ShapeDtypeStruct((t, d), BF16)] + side_shapes,
        compiler_params=_params("arbitrary"),
        name="merge",
    )(yr, yc, gates, gates, w_ret_o_bf, w_conv_o_bf, side_weight)


def _mixout_kernel(m_ref, w_ref, x_ref, g1_ref, g_ref, sc_ref, sh_ref, h_ref, xn_ref, *, n_sub):
    tr = m_ref.shape[0] // n_sub
    for s in range(n_sub):
        rows = pl.ds(s * tr, tr)
        mixed = jnp.dot(m_ref[rows, :], w_ref[...], preferred_element_type=F32)
        h = x_ref[rows, :] + g1_ref[...] * mixed
        h_ref[rows, :] = h
        xn_ref[rows, :] = _mod_rmsnorm(h_ref, rows, g_ref[...], sc_ref[...], sh_ref[...]).astype(BF16)


def _mixout(m, w_mix_o_bf, x2d, mod3, norm_g, seq, tm=512, n_sub=2):
    t, d = x2d.shape
    tps = seq // tm
    mod_spec = lambda k: pl.BlockSpec((None, 1, d), lambda i: (i // tps, 0, k))
    return pl.pallas_call(
        functools.partial(_mixout_kernel, n_sub=n_sub),
        grid=(t // tm,),
        in_specs=[pl.BlockSpec((tm, d), lambda i: (i, 0)),
                  pl.BlockSpec((d, d), lambda i: (0, 0), pipeline_mode=pl.Buffered(1)),
                  pl.BlockSpec((tm, d), lambda i: (i, 0)),
                  mod_spec(2), pl.BlockSpec((1, d), lambda i: (0, 0)), mod_spec(4), mod_spec(3)],
        out_specs=[pl.BlockSpec((tm, d), lambda i: (i, 0)), pl.BlockSpec((tm, d), lambda i: (i, 0))],
        out_shape=[jax.ShapeDtypeStruct((t, d), F32), jax.ShapeDtypeStruct((t, d), BF16)],
        compiler_params=_params("arbitrary"),
        name="mixout",
    )(m, w_mix_o_bf, x2d, mod3, norm_g.reshape(1, d), mod3, mod3)


def _ffn_kernel(xn_ref, wu_ref, wg_ref, wfc_ref, bfc_ref, wd_ref, h_hbm, g2_ref, fg_ref,
                o_ref, halo_ref, h_sem, *, tiles_per_seq, n_sub):
    i = pl.program_id(0)
    f = pl.program_id(1)
    tm = o_ref.shape[0]

    @pl.when(jnp.logical_and(i == 0, f == 0))
    def _():
        halo_ref[...] = jnp.zeros_like(halo_ref)

    def h1_copy():
        return pltpu.make_async_copy(h_hbm.at[pl.ds(i * tm, tm), :], o_ref, h_sem)

    def step(first):
        if first:
            h1_copy().start()
        xn = xn_ref[...]
        ts = wu_ref.shape[1] // n_sub
        not_seq_start = i % tiles_per_seq != 0
        col_blocks = [pl.ds(s * ts, ts) for s in range(n_sub)]
        up_gate = [(jnp.dot(xn, wu_ref[:, cols], preferred_element_type=F32),
                    jnp.dot(xn, wg_ref[:, cols], preferred_element_type=F32)) for cols in col_blocks]
        part = None
        for cols, (up, gpre) in zip(col_blocks, up_gate):
            prev = jnp.where(not_seq_start, halo_ref[f, :, cols], 0.0)
            gt = _causal_conv3(gpre, prev, wfc_ref[:, cols]) + bfc_ref[:, cols]
            halo_ref[f, :, cols] = gpre[gpre.shape[0] - HALO_ROWS:, :]
            act = (_silu(gt) * up).astype(BF16)
            down = jnp.dot(act, wd_ref[cols, :], preferred_element_type=F32)
            part = down if part is None else part + down
        if first:
            h1_copy().wait()
        o_ref[...] += g2_ref[...] * part

    pl.when(f == 0)(functools.partial(step, True))
    pl.when(f > 0)(functools.partial(step, False))

    @pl.when(f == pl.num_programs(1) - 1)
    def _():
        r = lax.rsqrt(jnp.mean(jnp.square(o_ref[...]), axis=-1, keepdims=True) + EPS)
        o_ref[...] = (o_ref[...] * r) * fg_ref[...]


def _ffn(xn2, w_up_bf, w_gate_bf, w_ffconv, b_ffconv, w_down_bf, h1, mod3, final_g, seq,
         tm=1024, tf=512, n_sub=2):
    t, d = xn2.shape
    dff = w_up_bf.shape[1]
    nf = dff // tf
    tps = seq // tm
    kern = functools.partial(_ffn_kernel, tiles_per_seq=tps, n_sub=n_sub)
    return pl.pallas_call(
        kern,
        grid=(t // tm, nf),
        in_specs=[pl.BlockSpec((tm, d), lambda i, f: (i, 0)),
                  pl.BlockSpec((d, tf), lambda i, f: (0, f)),
                  pl.BlockSpec((d, tf), lambda i, f: (0, f)),
                  pl.BlockSpec((CONV_WIDTH, tf), lambda i, f: (0, f)),
                  pl.BlockSpec((1, tf), lambda i, f: (0, f)),
                  pl.BlockSpec((tf, d), lambda i, f: (f, 0)),
                  pl.BlockSpec(memory_space=pl.ANY),
                  pl.BlockSpec((None, 1, d), lambda i, f: (i // tps, 0, 5)),
                  pl.BlockSpec((1, d), lambda i, f: (0, 0))],
        out_specs=pl.BlockSpec((tm, d), lambda i, f: (i, 0)),
        out_shape=jax.ShapeDtypeStruct((t, d), F32),
        scratch_shapes=[pltpu.VMEM((nf, HALO_ROWS, tf), F32), pltpu.SemaphoreType.DMA(())],
        compiler_params=_params("arbitrary", "arbitrary"),
        name="ffn",
    )(xn2, w_up_bf, w_gate_bf, w_ffconv, b_ffconv.reshape(1, dff), w_down_bf, h1, mod3,
      final_g.reshape(1, d))


def kernel(x, c, positions, norm1_g, norm2_g, w_ada, b_ada, w_in, b_gate, w_sc, w_ret_o, w_conv_o,
           w_mix_o, w_up, w_gate, w_ffconv, b_ffconv, w_down, final_g):
    batch, seq, d = x.shape
    depth = w_in.shape[0]
    assert depth == 1, "final-norm fusion assumes a single layer"
    t = batch * seq
    h = x.reshape(t, d)
    for l in range(depth):
        mod, cos, sin = _ada_rope(c, w_ada[l], b_ada[l], positions)
        mod3 = mod.reshape(batch, 1, 6 * d)
        xn = _norm1(h, norm1_g[l], mod3, seq)
        qkvg, w_ret_o_bf, w_conv_o_bf, w_mix_o_bf = _inproj_a(
            xn, w_in[l], cos, sin, [w_ret_o[l], w_conv_o[l], w_mix_o[l]])
        yc, w_gate_bf = _inproj_c(xn, w_in[l], w_sc[l], w_gate[l], seq)
        gates, w_up_bf = _inproj_g(xn, w_in[l], b_gate[l], w_up[l])
        yr = _retention(qkvg, batch, seq)
        merged, w_down_bf = _merge(yr, yc, gates, w_ret_o_bf, w_conv_o_bf, w_down[l])
        h1, xn2 = _mixout(merged, w_mix_o_bf, h, mod3, norm2_g[l], seq)
        h = _ffn(xn2, w_up_bf, w_gate_bf, w_ffconv[l], b_ffconv[l], w_down_bf, h1, mod3, final_g, seq)
    return h.reshape(batch, seq, d)
```

```python
import functools

import jax
import jax.numpy as jnp
from jax import lax
from jax.experimental import pallas as pl
from jax.experimental.pallas import tpu as pltpu

F32 = jnp.float32
BF16 = jnp.bfloat16

RET_HEADS = 8
RET_HEAD_DIM = 256
D_RET = RET_HEADS * RET_HEAD_DIM
ROT_HALF = RET_HEAD_DIM // 2
CONV_WIDTH = 3
ROPE_BASE = 10000.0
EPS = 1e-6
RET_CHUNK = 256
HALO_ROWS = 8

V7X_VMEM_LIMIT_BYTES = 56 * 1024 * 1024


def _params(*semantics):
    return pltpu.CompilerParams(dimension_semantics=semantics,
                                vmem_limit_bytes=V7X_VMEM_LIMIT_BYTES)


def _silu(v):
    return v * jax.nn.sigmoid(v)


def _mod_rmsnorm(src_ref, rows, gain, scale, shift):
    x = src_ref[rows, :]
    r = lax.rsqrt(jnp.mean(x * x, axis=-1, keepdims=True) + EPS)
    return (src_ref[rows, :] * r) * (gain * (1.0 + scale)) + shift


def _causal_conv3(u, prev, w):
    def taps(a):
        return w[0:1, :] * pltpu.roll(a, 2, 0) + w[1:2, :] * pltpu.roll(a, 1, 0) + w[2:3, :] * a

    head = taps(jnp.concatenate([prev, u[:HALO_ROWS, :]], axis=0))[HALO_ROWS:, :]
    return jnp.concatenate([head, taps(u)[HALO_ROWS:, :]], axis=0)


def _side_cast_specs(weights, steps, step_index):
    specs = [pl.BlockSpec((w.shape[0] // steps, w.shape[1]), lambda *g: (step_index(*g), 0)) for w in weights]
    shapes = [jax.ShapeDtypeStruct(w.shape, BF16) for w in weights]
    return specs, shapes


def _side_cast(in_refs, out_refs):
    for src, dst in zip(in_refs, out_refs):
        dst[...] = src[...].astype(BF16)


def _rope_angles(positions):
    inv_freq = ROPE_BASE ** (-jnp.arange(ROT_HALF, dtype=F32) / ROT_HALF)
    return positions.reshape(positions.size, 1), inv_freq.reshape(1, ROT_HALF)


def _ada_rope_kernel(c_ref, w_ref, b_ref, pos_ref, inv_ref, mod_ref, cos_ref, sin_ref):
    ca = _silu(c_ref[...]).astype(BF16)
    mod_ref[...] = jnp.dot(ca, w_ref[...].astype(BF16), preferred_element_type=F32) + b_ref[...]
    ang = pos_ref[...].astype(F32) * inv_ref[...]
    cos_ref[...] = jnp.cos(ang)
    sin_ref[...] = jnp.sin(ang)


def _ada_rope(c, w_ada, b_ada, positions, steps=16):
    b, d = c.shape
    n = w_ada.shape[1]
    t = positions.size
    tn, tr = n // steps, t // steps
    pos, inv_freq = _rope_angles(positions)
    return pl.pallas_call(
        _ada_rope_kernel,
        grid=(steps,),
        in_specs=[pl.BlockSpec((b, d), lambda j: (0, 0)),
                  pl.BlockSpec((d, tn), lambda j: (0, j)),
                  pl.BlockSpec((1, tn), lambda j: (0, j)),
                  pl.BlockSpec((tr, 1), lambda j: (j, 0)),
                  pl.BlockSpec((1, ROT_HALF), lambda j: (0, 0))],
        out_specs=[pl.BlockSpec((b, tn), lambda j: (0, j)),
                   pl.BlockSpec((tr, ROT_HALF), lambda j: (j, 0)),
                   pl.BlockSpec((tr, ROT_HALF), lambda j: (j, 0))],
        out_shape=[jax.ShapeDtypeStruct((b, n), F32),
                   jax.ShapeDtypeStruct((t, ROT_HALF), F32),
                   jax.ShapeDtypeStruct((t, ROT_HALF), F32)],
        compiler_params=_params("arbitrary"),
        name="ada_rope",
    )(c, w_ada, b_ada.reshape(1, n), pos, inv_freq)


def _norm1_kernel(x_ref, g_ref, sc_ref, sh_ref, o_ref):
    o_ref[...] = _mod_rmsnorm(x_ref, slice(None), g_ref[...], sc_ref[...], sh_ref[...]).astype(BF16)


def _norm1(x2d, norm_g, mod3, seq, tm=1024):
    t, d = x2d.shape
    tps = seq // tm
    return pl.pallas_call(
        _norm1_kernel,
        grid=(t // tm,),
        in_specs=[pl.BlockSpec((tm, d), lambda i: (i, 0)),
                  pl.BlockSpec((1, d), lambda i: (0, 0)),
                  pl.BlockSpec((None, 1, d), lambda i: (i // tps, 0, 1)),
                  pl.BlockSpec((None, 1, d), lambda i: (i // tps, 0, 0))],
        out_specs=pl.BlockSpec((tm, d), lambda i: (i, 0)),
        out_shape=jax.ShapeDtypeStruct((t, d), BF16),
        compiler_params=_params("arbitrary"),
        name="norm1",
    )(x2d, norm_g.reshape(1, d), mod3, mod3)


def _inproj_a_kernel(xn_ref, w_ref, cos_ref, sin_ref, *rest, q_tiles, rot_tiles, gate_tile0, n_sub, n_side):
    side_in, (o_ref, *side_out, wbf_ref) = rest[:n_side], rest[n_side:]
    j = pl.program_id(0)

    @pl.when(pl.program_id(1) == 0)
    def _():
        wbf_ref[...] = w_ref[...].astype(BF16)

    _side_cast(side_in, side_out)
    heads = o_ref.shape[0]
    tr = xn_ref.shape[0] // n_sub

    def proj(rows):
        return jnp.dot(xn_ref[rows, :], wbf_ref[...], preferred_element_type=F32)

    @pl.when(j < rot_tiles)
    def _():
        scale = jnp.where(j < q_tiles, RET_HEAD_DIM ** -0.5, 1.0).astype(F32)
        for s in range(n_sub):
            rows = pl.ds(s * tr, tr)
            acc = proj(rows)
            cos = cos_ref[rows, :] * scale
            sin = sin_ref[rows, :] * scale
            for hh in range(heads):
                lo = hh * RET_HEAD_DIM
                x1 = acc[:, lo:lo + ROT_HALF]
                x2 = acc[:, lo + ROT_HALF:lo + RET_HEAD_DIM]
                o_ref[hh, rows, 0:ROT_HALF] = (x1 * cos - x2 * sin).astype(BF16)
                o_ref[hh, rows, ROT_HALF:RET_HEAD_DIM] = (x2 * cos + x1 * sin).astype(BF16)

    def plain(fn):
        for s in range(n_sub):
            rows = pl.ds(s * tr, tr)
            res = fn(proj(rows)).astype(BF16)
            for hh in range(heads):
                o_ref[hh, rows, :] = res[:, hh * RET_HEAD_DIM:(hh + 1) * RET_HEAD_DIM]

    pl.when(jnp.logical_and(j >= rot_tiles, j < gate_tile0))(functools.partial(plain, lambda a: a))
    pl.when(j >= gate_tile0)(functools.partial(plain, _silu))


def _inproj_a(xn, w_in, cos, sin, side_weights, tm=2048, tn=1024, n_sub=4):
    t, d = xn.shape
    n = 4 * D_RET
    heads = tn // RET_HEAD_DIM
    n_i = t // tm
    side_specs, side_shapes = _side_cast_specs(side_weights, (n // tn) * n_i, lambda j, i: j * n_i + i)
    kern = functools.partial(_inproj_a_kernel, q_tiles=D_RET // tn, rot_tiles=2 * D_RET // tn,
                             gate_tile0=3 * D_RET // tn, n_sub=n_sub, n_side=len(side_weights))
    return pl.pallas_call(
        kern,
        grid=(n // tn, n_i),
        in_specs=[pl.BlockSpec((tm, d), lambda j, i: (i, 0)),
                  pl.BlockSpec((d, tn), lambda j, i: (0, j)),
                  pl.BlockSpec((tm, ROT_HALF), lambda j, i: (i, 0)),
                  pl.BlockSpec((tm, ROT_HALF), lambda j, i: (i, 0))] + side_specs,
        out_specs=[pl.BlockSpec((heads, tm, RET_HEAD_DIM), lambda j, i: (j, i, 0))] + side_specs,
        out_shape=[jax.ShapeDtypeStruct((n // RET_HEAD_DIM, t, RET_HEAD_DIM), BF16)] + side_shapes,
        scratch_shapes=[pltpu.VMEM((d, tn), BF16)],
        compiler_params=_params("arbitrary", "arbitrary"),
        name="inproj_a",
    )(xn, w_in, cos, sin, *side_weights)


def _inproj_c_kernel(xn_ref, wb_ref, wc_ref, wx_ref, wsc_ref, side_ref, o_ref, side_out_ref,
                     wb_bf, wc_bf, wx_bf, halo_ref, *, tiles_per_seq):
    j = pl.program_id(0)
    i = pl.program_id(1)

    @pl.when(jnp.logical_and(i == 0, j == 0))
    def _():
        halo_ref[...] = jnp.zeros_like(halo_ref)

    @pl.when(i == 0)
    def _():
        wb_bf[...] = wb_ref[...].astype(BF16)
        wc_bf[...] = wc_ref[...].astype(BF16)
        wx_bf[...] = wx_ref[...].astype(BF16)

    _side_cast([side_ref], [side_out_ref])
    xn = xn_ref[...]
    cg = jnp.dot(xn, wc_bf[...], preferred_element_type=F32)
    xc = jnp.dot(xn, wx_bf[...], preferred_element_type=F32)
    u = cg * xc
    prev = jnp.where(i % tiles_per_seq == 0, 0.0, halo_ref[...])
    conv = _causal_conv3(u, prev, wsc_ref[...])
    halo_ref[...] = u[u.shape[0] - HALO_ROWS:, :]
    bg = jnp.dot(xn, wb_bf[...], preferred_element_type=F32)
    o_ref[...] = (bg * conv).astype(BF16)


def _inproj_c(xn, w_in, w_sc, side_weight, seq, tm=1024, tc=512):
    t, d = xn.shape
    nc = d // tc
    n_i = t // tm
    first = 4 * D_RET // tc
    kern = functools.partial(_inproj_c_kernel, tiles_per_seq=seq // tm)
    wspec = lambda sec: pl.BlockSpec((d, tc), lambda j, i: (0, first + sec * nc + j))
    side_specs, side_shapes = _side_cast_specs([side_weight], nc * n_i, lambda j, i: j * n_i + i)
    return pl.pallas_call(
        kern,
        grid=(nc, n_i),
        in_specs=[pl.BlockSpec((tm, d), lambda j, i: (i, 0)),
                  wspec(0), wspec(1), wspec(2),
                  pl.BlockSpec((CONV_WIDTH, tc), lambda j, i: (0, j))] + side_specs,
        out_specs=[pl.BlockSpec((tm, tc), lambda j, i: (i, j))] + side_specs,
        out_shape=[jax.ShapeDtypeStruct((t, d), BF16)] + side_shapes,
        scratch_shapes=[pltpu.VMEM((d, tc), BF16)] * 3 + [pltpu.VMEM((HALO_ROWS, tc), F32)],
        compiler_params=_params("arbitrary", "arbitrary"),
        name="inproj_c",
    )(xn, w_in, w_in, w_in, w_sc, side_weight)


def _inproj_g_kernel(xn_ref, w_ref, b_ref, side_ref, o_ref, side_out_ref, wbf_ref, *, n_sub):
    @pl.when(pl.program_id(1) == 0)
    def _():
        wbf_ref[...] = w_ref[...].astype(BF16)

    _side_cast([side_ref], [side_out_ref])
    tr = xn_ref.shape[0] // n_sub
    for s in range(n_sub):
        rows = pl.ds(s * tr, tr)
        logits = jnp.dot(xn_ref[rows, :], wbf_ref[...], preferred_element_type=F32) + b_ref[...]
        o_ref[rows, :] = jax.nn.sigmoid(logits).astype(BF16)


def _inproj_g(xn, w_in, b_gate, side_weight, tm=2048, tn=1024, n_sub=4):
    t, d = xn.shape
    n = b_gate.shape[0]
    n_i = t // tm
    first = (w_in.shape[1] - n) // tn
    side_specs, side_shapes = _side_cast_specs([side_weight], (n // tn) * n_i, lambda j, i: j * n_i + i)
    return pl.pallas_call(
        functools.partial(_inproj_g_kernel, n_sub=n_sub),
        grid=(n // tn, n_i),
        in_specs=[pl.BlockSpec((tm, d), lambda j, i: (i, 0)),
                  pl.BlockSpec((d, tn), lambda j, i: (0, first + j)),
                  pl.BlockSpec((1, tn), lambda j, i: (0, j))] + side_specs,
        out_specs=[pl.BlockSpec((tm, tn), lambda j, i: (i, j))] + side_specs,
        out_shape=[jax.ShapeDtypeStruct((t, n), BF16)] + side_shapes,
        scratch_shapes=[pltpu.VMEM((d, tn), BF16)],
        compiler_params=_params("arbitrary", "arbitrary"),
        name="inproj_g",
    )(xn, w_in, b_gate.reshape(1, n), side_weight)


def _ret_kernel(q_ref, k_ref, v_ref, gs_ref, o_ref):
    heads = q_ref.shape[0]
    first = pl.program_id(1) * heads

    def prepare(hh):
        return _ret_scores_and_states(first + hh, q_ref.at[hh], k_ref.at[hh], v_ref.at[hh])

    prep = prepare(0)
    for hh in range(heads):
        next_prep = prepare(hh + 1) if hh + 1 < heads else None
        cols = pl.ds(hh * RET_HEAD_DIM, RET_HEAD_DIM)
        _ret_outputs(prep, q_ref.at[hh], v_ref.at[hh], gs_ref.at[hh], o_ref.at[:, cols])
        prep = next_prep


def _ret_scores_and_states(head_index, q_ref, k_ref, v_ref):
    c = RET_CHUNK
    n_chunks = q_ref.shape[0] // c
    head = jnp.full((1, RET_HEAD_DIM), head_index, jnp.int32).astype(F32)
    log_gamma = jnp.log(1.0 - jnp.exp2(-5.0 - head))
    row = lax.broadcasted_iota(jnp.int32, (c, c), 0).astype(F32)
    col = lax.broadcasted_iota(jnp.int32, (c, c), 1).astype(F32)
    diff = row - col
    decay = jnp.where(diff >= 0, jnp.exp(log_gamma * jnp.maximum(diff, 0.0)), 0.0)
    q_dec = jnp.exp(log_gamma * (row + 1.0))
    k_dec = jnp.exp(log_gamma * (c - 1.0 - row))
    chunk_dec = jnp.exp(log_gamma * float(c))
    chunks = [pl.ds(n * c, c) for n in range(n_chunks)]

    kvs = []
    for rows in chunks[:-1]:
        kd = (k_ref[rows, :].astype(F32) * k_dec).astype(BF16)
        kvs.append(lax.dot_general(kd, v_ref[rows, :], (((0,), (0,)), ((), ())),
                                   preferred_element_type=F32))
    scores = [lax.dot_general(q_ref[rows, :], k_ref[rows, :], (((1,), (1,)), ((), ())),
                              preferred_element_type=F32) for rows in chunks]
    states = [None]
    for kv in kvs:
        states.append(kv if states[-1] is None else chunk_dec * states[-1] + kv)
    return chunks, scores, states, decay, q_dec


def _ret_outputs(prep, q_ref, v_ref, gs_ref, o_ref):
    chunks, scores, states, decay, q_dec = prep
    for rows, s, state in zip(chunks, scores, states):
        y = jnp.dot((s * decay).astype(BF16), v_ref[rows, :], preferred_element_type=F32)
        if state is not None:
            y = y + q_dec * jnp.dot(q_ref[rows, :], state.astype(BF16), preferred_element_type=F32)
        yn = y * lax.rsqrt(jnp.mean(y * y, axis=-1, keepdims=True) + EPS)
        o_ref[rows, :] = (yn * gs_ref[rows, :].astype(F32)).astype(BF16)


def _retention(qkvg, batch, seq, heads_per_step=4):
    t = qkvg.shape[1]
    groups = RET_HEADS // heads_per_step
    spec = lambda sec: pl.BlockSpec((heads_per_step, seq, RET_HEAD_DIM),
                                    lambda b, g: (sec * groups + g, b, 0))
    return pl.pallas_call(
        _ret_kernel,
        grid=(batch, groups),
        in_specs=[spec(0), spec(1), spec(2), spec(3)],
        out_specs=pl.BlockSpec((seq, heads_per_step * RET_HEAD_DIM), lambda b, g: (b, g)),
        out_shape=jax.ShapeDtypeStruct((t, D_RET), BF16),
        compiler_params=_params("arbitrary", "arbitrary"),
        name="ret",
    )(qkvg, qkvg, qkvg, qkvg)


def _merge_kernel(yr_ref, yc_ref, ga_ref, gb_ref, wr_ref, wc_ref, side_ref, o_ref, side_out_ref):
    _side_cast([side_ref], [side_out_ref])
    ya = jnp.dot(yr_ref[...], wr_ref[...], preferred_element_type=F32)
    yb = jnp.dot(yc_ref[...], wc_ref[...], preferred_element_type=F32)
    o_ref[...] = (ga_ref[...].astype(F32) * ya + gb_ref[...].astype(F32) * yb).astype(BF16)


def _merge(yr, yc, gates, w_ret_o_bf, w_conv_o_bf, side_weight, tm=512):
    t, d = yr.shape
    resident = lambda: pl.BlockSpec((d, d), lambda i: (0, 0), pipeline_mode=pl.Buffered(1))
    side_specs, side_shapes = _side_cast_specs([side_weight], t // tm, lambda i: i)
    return pl.pallas_call(
        _merge_kernel,
        grid=(t // tm,),
        in_specs=[pl.BlockSpec((tm, d), lambda i: (i, 0)),
                  pl.BlockSpec((tm, d), lambda i: (i, 0)),
                  pl.BlockSpec((tm, d), lambda i: (i, 0)),
                  pl.BlockSpec((tm, d), lambda i: (i, 1)),
                  resident(), resident()] + side_specs,
        out_specs=[pl.BlockSpec((tm, d), lambda i: (i, 0))] + side_specs,
        out_shape=[jax.ShapeDtypeStruct((t, d), BF16)] + side_shapes,
        compiler_params=_params("arbitrary"),
        name="merge",
    )(yr, yc, gates, gates, w_ret_o_bf, w_conv_o_bf, side_weight)


def _mixout_kernel(m_ref, w_ref, x_ref, g1_ref, g_ref, sc_ref, sh_ref, h_ref, xn_ref, *, n_sub):
    tr = m_ref.shape[0] // n_sub
    for s in range(n_sub):
        rows = pl.ds(s * tr, tr)
        mixed = jnp.dot(m_ref[rows, :], w_ref[...], preferred_element_type=F32)
        h = x_ref[rows, :] + g1_ref[...] * mixed
        h_ref[rows, :] = h
        xn_ref[rows, :] = _mod_rmsnorm(h_ref, rows, g_ref[...], sc_ref[...], sh_ref[...]).astype(BF16)


def _mixout(m, w_mix_o_bf, x2d, mod3, norm_g, seq, tm=512, n_sub=2):
    t, d = x2d.shape
    tps = seq // tm
    mod_spec = lambda k: pl.BlockSpec((None, 1, d), lambda i: (i // tps, 0, k))
    return pl.pallas_call(
        functools.partial(_mixout_kernel, n_sub=n_sub),
        grid=(t // tm,),
        in_specs=[pl.BlockSpec((tm, d), lambda i: (i, 0)),
                  pl.BlockSpec((d, d), lambda i: (0, 0), pipeline_mode=pl.Buffered(1)),
                  pl.BlockSpec((tm, d), lambda i: (i, 0)),
                  mod_spec(2), pl.BlockSpec((1, d), lambda i: (0, 0)), mod_spec(4), mod_spec(3)],
        out_specs=[pl.BlockSpec((tm, d), lambda i: (i, 0)), pl.BlockSpec((tm, d), lambda i: (i, 0))],
        out_shape=[jax.ShapeDtypeStruct((t, d), F32), jax.ShapeDtypeStruct((t, d), BF16)],
        compiler_params=_params("arbitrary"),
        name="mixout",
    )(m, w_mix_o_bf, x2d, mod3, norm_g.reshape(1, d), mod3, mod3)


def _ffn_kernel(xn_ref, wu_ref, wg_ref, wfc_ref, bfc_ref, wd_ref, h_hbm, g2_ref, fg_ref,
                o_ref, halo_ref, h_sem, *, tiles_per_seq, sub_width, last_width):
    i = pl.program_id(0)
    f = pl.program_id(1)
    last = pl.num_programs(1) - 1
    tm = o_ref.shape[0]

    @pl.when(jnp.logical_and(i == 0, f == 0))
    def _():
        halo_ref[...] = jnp.zeros_like(halo_ref)

    def h1_copy():
        return pltpu.make_async_copy(h_hbm.at[pl.ds(i * tm, tm), :], o_ref, h_sem)

    def step(first, width):
        if first:
            h1_copy().start()
        xn = xn_ref[...]
        not_seq_start = i % tiles_per_seq != 0
        col_blocks = [pl.ds(s * sub_width, sub_width) for s in range(width // sub_width)]
        up_gate = [(jnp.dot(xn, wu_ref[:, cols], preferred_element_type=F32),
                    jnp.dot(xn, wg_ref[:, cols], preferred_element_type=F32)) for cols in col_blocks]
        part = None
        for cols, (up, gpre) in zip(col_blocks, up_gate):
            prev = jnp.where(not_seq_start, halo_ref[f, :, cols], 0.0)
            gt = _causal_conv3(gpre, prev, wfc_ref[:, cols]) + bfc_ref[:, cols]
            halo_ref[f, :, cols] = gpre[gpre.shape[0] - HALO_ROWS:, :]
            act = (_silu(gt) * up).astype(BF16)
            down = jnp.dot(act, wd_ref[cols, :], preferred_element_type=F32)
            part = down if part is None else part + down
        if first:
            h1_copy().wait()
        o_ref[...] += g2_ref[...] * part

    tile_width = wu_ref.shape[1]
    pl.when(f == 0)(functools.partial(step, True, tile_width))
    pl.when(jnp.logical_and(f > 0, f < last))(functools.partial(step, False, tile_width))
    pl.when(f == last)(functools.partial(step, False, last_width))

    @pl.when(f == last)
    def _():
        r = lax.rsqrt(jnp.mean(jnp.square(o_ref[...]), axis=-1, keepdims=True) + EPS)
        o_ref[...] = (o_ref[...] * r) * fg_ref[...]


def _ffn(xn2, w_up_bf, w_gate_bf, w_ffconv, b_ffconv, w_down_bf, h1, mod3, final_g, seq,
         tm=1024, tf=768, sub_width=256):
    t, d = xn2.shape
    dff = w_up_bf.shape[1]
    nf = pl.cdiv(dff, tf)
    tps = seq // tm
    assert nf >= 2 and tf % sub_width == 0 and dff % sub_width == 0
    kern = functools.partial(_ffn_kernel, tiles_per_seq=tps, sub_width=sub_width,
                             last_width=dff - (nf - 1) * tf)
    return pl.pallas_call(
        kern,
        grid=(t // tm, nf),
        in_specs=[pl.BlockSpec((tm, d), lambda i, f: (i, 0)),
                  pl.BlockSpec((d, tf), lambda i, f: (0, f)),
                  pl.BlockSpec((d, tf), lambda i, f: (0, f)),
                  pl.BlockSpec((CONV_WIDTH, tf), lambda i, f: (0, f)),
                  pl.BlockSpec((1, tf), lambda i, f: (0, f)),
                  pl.BlockSpec((tf, d), lambda i, f: (f, 0)),
                  pl.BlockSpec(memory_space=pl.ANY),
                  pl.BlockSpec((None, 1, d), lambda i, f: (i // tps, 0, 5)),
                  pl.BlockSpec((1, d), lambda i, f: (0, 0))],
        out_specs=pl.BlockSpec((tm, d), lambda i, f: (i, 0)),
        out_shape=jax.ShapeDtypeStruct((t, d), F32),
        scratch_shapes=[pltpu.VMEM((nf, HALO_ROWS, tf), F32), pltpu.SemaphoreType.DMA(())],
        compiler_params=_params("arbitrary", "arbitrary"),
        name="ffn",
    )(xn2, w_up_bf, w_gate_bf, w_ffconv, b_ffconv.reshape(1, dff), w_down_bf, h1, mod3,
      final_g.reshape(1, d))


def kernel(x, c, positions, norm1_g, norm2_g, w_ada, b_ada, w_in, b_gate, w_sc, w_ret_o, w_conv_o,
           w_mix_o, w_up, w_gate, w_ffconv, b_ffconv, w_down, final_g):
    batch, seq, d = x.shape
    depth = w_in.shape[0]
    assert depth == 1, "final-norm fusion assumes a single layer"
    t = batch * seq
    h = x.reshape(t, d)
    for l in range(depth):
        mod, cos, sin = _ada_rope(c, w_ada[l], b_ada[l], positions)
        mod3 = mod.reshape(batch, 1, 6 * d)
        xn = _norm1(h, norm1_g[l], mod3, seq)
        qkvg, w_ret_o_bf, w_conv_o_bf, w_mix_o_bf = _inproj_a(
            xn, w_in[l], cos, sin, [w_ret_o[l], w_conv_o[l], w_mix_o[l]])
        yc, w_gate_bf = _inproj_c(xn, w_in[l], w_sc[l], w_gate[l], seq)
        gates, w_up_bf = _inproj_g(xn, w_in[l], b_gate[l], w_up[l])
        yr = _retention(qkvg, batch, seq)
        merged, w_down_bf = _merge(yr, yc, gates, w_ret_o_bf, w_conv_o_bf, w_down[l])
        h1, xn2 = _mixout(merged, w_mix_o_bf, h, mod3, norm2_g[l], seq)
        h = _ffn(xn2, w_up_bf, w_gate_bf, w_ffconv[l], b_ffconv[l], w_down_bf, h1, mod3, final_g, seq)
    return h.reshape(batch, seq, d)
```

```python
import functools

import jax
import jax.numpy as jnp
from jax import lax
from jax.experimental import pallas as pl
from jax.experimental.pallas import tpu as pltpu

F32 = jnp.float32
BF16 = jnp.bfloat16

RET_HEADS = 8
RET_HEAD_DIM = 256
D_RET = RET_HEADS * RET_HEAD_DIM
ROT_HALF = RET_HEAD_DIM // 2
CONV_WIDTH = 3
ROPE_BASE = 10000.0
EPS = 1e-6
RET_CHUNK = 256
HALO_ROWS = 8

V7X_VMEM_LIMIT_BYTES = 56 * 1024 * 1024


def _params(*semantics):
    return pltpu.CompilerParams(dimension_semantics=semantics,
                                vmem_limit_bytes=V7X_VMEM_LIMIT_BYTES)


def _silu(v):
    return v * jax.nn.sigmoid(v)


def _mod_rmsnorm(src_ref, rows, gain, scale, shift):
    x = src_ref[rows, :]
    r = lax.rsqrt(jnp.mean(x * x, axis=-1, keepdims=True) + EPS)
    return (src_ref[rows, :] * r) * (gain * (1.0 + scale)) + shift


def _causal_conv3(u, prev, w):
    def taps(a):
        return w[0:1, :] * pltpu.roll(a, 2, 0) + w[1:2, :] * pltpu.roll(a, 1, 0) + w[2:3, :] * a

    head = taps(jnp.concatenate([prev, u[:HALO_ROWS, :]], axis=0))[HALO_ROWS:, :]
    return jnp.concatenate([head, taps(u)[HALO_ROWS:, :]], axis=0)


def _side_cast_specs(weights, steps, step_index):
    specs = [pl.BlockSpec((w.shape[0] // steps, w.shape[1]), lambda *g: (step_index(*g), 0)) for w in weights]
    shapes = [jax.ShapeDtypeStruct(w.shape, BF16) for w in weights]
    return specs, shapes


def _side_cast(in_refs, out_refs):
    for src, dst in zip(in_refs, out_refs):
        dst[...] = src[...].astype(BF16)


def _rope_angles(positions):
    inv_freq = ROPE_BASE ** (-jnp.arange(ROT_HALF, dtype=F32) / ROT_HALF)
    return positions.reshape(positions.size, 1), inv_freq.reshape(1, ROT_HALF)


def _ada_rope_kernel(c_ref, w_ref, b_ref, pos_ref, inv_ref, mod_ref, cos_ref, sin_ref):
    ca = _silu(c_ref[...]).astype(BF16)
    mod_ref[...] = jnp.dot(ca, w_ref[...].astype(BF16), preferred_element_type=F32) + b_ref[...]
    ang = pos_ref[...].astype(F32) * inv_ref[...]
    cos_ref[0] = jnp.cos(ang)
    sin_ref[0] = jnp.sin(ang)
    cos_ref[1] = jnp.zeros_like(ang)
    sin_ref[1] = jnp.zeros_like(ang)


def _ada_rope(c, w_ada, b_ada, positions, steps=16):
    b, d = c.shape
    n = w_ada.shape[1]
    half = positions.size // 2
    tn, tr = n // steps, half // steps
    pos, inv_freq = _rope_angles(positions)
    rope_spec = pl.BlockSpec((2, tr, ROT_HALF), lambda j: (0, j, 0))
    return pl.pallas_call(
        _ada_rope_kernel,
        grid=(steps,),
        in_specs=[pl.BlockSpec((b, d), lambda j: (0, 0)),
                  pl.BlockSpec((d, tn), lambda j: (0, j)),
                  pl.BlockSpec((1, tn), lambda j: (0, j)),
                  pl.BlockSpec((tr, 1), lambda j: (j, 0)),
                  pl.BlockSpec((1, ROT_HALF), lambda j: (0, 0))],
        out_specs=[pl.BlockSpec((b, tn), lambda j: (0, j)), rope_spec, rope_spec],
        out_shape=[jax.ShapeDtypeStruct((b, n), F32),
                   jax.ShapeDtypeStruct((2, half, ROT_HALF), F32),
                   jax.ShapeDtypeStruct((2, half, ROT_HALF), F32)],
        compiler_params=_params("arbitrary"),
        name="ada_rope",
    )(c, w_ada, b_ada.reshape(1, n), pos, inv_freq)


def _norm1_kernel(x_ref, g_ref, sc_ref, sh_ref, pos_ref, inv_ref, cos_in, sin_in, o_ref, cos_ref, sin_ref):
    del cos_in, sin_in
    o_ref[...] = _mod_rmsnorm(x_ref, slice(None), g_ref[...], sc_ref[...], sh_ref[...]).astype(BF16)
    ang = pos_ref[...].astype(F32) * inv_ref[...]
    cos_ref[...] = jnp.cos(ang)
    sin_ref[...] = jnp.sin(ang)


def _norm1(x2d, norm_g, mod3, positions, cos, sin, seq, tm=1024):
    t, d = x2d.shape
    tps = seq // tm
    steps = t // tm
    tr = cos.shape[1] // steps
    pos, inv_freq = _rope_angles(positions)
    rope_spec = pl.BlockSpec((None, tr, ROT_HALF), lambda i: (1, i, 0))
    return pl.pallas_call(
        _norm1_kernel,
        grid=(steps,),
        in_specs=[pl.BlockSpec((tm, d), lambda i: (i, 0)),
                  pl.BlockSpec((1, d), lambda i: (0, 0)),
                  pl.BlockSpec((None, 1, d), lambda i: (i // tps, 0, 1)),
                  pl.BlockSpec((None, 1, d), lambda i: (i // tps, 0, 0)),
                  pl.BlockSpec((tr, 1), lambda i: (steps + i, 0)),
                  pl.BlockSpec((1, ROT_HALF), lambda i: (0, 0)),
                  pl.BlockSpec(memory_space=pl.ANY),
                  pl.BlockSpec(memory_space=pl.ANY)],
        out_specs=[pl.BlockSpec((tm, d), lambda i: (i, 0)), rope_spec, rope_spec],
        out_shape=[jax.ShapeDtypeStruct((t, d), BF16),
                   jax.ShapeDtypeStruct(cos.shape, F32), jax.ShapeDtypeStruct(sin.shape, F32)],
        input_output_aliases={6: 1, 7: 2},
        compiler_params=_params("arbitrary"),
        name="norm1",
    )(x2d, norm_g.reshape(1, d), mod3, mod3, pos, inv_freq, cos, sin)


def _inproj_a_kernel(xn_ref, w_ref, cos_ref, sin_ref, *rest, q_tiles, rot_tiles, gate_tile0, n_sub, n_side):
    side_in, (o_ref, *side_out, wbf_ref) = rest[:n_side], rest[n_side:]
    j = pl.program_id(0)

    @pl.when(pl.program_id(1) == 0)
    def _():
        wbf_ref[...] = w_ref[...].astype(BF16)

    _side_cast(side_in, side_out)
    heads = o_ref.shape[0]
    tr = xn_ref.shape[0] // n_sub

    def proj(rows):
        return jnp.dot(xn_ref[rows, :], wbf_ref[...], preferred_element_type=F32)

    @pl.when(j < rot_tiles)
    def _():
        scale = jnp.where(j < q_tiles, RET_HEAD_DIM ** -0.5, 1.0).astype(F32)
        for s in range(n_sub):
            rows = pl.ds(s * tr, tr)
            acc = proj(rows)
            cos = cos_ref[rows, :] * scale
            sin = sin_ref[rows, :] * scale
            for hh in range(heads):
                lo = hh * RET_HEAD_DIM
                x1 = acc[:, lo:lo + ROT_HALF]
                x2 = acc[:, lo + ROT_HALF:lo + RET_HEAD_DIM]
                o_ref[hh, rows, 0:ROT_HALF] = (x1 * cos - x2 * sin).astype(BF16)
                o_ref[hh, rows, ROT_HALF:RET_HEAD_DIM] = (x2 * cos + x1 * sin).astype(BF16)

    def plain(fn):
        for s in range(n_sub):
            rows = pl.ds(s * tr, tr)
            res = fn(proj(rows)).astype(BF16)
            for hh in range(heads):
                o_ref[hh, rows, :] = res[:, hh * RET_HEAD_DIM:(hh + 1) * RET_HEAD_DIM]

    pl.when(jnp.logical_and(j >= rot_tiles, j < gate_tile0))(functools.partial(plain, lambda a: a))
    pl.when(j >= gate_tile0)(functools.partial(plain, _silu))


def _inproj_a(xn, w_in, cos, sin, side_weights, tm=2048, tn=1024, n_sub=4):
    t, d = xn.shape
    n = 4 * D_RET
    heads = tn // RET_HEAD_DIM
    n_i = t // tm
    side_specs, side_shapes = _side_cast_specs(side_weights, (n // tn) * n_i, lambda j, i: j * n_i + i)
    kern = functools.partial(_inproj_a_kernel, q_tiles=D_RET // tn, rot_tiles=2 * D_RET // tn,
                             gate_tile0=3 * D_RET // tn, n_sub=n_sub, n_side=len(side_weights))
    return pl.pallas_call(
        kern,
        grid=(n // tn, n_i),
        in_specs=[pl.BlockSpec((tm, d), lambda j, i: (i, 0)),
                  pl.BlockSpec((d, tn), lambda j, i: (0, j)),
                  pl.BlockSpec((tm, ROT_HALF), lambda j, i: (i, 0)),
                  pl.BlockSpec((tm, ROT_HALF), lambda j, i: (i, 0))] + side_specs,
        out_specs=[pl.BlockSpec((heads, tm, RET_HEAD_DIM), lambda j, i: (j, i, 0))] + side_specs,
        out_shape=[jax.ShapeDtypeStruct((n // RET_HEAD_DIM, t, RET_HEAD_DIM), BF16)] + side_shapes,
        scratch_shapes=[pltpu.VMEM((d, tn), BF16)],
        compiler_params=_params("arbitrary", "arbitrary"),
        name="inproj_a",
    )(xn, w_in, cos, sin, *side_weights)


def _inproj_c_kernel(xn_ref, wb_ref, wc_ref, wx_ref, wsc_ref, side_ref, o_ref, side_out_ref,
                     wb_bf, wc_bf, wx_bf, halo_ref, *, tiles_per_seq):
    j = pl.program_id(0)
    i = pl.program_id(1)

    @pl.when(jnp.logical_and(i == 0, j == 0))
    def _():
        halo_ref[...] = jnp.zeros_like(halo_ref)

    @pl.when(i == 0)
    def _():
        wb_bf[...] = wb_ref[...].astype(BF16)
        wc_bf[...] = wc_ref[...].astype(BF16)
        wx_bf[...] = wx_ref[...].astype(BF16)

    _side_cast([side_ref], [side_out_ref])
    xn = xn_ref[...]
    cg = jnp.dot(xn, wc_bf[...], preferred_element_type=F32)
    xc = jnp.dot(xn, wx_bf[...], preferred_element_type=F32)
    u = cg * xc
    prev = jnp.where(i % tiles_per_seq == 0, 0.0, halo_ref[...])
    conv = _causal_conv3(u, prev, wsc_ref[...])
    halo_ref[...] = u[u.shape[0] - HALO_ROWS:, :]
    bg = jnp.dot(xn, wb_bf[...], preferred_element_type=F32)
    o_ref[...] = (bg * conv).astype(BF16)


def _inproj_c(xn, w_in, w_sc, side_weight, seq, tm=1024, tc=512):
    t, d = xn.shape
    nc = d // tc
    n_i = t // tm
    first = 4 * D_RET // tc
    kern = functools.partial(_inproj_c_kernel, tiles_per_seq=seq // tm)
    wspec = lambda sec: pl.BlockSpec((d, tc), lambda j, i: (0, first + sec * nc + j))
    side_specs, side_shapes = _side_cast_specs([side_weight], nc * n_i, lambda j, i: j * n_i + i)
    return pl.pallas_call(
        kern,
        grid=(nc, n_i),
        in_specs=[pl.BlockSpec((tm, d), lambda j, i: (i, 0)),
                  wspec(0), wspec(1), wspec(2),
                  pl.BlockSpec((CONV_WIDTH, tc), lambda j, i: (0, j))] + side_specs,
        out_specs=[pl.BlockSpec((tm, tc), lambda j, i: (i, j))] + side_specs,
        out_shape=[jax.ShapeDtypeStruct((t, d), BF16)] + side_shapes,
        scratch_shapes=[pltpu.VMEM((d, tc), BF16)] * 3 + [pltpu.VMEM((HALO_ROWS, tc), F32)],
        compiler_params=_params("arbitrary", "arbitrary"),
        name="inproj_c",
    )(xn, w_in, w_in, w_in, w_sc, side_weight)


def _inproj_g_kernel(xn_ref, w_ref, b_ref, side_ref, o_ref, side_out_ref, wbf_ref, *, n_sub):
    @pl.when(pl.program_id(1) == 0)
    def _():
        wbf_ref[...] = w_ref[...].astype(BF16)

    _side_cast([side_ref], [side_out_ref])
    tr = xn_ref.shape[0] // n_sub
    for s in range(n_sub):
        rows = pl.ds(s * tr, tr)
        logits = jnp.dot(xn_ref[rows, :], wbf_ref[...], preferred_element_type=F32) + b_ref[...]
        o_ref[rows, :] = jax.nn.sigmoid(logits).astype(BF16)


def _inproj_g(xn, w_in, b_gate, side_weight, tm=2048, tn=1024, n_sub=4):
    t, d = xn.shape
    n = b_gate.shape[0]
    n_i = t // tm
    first = (w_in.shape[1] - n) // tn
    side_specs, side_shapes = _side_cast_specs([side_weight], (n // tn) * n_i, lambda j, i: j * n_i + i)
    return pl.pallas_call(
        functools.partial(_inproj_g_kernel, n_sub=n_sub),
        grid=(n // tn, n_i),
        in_specs=[pl.BlockSpec((tm, d), lambda j, i: (i, 0)),
                  pl.BlockSpec((d, tn), lambda j, i: (0, first + j)),
                  pl.BlockSpec((1, tn), lambda j, i: (0, j))] + side_specs,
        out_specs=[pl.BlockSpec((tm, tn), lambda j, i: (i, j))] + side_specs,
        out_shape=[jax.ShapeDtypeStruct((t, n), BF16)] + side_shapes,
        scratch_shapes=[pltpu.VMEM((d, tn), BF16)],
        compiler_params=_params("arbitrary", "arbitrary"),
        name="inproj_g",
    )(xn, w_in, b_gate.reshape(1, n), side_weight)


def _ret_kernel(q_ref, k_ref, v_ref, gs_ref, o_ref):
    heads = q_ref.shape[0]
    first = pl.program_id(1) * heads

    def prepare(hh):
        return _ret_scores_and_states(first + hh, q_ref.at[hh], k_ref.at[hh], v_ref.at[hh])

    prep = prepare(0)
    for hh in range(heads):
        next_prep = prepare(hh + 1) if hh + 1 < heads else None
        cols = pl.ds(hh * RET_HEAD_DIM, RET_HEAD_DIM)
        _ret_outputs(prep, q_ref.at[hh], v_ref.at[hh], gs_ref.at[hh], o_ref.at[:, cols])
        prep = next_prep


def _ret_scores_and_states(head_index, q_ref, k_ref, v_ref):
    c = RET_CHUNK
    n_chunks = q_ref.shape[0] // c
    head = jnp.full((1, RET_HEAD_DIM), head_index, jnp.int32).astype(F32)
    log_gamma = jnp.log(1.0 - jnp.exp2(-5.0 - head))
    row = lax.broadcasted_iota(jnp.int32, (c, c), 0).astype(F32)
    col = lax.broadcasted_iota(jnp.int32, (c, c), 1).astype(F32)
    diff = row - col
    decay = jnp.where(diff >= 0, jnp.exp(log_gamma * jnp.maximum(diff, 0.0)), 0.0)
    q_dec = jnp.exp(log_gamma * (row + 1.0))
    k_dec = jnp.exp(log_gamma * (c - 1.0 - row))
    chunk_dec = jnp.exp(log_gamma * float(c))
    chunks = [pl.ds(n * c, c) for n in range(n_chunks)]

    kvs = []
    for rows in chunks[:-1]:
        kd = (k_ref[rows, :].astype(F32) * k_dec).astype(BF16)
        kvs.append(lax.dot_general(kd, v_ref[rows, :], (((0,), (0,)), ((), ())),
                                   preferred_element_type=F32))
    scores = [lax.dot_general(q_ref[rows, :], k_ref[rows, :], (((1,), (1,)), ((), ())),
                              preferred_element_type=F32) for rows in chunks]
    states = [None]
    for kv in kvs:
        states.append(kv if states[-1] is None else chunk_dec * states[-1] + kv)
    return chunks, scores, states, decay, q_dec


def _ret_outputs(prep, q_ref, v_ref, gs_ref, o_ref):
    chunks, scores, states, decay, q_dec = prep
    for rows, s, state in zip(chunks, scores, states):
        y = jnp.dot((s * decay).astype(BF16), v_ref[rows, :], preferred_element_type=F32)
        if state is not None:
            y = y + q_dec * jnp.dot(q_ref[rows, :], state.astype(BF16), preferred_element_type=F32)
        yn = y * lax.rsqrt(jnp.mean(y * y, axis=-1, keepdims=True) + EPS)
        o_ref[rows, :] = (yn * gs_ref[rows, :].astype(F32)).astype(BF16)


def _retention(qkvg, batch, seq, heads_per_step=4):
    t = qkvg.shape[1]
    groups = RET_HEADS // heads_per_step
    spec = lambda sec: pl.BlockSpec((heads_per_step, seq, RET_HEAD_DIM),
                                    lambda b, g: (sec * groups + g, b, 0))
    return pl.pallas_call(
        _ret_kernel,
        grid=(batch, groups),
        in_specs=[spec(0), spec(1), spec(2), spec(3)],
        out_specs=pl.BlockSpec((seq, heads_per_step * RET_HEAD_DIM), lambda b, g: (b, g)),
        out_shape=jax.ShapeDtypeStruct((t, D_RET), BF16),
        compiler_params=_params("arbitrary", "arbitrary"),
        name="ret",
    )(qkvg, qkvg, qkvg, qkvg)


def _merge_kernel(yr_ref, yc_ref, ga_ref, gb_ref, wr_ref, wc_ref, side_ref, o_ref, side_out_ref):
    _side_cast([side_ref], [side_out_ref])
    ya = jnp.dot(yr_ref[...], wr_ref[...], preferred_element_type=F32)
    yb = jnp.dot(yc_ref[...], wc_ref[...], preferred_element_type=F32)
    o_ref[...] = (ga_ref[...].astype(F32) * ya + gb_ref[...].astype(F32) * yb).astype(BF16)


def _merge(yr, yc, gates, w_ret_o_bf, w_conv_o_bf, side_weight, tm=512):
    t, d = yr.shape
    resident = lambda: pl.BlockSpec((d, d), lambda i: (0, 0), pipeline_mode=pl.Buffered(1))
    side_specs, side_shapes = _side_cast_specs([side_weight], t // tm, lambda i: i)
    return pl.pallas_call(
        _merge_kernel,
        grid=(t // tm,),
        in_specs=[pl.BlockSpec((tm, d), lambda i: (i, 0)),
                  pl.BlockSpec((tm, d), lambda i: (i, 0)),
                  pl.BlockSpec((tm, d), lambda i: (i, 0)),
                  pl.BlockSpec((tm, d), lambda i: (i, 1)),
                  resident(), resident()] + side_specs,
        out_specs=[pl.BlockSpec((tm, d), lambda i: (i, 0))] + side_specs,
        out_shape=[jax.ShapeDtypeStruct((t, d), BF16)] + side_shapes,
        compiler_params=_params("arbitrary"),
        name="merge",
    )(yr, yc, gates, gates, w_ret_o_bf, w_conv_o_bf, side_weight)


def _mixout_kernel(m_ref, w_ref, x_ref, g1_ref, g_ref, sc_ref, sh_ref, h_ref, xn_ref, *, n_sub):
    tr = m_ref.shape[0] // n_sub
    for s in range(n_sub):
        rows = pl.ds(s * tr, tr)
        mixed = jnp.dot(m_ref[rows, :], w_ref[...], preferred_element_type=F32)
        h = x_ref[rows, :] + g1_ref[...] * mixed
        h_ref[rows, :] = h
        xn_ref[rows, :] = _mod_rmsnorm(h_ref, rows, g_ref[...], sc_ref[...], sh_ref[...]).astype(BF16)


def _mixout(m, w_mix_o_bf, x2d, mod3, norm_g, seq, tm=512, n_sub=2):
    t, d = x2d.shape
    tps = seq // tm
    mod_spec = lambda k: pl.BlockSpec((None, 1, d), lambda i: (i // tps, 0, k))
    return pl.pallas_call(
        functools.partial(_mixout_kernel, n_sub=n_sub),
        grid=(t // tm,),
        in_specs=[pl.BlockSpec((tm, d), lambda i: (i, 0)),
                  pl.BlockSpec((d, d), lambda i: (0, 0), pipeline_mode=pl.Buffered(1)),
                  pl.BlockSpec((tm, d), lambda i: (i, 0)),
                  mod_spec(2), pl.BlockSpec((1, d), lambda i: (0, 0)), mod_spec(4), mod_spec(3)],
        out_specs=[pl.BlockSpec((tm, d), lambda i: (i, 0)), pl.BlockSpec((tm, d), lambda i: (i, 0))],
        out_shape=[jax.ShapeDtypeStruct((t, d), F32), jax.ShapeDtypeStruct((t, d), BF16)],
        compiler_params=_params("arbitrary"),
        name="mixout",
    )(m, w_mix_o_bf, x2d, mod3, norm_g.reshape(1, d), mod3, mod3)


def _ffn_kernel(xn_ref, wu_ref, wg_ref, wfc_ref, bfc_ref, wd_ref, h_hbm, g2_ref, fg_ref,
                o_ref, halo_ref, h_sem, *, tiles_per_seq, sub_width, last_width):
    i = pl.program_id(0)
    f = pl.program_id(1)
    last = pl.num_programs(1) - 1
    tm = o_ref.shape[0]

    @pl.when(jnp.logical_and(i == 0, f == 0))
    def _():
        halo_ref[...] = jnp.zeros_like(halo_ref)

    def h1_copy():
        return pltpu.make_async_copy(h_hbm.at[pl.ds(i * tm, tm), :], o_ref, h_sem)

    def step(first, width):
        if first:
            h1_copy().start()
        xn = xn_ref[...]
        not_seq_start = i % tiles_per_seq != 0
        col_blocks = [pl.ds(s * sub_width, sub_width) for s in range(width // sub_width)]
        up_gate = [(jnp.dot(xn, wu_ref[:, cols], preferred_element_type=F32),
                    jnp.dot(xn, wg_ref[:, cols], preferred_element_type=F32)) for cols in col_blocks]
        part = None
        for cols, (up, gpre) in zip(col_blocks, up_gate):
            prev = jnp.where(not_seq_start, halo_ref[f, :, cols], 0.0)
            gt = _causal_conv3(gpre, prev, wfc_ref[:, cols]) + bfc_ref[:, cols]
            halo_ref[f, :, cols] = gpre[gpre.shape[0] - HALO_ROWS:, :]
            act = (_silu(gt) * up).astype(BF16)
            down = jnp.dot(act, wd_ref[cols, :], preferred_element_type=F32)
            part = down if part is None else part + down
        if first:
            h1_copy().wait()
        o_ref[...] += g2_ref[...] * part

    tile_width = wu_ref.shape[1]
    pl.when(f == 0)(functools.partial(step, True, tile_width))
    pl.when(jnp.logical_and(f > 0, f < last))(functools.partial(step, False, tile_width))
    pl.when(f == last)(functools.partial(step, False, last_width))

    @pl.when(f == last)
    def _():
        r = lax.rsqrt(jnp.mean(jnp.square(o_ref[...]), axis=-1, keepdims=True) + EPS)
        o_ref[...] = (o_ref[...] * r) * fg_ref[...]


def _ffn(xn2, w_up_bf, w_gate_bf, w_ffconv, b_ffconv, w_down_bf, h1, mod3, final_g, seq,
         tm=1024, tf=768, sub_width=256):
    t, d = xn2.shape
    dff = w_up_bf.shape[1]
    nf = pl.cdiv(dff, tf)
    tps = seq // tm
    assert nf >= 2 and tf % sub_width == 0 and dff % sub_width == 0
    kern = functools.partial(_ffn_kernel, tiles_per_seq=tps, sub_width=sub_width,
                             last_width=dff - (nf - 1) * tf)
    return pl.pallas_call(
        kern,
        grid=(t // tm, nf),
        in_specs=[pl.BlockSpec((tm, d), lambda i, f: (i, 0)),
                  pl.BlockSpec((d, tf), lambda i, f: (0, f)),
                  pl.BlockSpec((d, tf), lambda i, f: (0, f)),
                  pl.BlockSpec((CONV_WIDTH, tf), lambda i, f: (0, f)),
                  pl.BlockSpec((1, tf), lambda i, f: (0, f)),
                  pl.BlockSpec((tf, d), lambda i, f: (f, 0)),
                  pl.BlockSpec(memory_space=pl.ANY),
                  pl.BlockSpec((None, 1, d), lambda i, f: (i // tps, 0, 5)),
                  pl.BlockSpec((1, d), lambda i, f: (0, 0))],
        out_specs=pl.BlockSpec((tm, d), lambda i, f: (i, 0)),
        out_shape=jax.ShapeDtypeStruct((t, d), F32),
        scratch_shapes=[pltpu.VMEM((nf, HALO_ROWS, tf), F32), pltpu.SemaphoreType.DMA(())],
        compiler_params=_params("arbitrary", "arbitrary"),
        name="ffn",
    )(xn2, w_up_bf, w_gate_bf, w_ffconv, b_ffconv.reshape(1, dff), w_down_bf, h1, mod3,
      final_g.reshape(1, d))


def kernel(x, c, positions, norm1_g, norm2_g, w_ada, b_ada, w_in, b_gate, w_sc, w_ret_o, w_conv_o,
           w_mix_o, w_up, w_gate, w_ffconv, b_ffconv, w_down, final_g):
    batch, seq, d = x.shape
    depth = w_in.shape[0]
    assert depth == 1, "final-norm fusion assumes a single layer"
    t = batch * seq
    h = x.reshape(t, d)
    for l in range(depth):
        mod, cos, sin = _ada_rope(c, w_ada[l], b_ada[l], positions)
        mod3 = mod.reshape(batch, 1, 6 * d)
        xn, cos, sin = _norm1(h, norm1_g[l], mod3, positions, cos, sin, seq)
        qkvg, w_ret_o_bf, w_conv_o_bf, w_mix_o_bf = _inproj_a(
            xn, w_in[l], cos.reshape(t, ROT_HALF), sin.reshape(t, ROT_HALF),
            [w_ret_o[l], w_conv_o[l], w_mix_o[l]])
        yc, w_gate_bf = _inproj_c(xn, w_in[l], w_sc[l], w_gate[l], seq)
        gates, w_up_bf = _inproj_g(xn, w_in[l], b_gate[l], w_up[l])
        yr = _retention(qkvg, batch, seq)
        merged, w_down_bf = _merge(yr, yc, gates, w_ret_o_bf, w_conv_o_bf, w_down[l])
        h1, xn2 = _mixout(merged, w_mix_o_bf, h, mod3, norm2_g[l], seq)
        h = _ffn(xn2, w_up_bf, w_gate_bf, w_ffconv[l], b_ffconv[l], w_down_bf, h1, mod3, final_g, seq)
    return h.reshape(batch, seq, d)
```

```python
import functools

import jax
import jax.numpy as jnp
from jax import lax
from jax.experimental import pallas as pl
from jax.experimental.pallas import tpu as pltpu

F32 = jnp.float32
BF16 = jnp.bfloat16

RET_HEADS = 8
RET_HEAD_DIM = 256
D_RET = RET_HEADS * RET_HEAD_DIM
ROT_HALF = RET_HEAD_DIM // 2
CONV_WIDTH = 3
ROPE_BASE = 10000.0
EPS = 1e-6
RET_CHUNK = 256
HALO_ROWS = 8

V7X_VMEM_LIMIT_BYTES = 56 * 1024 * 1024


def _params(*semantics):
    return pltpu.CompilerParams(dimension_semantics=semantics,
                                vmem_limit_bytes=V7X_VMEM_LIMIT_BYTES)


def _silu(v):
    return v * jax.nn.sigmoid(v)


def _mod_rmsnorm(src_ref, rows, gain, scale, shift):
    x = src_ref[rows, :]
    r = lax.rsqrt(jnp.mean(x * x, axis=-1, keepdims=True) + EPS)
    return (src_ref[rows, :] * r) * (gain * (1.0 + scale)) + shift


def _causal_conv3(u, prev, w):
    def taps(a):
        return w[0:1, :] * pltpu.roll(a, 2, 0) + w[1:2, :] * pltpu.roll(a, 1, 0) + w[2:3, :] * a

    head = taps(jnp.concatenate([prev, u[:HALO_ROWS, :]], axis=0))[HALO_ROWS:, :]
    return jnp.concatenate([head, taps(u)[HALO_ROWS:, :]], axis=0)


def _side_cast_specs(weights, steps, step_index):
    specs = [pl.BlockSpec((w.shape[0] // steps, w.shape[1]), lambda *g: (step_index(*g), 0)) for w in weights]
    shapes = [jax.ShapeDtypeStruct(w.shape, BF16) for w in weights]
    return specs, shapes


def _side_cast(in_refs, out_refs):
    for src, dst in zip(in_refs, out_refs):
        dst[...] = src[...].astype(BF16)


def _rope_angles(positions, tr):
    inv_freq = ROPE_BASE ** (-jnp.arange(ROT_HALF, dtype=F32) / ROT_HALF)
    return positions.reshape(positions.size // tr, 1, tr), inv_freq.reshape(ROT_HALF, 1)


def _rope_tables(pos_ref, inv_ref):
    ang_t = inv_ref[...] * pos_ref[...].astype(F32)
    return jnp.cos(ang_t).T, jnp.sin(ang_t).T


def _ada_rope_kernel(c_ref, w_ref, b_ref, pos_ref, inv_ref, mod_ref, cos_ref, sin_ref):
    @pl.when(pl.program_id(0) == 0)
    def _():
        mod_ref[...] = jnp.broadcast_to(b_ref[...], mod_ref.shape)

    ca = _silu(c_ref[...]).astype(BF16)
    mod_ref[...] += jnp.dot(ca, w_ref[...].astype(BF16), preferred_element_type=F32)
    cos, sin = _rope_tables(pos_ref, inv_ref)
    cos_ref[0] = cos
    sin_ref[0] = sin
    cos_ref[1] = jnp.zeros_like(cos)
    sin_ref[1] = jnp.zeros_like(sin)


def _ada_rope(c, w_ada, b_ada, positions, steps=16):
    b, d = c.shape
    n = w_ada.shape[1]
    half = positions.size // 2
    tk, tr = d // steps, half // steps
    pos, inv_freq = _rope_angles(positions, tr)
    rope_spec = pl.BlockSpec((2, tr, ROT_HALF), lambda j: (0, j, 0))
    return pl.pallas_call(
        _ada_rope_kernel,
        grid=(steps,),
        in_specs=[pl.BlockSpec((b, tk), lambda j: (0, j)),
                  pl.BlockSpec((tk, n), lambda j: (j, 0)),
                  pl.BlockSpec((1, n), lambda j: (0, 0)),
                  pl.BlockSpec((None, 1, tr), lambda j: (j, 0, 0)),
                  pl.BlockSpec((ROT_HALF, 1), lambda j: (0, 0))],
        out_specs=[pl.BlockSpec((b, n), lambda j: (0, 0)), rope_spec, rope_spec],
        out_shape=[jax.ShapeDtypeStruct((b, n), F32),
                   jax.ShapeDtypeStruct((2, half, ROT_HALF), F32),
                   jax.ShapeDtypeStruct((2, half, ROT_HALF), F32)],
        compiler_params=_params("arbitrary"),
        name="ada_rope",
    )(c, w_ada, b_ada.reshape(1, n), pos, inv_freq)


def _norm1_kernel(x_ref, g_ref, sc_ref, sh_ref, pos_ref, inv_ref, cos_in, sin_in, o_ref, cos_ref, sin_ref):
    del cos_in, sin_in
    o_ref[...] = _mod_rmsnorm(x_ref, slice(None), g_ref[...], sc_ref[...], sh_ref[...]).astype(BF16)
    cos_ref[...], sin_ref[...] = _rope_tables(pos_ref, inv_ref)


def _norm1(x2d, norm_g, mod3, positions, cos, sin, seq, tm=1024):
    t, d = x2d.shape
    tps = seq // tm
    steps = t // tm
    tr = cos.shape[1] // steps
    pos, inv_freq = _rope_angles(positions, tr)
    rope_spec = pl.BlockSpec((None, tr, ROT_HALF), lambda i: (1, i, 0))
    return pl.pallas_call(
        _norm1_kernel,
        grid=(steps,),
        in_specs=[pl.BlockSpec((tm, d), lambda i: (i, 0)),
                  pl.BlockSpec((1, d), lambda i: (0, 0)),
                  pl.BlockSpec((None, 1, d), lambda i: (i // tps, 0, 1)),
                  pl.BlockSpec((None, 1, d), lambda i: (i // tps, 0, 0)),
                  pl.BlockSpec((None, 1, tr), lambda i: (steps + i, 0, 0)),
                  pl.BlockSpec((ROT_HALF, 1), lambda i: (0, 0)),
                  pl.BlockSpec(memory_space=pl.ANY),
                  pl.BlockSpec(memory_space=pl.ANY)],
        out_specs=[pl.BlockSpec((tm, d), lambda i: (i, 0)), rope_spec, rope_spec],
        out_shape=[jax.ShapeDtypeStruct((t, d), BF16),
                   jax.ShapeDtypeStruct(cos.shape, F32), jax.ShapeDtypeStruct(sin.shape, F32)],
        input_output_aliases={6: 1, 7: 2},
        compiler_params=_params("arbitrary"),
        name="norm1",
    )(x2d, norm_g.reshape(1, d), mod3, mod3, pos, inv_freq, cos, sin)


def _inproj_a_kernel(xn_ref, w_ref, cos_ref, sin_ref, *rest, q_tiles, rot_tiles, gate_tile0, n_sub, n_side):
    side_in, (o_ref, *side_out, wbf_ref) = rest[:n_side], rest[n_side:]
    j = pl.program_id(0)

    @pl.when(pl.program_id(1) == 0)
    def _():
        wbf_ref[...] = w_ref[...].astype(BF16)

    _side_cast(side_in, side_out)
    heads = o_ref.shape[0]
    tr = xn_ref.shape[0] // n_sub

    def proj(rows):
        return jnp.dot(xn_ref[rows, :], wbf_ref[...], preferred_element_type=F32)

    @pl.when(j < rot_tiles)
    def _():
        scale = jnp.where(j < q_tiles, RET_HEAD_DIM ** -0.5, 1.0).astype(F32)
        for s in range(n_sub):
            rows = pl.ds(s * tr, tr)
            acc = proj(rows)
            cos = cos_ref[rows, :] * scale
            sin = sin_ref[rows, :] * scale
            for hh in range(heads):
                lo = hh * RET_HEAD_DIM
                x1 = acc[:, lo:lo + ROT_HALF]
                x2 = acc[:, lo + ROT_HALF:lo + RET_HEAD_DIM]
                o_ref[hh, rows, 0:ROT_HALF] = (x1 * cos - x2 * sin).astype(BF16)
                o_ref[hh, rows, ROT_HALF:RET_HEAD_DIM] = (x2 * cos + x1 * sin).astype(BF16)

    def plain(fn):
        for s in range(n_sub):
            rows = pl.ds(s * tr, tr)
            res = fn(proj(rows)).astype(BF16)
            for hh in range(heads):
                o_ref[hh, rows, :] = res[:, hh * RET_HEAD_DIM:(hh + 1) * RET_HEAD_DIM]

    pl.when(jnp.logical_and(j >= rot_tiles, j < gate_tile0))(functools.partial(plain, lambda a: a))
    pl.when(j >= gate_tile0)(functools.partial(plain, _silu))


def _inproj_a(xn, w_in, cos, sin, side_weights, tm=2048, tn=1024, n_sub=4):
    t, d = xn.shape
    n = 4 * D_RET
    heads = tn // RET_HEAD_DIM
    n_i = t // tm
    side_specs, side_shapes = _side_cast_specs(side_weights, (n // tn) * n_i, lambda j, i: j * n_i + i)
    kern = functools.partial(_inproj_a_kernel, q_tiles=D_RET // tn, rot_tiles=2 * D_RET // tn,
                             gate_tile0=3 * D_RET // tn, n_sub=n_sub, n_side=len(side_weights))
    return pl.pallas_call(
        kern,
        grid=(n // tn, n_i),
        in_specs=[pl.BlockSpec((tm, d), lambda j, i: (i, 0)),
                  pl.BlockSpec((d, tn), lambda j, i: (0, j)),
                  pl.BlockSpec((tm, ROT_HALF), lambda j, i: (i, 0)),
                  pl.BlockSpec((tm, ROT_HALF), lambda j, i: (i, 0))] + side_specs,
        out_specs=[pl.BlockSpec((heads, tm, RET_HEAD_DIM), lambda j, i: (j, i, 0))] + side_specs,
        out_shape=[jax.ShapeDtypeStruct((n // RET_HEAD_DIM, t, RET_HEAD_DIM), BF16)] + side_shapes,
        scratch_shapes=[pltpu.VMEM((d, tn), BF16)],
        compiler_params=_params("arbitrary", "arbitrary"),
        name="inproj_a",
    )(xn, w_in, cos, sin, *side_weights)


def _inproj_c_kernel(xn_ref, wb_ref, wc_ref, wx_ref, wsc_ref, side_ref, o_ref, side_out_ref,
                     wb_bf, wc_bf, wx_bf, halo_ref, *, tiles_per_seq):
    j = pl.program_id(0)
    i = pl.program_id(1)

    @pl.when(jnp.logical_and(i == 0, j == 0))
    def _():
        halo_ref[...] = jnp.zeros_like(halo_ref)

    @pl.when(i == 0)
    def _():
        wb_bf[...] = wb_ref[...].astype(BF16)
        wc_bf[...] = wc_ref[...].astype(BF16)
        wx_bf[...] = wx_ref[...].astype(BF16)

    _side_cast([side_ref], [side_out_ref])
    xn = xn_ref[...]
    cg = jnp.dot(xn, wc_bf[...], preferred_element_type=F32)
    xc = jnp.dot(xn, wx_bf[...], preferred_element_type=F32)
    u = cg * xc
    prev = jnp.where(i % tiles_per_seq == 0, 0.0, halo_ref[...])
    conv = _causal_conv3(u, prev, wsc_ref[...])
    halo_ref[...] = u[u.shape[0] - HALO_ROWS:, :]
    bg = jnp.dot(xn, wb_bf[...], preferred_element_type=F32)
    o_ref[...] = (bg * conv).astype(BF16)


def _inproj_c(xn, w_in, w_sc, side_weight, seq, tm=1024, tc=512):
    t, d = xn.shape
    nc = d // tc
    n_i = t // tm
    first = 4 * D_RET // tc
    kern = functools.partial(_inproj_c_kernel, tiles_per_seq=seq // tm)
    wspec = lambda sec: pl.BlockSpec((d, tc), lambda j, i: (0, first + sec * nc + j))
    side_specs, side_shapes = _side_cast_specs([side_weight], nc * n_i, lambda j, i: j * n_i + i)
    return pl.pallas_call(
        kern,
        grid=(nc, n_i),
        in_specs=[pl.BlockSpec((tm, d), lambda j, i: (i, 0)),
                  wspec(0), wspec(1), wspec(2),
                  pl.BlockSpec((CONV_WIDTH, tc), lambda j, i: (0, j))] + side_specs,
        out_specs=[pl.BlockSpec((tm, tc), lambda j, i: (i, j))] + side_specs,
        out_shape=[jax.ShapeDtypeStruct((t, d), BF16)] + side_shapes,
        scratch_shapes=[pltpu.VMEM((d, tc), BF16)] * 3 + [pltpu.VMEM((HALO_ROWS, tc), F32)],
        compiler_params=_params("arbitrary", "arbitrary"),
        name="inproj_c",
    )(xn, w_in, w_in, w_in, w_sc, side_weight)


def _inproj_g_kernel(xn_ref, w_ref, b_ref, side_ref, o_ref, side_out_ref, wbf_ref, *, n_sub):
    @pl.when(pl.program_id(1) == 0)
    def _():
        wbf_ref[...] = w_ref[...].astype(BF16)

    _side_cast([side_ref], [side_out_ref])
    tr = xn_ref.shape[0] // n_sub
    for s in range(n_sub):
        rows = pl.ds(s * tr, tr)
        logits = jnp.dot(xn_ref[rows, :], wbf_ref[...], preferred_element_type=F32) + b_ref[...]
        o_ref[rows, :] = jax.nn.sigmoid(logits).astype(BF16)


def _inproj_g(xn, w_in, b_gate, side_weight, tm=2048, tn=1024, n_sub=4):
    t, d = xn.shape
    n = b_gate.shape[0]
    n_i = t // tm
    first = (w_in.shape[1] - n) // tn
    side_specs, side_shapes = _side_cast_specs([side_weight], (n // tn) * n_i, lambda j, i: j * n_i + i)
    return pl.pallas_call(
        functools.partial(_inproj_g_kernel, n_sub=n_sub),
        grid=(n // tn, n_i),
        in_specs=[pl.BlockSpec((tm, d), lambda j, i: (i, 0)),
                  pl.BlockSpec((d, tn), lambda j, i: (0, first + j)),
                  pl.BlockSpec((1, tn), lambda j, i: (0, j))] + side_specs,
        out_specs=[pl.BlockSpec((tm, tn), lambda j, i: (i, j))] + side_specs,
        out_shape=[jax.ShapeDtypeStruct((t, n), BF16)] + side_shapes,
        scratch_shapes=[pltpu.VMEM((d, tn), BF16)],
        compiler_params=_params("arbitrary", "arbitrary"),
        name="inproj_g",
    )(xn, w_in, b_gate.reshape(1, n), side_weight)


def _ret_kernel(q_ref, k_ref, v_ref, gs_ref, o_ref):
    heads = q_ref.shape[0]
    first = pl.program_id(1) * heads

    def prepare(hh):
        return _ret_scores_and_states(first + hh, q_ref.at[hh], k_ref.at[hh], v_ref.at[hh])

    prep = prepare(0)
    for hh in range(heads):
        next_prep = prepare(hh + 1) if hh + 1 < heads else None
        cols = pl.ds(hh * RET_HEAD_DIM, RET_HEAD_DIM)
        _ret_outputs(prep, q_ref.at[hh], v_ref.at[hh], gs_ref.at[hh], o_ref.at[:, cols])
        prep = next_prep


def _ret_scores_and_states(head_index, q_ref, k_ref, v_ref):
    c = RET_CHUNK
    n_chunks = q_ref.shape[0] // c
    head = jnp.full((1, RET_HEAD_DIM), head_index, jnp.int32).astype(F32)
    log_gamma = jnp.log(1.0 - jnp.exp2(-5.0 - head))
    row = lax.broadcasted_iota(jnp.int32, (c, c), 0).astype(F32)
    col = lax.broadcasted_iota(jnp.int32, (c, c), 1).astype(F32)
    diff = row - col
    decay = jnp.where(diff >= 0, jnp.exp(log_gamma * jnp.maximum(diff, 0.0)), 0.0)
    q_dec = jnp.exp(log_gamma * (row + 1.0))
    k_dec = jnp.exp(log_gamma * (c - 1.0 - row))
    chunk_dec = jnp.exp(log_gamma * float(c))
    chunks = [pl.ds(n * c, c) for n in range(n_chunks)]

    kvs = []
    for rows in chunks[:-1]:
        kd = (k_ref[rows, :].astype(F32) * k_dec).astype(BF16)
        kvs.append(lax.dot_general(kd, v_ref[rows, :], (((0,), (0,)), ((), ())),
                                   preferred_element_type=F32))
    scores = [lax.dot_general(q_ref[rows, :], k_ref[rows, :], (((1,), (1,)), ((), ())),
                              preferred_element_type=F32) for rows in chunks]
    states = [None]
    for kv in kvs:
        states.append(kv if states[-1] is None else chunk_dec * states[-1] + kv)
    return chunks, scores, states, decay, q_dec


def _ret_outputs(prep, q_ref, v_ref, gs_ref, o_ref):
    chunks, scores, states, decay, q_dec = prep
    for rows, s, state in zip(chunks, scores, states):
        y = jnp.dot((s * decay).astype(BF16), v_ref[rows, :], preferred_element_type=F32)
        if state is not None:
            y = y + q_dec * jnp.dot(q_ref[rows, :], state.astype(BF16), preferred_element_type=F32)
        yn = y * lax.rsqrt(jnp.mean(y * y, axis=-1, keepdims=True) + EPS)
        o_ref[rows, :] = (yn * gs_ref[rows, :].astype(F32)).astype(BF16)


def _retention(qkvg, batch, seq, heads_per_step=4):
    t = qkvg.shape[1]
    groups = RET_HEADS // heads_per_step
    spec = lambda sec: pl.BlockSpec((heads_per_step, seq, RET_HEAD_DIM),
                                    lambda b, g: (sec * groups + g, b, 0))
    return pl.pallas_call(
        _ret_kernel,
        grid=(batch, groups),
        in_specs=[spec(0), spec(1), spec(2), spec(3)],
        out_specs=pl.BlockSpec((seq, heads_per_step * RET_HEAD_DIM), lambda b, g: (b, g)),
        out_shape=jax.ShapeDtypeStruct((t, D_RET), BF16),
        compiler_params=_params("arbitrary", "arbitrary"),
        name="ret",
    )(qkvg, qkvg, qkvg, qkvg)


def _merge_kernel(yr_ref, yc_ref, ga_ref, gb_ref, wr_ref, wc_ref, side_ref, o_ref, side_out_ref):
    _side_cast([side_ref], [side_out_ref])
    ya = jnp.dot(yr_ref[...], wr_ref[...], preferred_element_type=F32)
    yb = jnp.dot(yc_ref[...], wc_ref[...], preferred_element_type=F32)
    o_ref[...] = (ga_ref[...].astype(F32) * ya + gb_ref[...].astype(F32) * yb).astype(BF16)


def _merge(yr, yc, gates, w_ret_o_bf, w_conv_o_bf, side_weight, tm=512):
    t, d = yr.shape
    resident = lambda: pl.BlockSpec((d, d), lambda i: (0, 0), pipeline_mode=pl.Buffered(1))
    side_specs, side_shapes = _side_cast_specs([side_weight], t // tm, lambda i: i)
    return pl.pallas_call(
        _merge_kernel,
        grid=(t // tm,),
        in_specs=[pl.BlockSpec((tm, d), lambda i: (i, 0)),
                  pl.BlockSpec((tm, d), lambda i: (i, 0)),
                  pl.BlockSpec((tm, d), lambda i: (i, 0)),
                  pl.BlockSpec((tm, d), lambda i: (i, 1)),
                  resident(), resident()] + side_specs,
        out_specs=[pl.BlockSpec((tm, d), lambda i: (i, 0))] + side_specs,
        out_shape=[jax.ShapeDtypeStruct((t, d), BF16)] + side_shapes,
        compiler_params=_params("arbitrary"),
        name="merge",
    )(yr, yc, gates, gates, w_ret_o_bf, w_conv_o_bf, side_weight)


def _mixout_kernel(m_ref, w_ref, x_ref, g1_ref, g_ref, sc_ref, sh_ref, h_ref, xn_ref, *, n_sub):
    tr = m_ref.shape[0] // n_sub
    for s in range(n_sub):
        rows = pl.ds(s * tr, tr)
        mixed = jnp.dot(m_ref[rows, :], w_ref[...], preferred_element_type=F32)
        h = x_ref[rows, :] + g1_ref[...] * mixed
        h_ref[rows, :] = h
        xn_ref[rows, :] = _mod_rmsnorm(h_ref, rows, g_ref[...], sc_ref[...], sh_ref[...]).astype(BF16)


def _mixout(m, w_mix_o_bf, x2d, mod3, norm_g, seq, tm=512, n_sub=2):
    t, d = x2d.shape
    tps = seq // tm
    mod_spec = lambda k: pl.BlockSpec((None, 1, d), lambda i: (i // tps, 0, k))
    return pl.pallas_call(
        functools.partial(_mixout_kernel, n_sub=n_sub),
        grid=(t // tm,),
        in_specs=[pl.BlockSpec((tm, d), lambda i: (i, 0)),
                  pl.BlockSpec((d, d), lambda i: (0, 0), pipeline_mode=pl.Buffered(1)),
                  pl.BlockSpec((tm, d), lambda i: (i, 0)),
                  mod_spec(2), pl.BlockSpec((1, d), lambda i: (0, 0)), mod_spec(4), mod_spec(3)],
        out_specs=[pl.BlockSpec((tm, d), lambda i: (i, 0)), pl.BlockSpec((tm, d), lambda i: (i, 0))],
        out_shape=[jax.ShapeDtypeStruct((t, d), F32), jax.ShapeDtypeStruct((t, d), BF16)],
        compiler_params=_params("arbitrary"),
        name="mixout",
    )(m, w_mix_o_bf, x2d, mod3, norm_g.reshape(1, d), mod3, mod3)


def _ffn_kernel(xn_ref, wu_ref, wg_ref, wfc_ref, bfc_ref, wd_ref, h_hbm, g2_ref, fg_ref,
                o_ref, halo_ref, h_sem, *, tiles_per_seq, sub_width, last_width):
    i = pl.program_id(0)
    f = pl.program_id(1)
    last = pl.num_programs(1) - 1
    tm = o_ref.shape[0]

    @pl.when(jnp.logical_and(i == 0, f == 0))
    def _():
        halo_ref[...] = jnp.zeros_like(halo_ref)

    def h1_copy():
        return pltpu.make_async_copy(h_hbm.at[pl.ds(i * tm, tm), :], o_ref, h_sem)

    def step(first, width):
        if first:
            h1_copy().start()
        xn = xn_ref[...]
        not_seq_start = i % tiles_per_seq != 0
        col_blocks = [pl.ds(s * sub_width, sub_width) for s in range(width // sub_width)]
        up_gate = [(jnp.dot(xn, wu_ref[:, cols], preferred_element_type=F32),
                    jnp.dot(xn, wg_ref[:, cols], preferred_element_type=F32)) for cols in col_blocks]
        part = None
        for cols, (up, gpre) in zip(col_blocks, up_gate):
            prev = jnp.where(not_seq_start, halo_ref[f, :, cols], 0.0)
            gt = _causal_conv3(gpre, prev, wfc_ref[:, cols]) + bfc_ref[:, cols]
            halo_ref[f, :, cols] = gpre[gpre.shape[0] - HALO_ROWS:, :]
            act = (_silu(gt) * up).astype(BF16)
            down = jnp.dot(act, wd_ref[cols, :], preferred_element_type=F32)
            part = down if part is None else part + down
        if first:
            h1_copy().wait()
        o_ref[...] += g2_ref[...] * part

    tile_width = wu_ref.shape[1]
    pl.when(f == 0)(functools.partial(step, True, tile_width))
    pl.when(jnp.logical_and(f > 0, f < last))(functools.partial(step, False, tile_width))
    pl.when(f == last)(functools.partial(step, False, last_width))

    @pl.when(f == last)
    def _():
        r = lax.rsqrt(jnp.mean(jnp.square(o_ref[...]), axis=-1, keepdims=True) + EPS)
        o_ref[...] = (o_ref[...] * r) * fg_ref[...]


def _ffn(xn2, w_up_bf, w_gate_bf, w_ffconv, b_ffconv, w_down_bf, h1, mod3, final_g, seq,
         tm=1024, tf=768, sub_width=256):
    t, d = xn2.shape
    dff = w_up_bf.shape[1]
    nf = pl.cdiv(dff, tf)
    tps = seq // tm
    assert nf >= 2 and tf % sub_width == 0 and dff % sub_width == 0
    kern = functools.partial(_ffn_kernel, tiles_per_seq=tps, sub_width=sub_width,
                             last_width=dff - (nf - 1) * tf)
    return pl.pallas_call(
        kern,
        grid=(t // tm, nf),
        in_specs=[pl.BlockSpec((tm, d), lambda i, f: (i, 0)),
                  pl.BlockSpec((d, tf), lambda i, f: (0, f)),
                  pl.BlockSpec((d, tf), lambda i, f: (0, f)),
                  pl.BlockSpec((CONV_WIDTH, tf), lambda i, f: (0, f)),
                  pl.BlockSpec((1, tf), lambda i, f: (0, f)),
                  pl.BlockSpec((tf, d), lambda i, f: (f, 0)),
                  pl.BlockSpec(memory_space=pl.ANY),
                  pl.BlockSpec((None, 1, d), lambda i, f: (i // tps, 0, 5)),
                  pl.BlockSpec((1, d), lambda i, f: (0, 0))],
        out_specs=pl.BlockSpec((tm, d), lambda i, f: (i, 0)),
        out_shape=jax.ShapeDtypeStruct((t, d), F32),
        scratch_shapes=[pltpu.VMEM((nf, HALO_ROWS, tf), F32), pltpu.SemaphoreType.DMA(())],
        compiler_params=_params("arbitrary", "arbitrary"),
        name="ffn",
    )(xn2, w_up_bf, w_gate_bf, w_ffconv, b_ffconv.reshape(1, dff), w_down_bf, h1, mod3,
      final_g.reshape(1, d))


def kernel(x, c, positions, norm1_g, norm2_g, w_ada, b_ada, w_in, b_gate, w_sc, w_ret_o, w_conv_o,
           w_mix_o, w_up, w_gate, w_ffconv, b_ffconv, w_down, final_g):
    batch, seq, d = x.shape
    depth = w_in.shape[0]
    assert depth == 1, "final-norm fusion assumes a single layer"
    t = batch * seq
    h = x.reshape(t, d)
    for l in range(depth):
        mod, cos, sin = _ada_rope(c, w_ada[l], b_ada[l], positions)
        mod3 = mod.reshape(batch, 1, 6 * d)
        xn, cos, sin = _norm1(h, norm1_g[l], mod3, positions, cos, sin, seq)
        qkvg, w_ret_o_bf, w_conv_o_bf, w_mix_o_bf = _inproj_a(
            xn, w_in[l], cos.reshape(t, ROT_HALF), sin.reshape(t, ROT_HALF),
            [w_ret_o[l], w_conv_o[l], w_mix_o[l]])
        yc, w_gate_bf = _inproj_c(xn, w_in[l], w_sc[l], w_gate[l], seq)
        gates, w_up_bf = _inproj_g(xn, w_in[l], b_gate[l], w_up[l])
        yr = _retention(qkvg, batch, seq)
        merged, w_down_bf = _merge(yr, yc, gates, w_ret_o_bf, w_conv_o_bf, w_down[l])
        h1, xn2 = _mixout(merged, w_mix_o_bf, h, mod3, norm2_g[l], seq)
        h = _ffn(xn2, w_up_bf, w_gate_bf, w_ffconv[l], b_ffconv[l], w_down_bf, h1, mod3, final_g, seq)
    return h.reshape(batch, seq, d)
```

```python
import functools

import jax
import jax.numpy as jnp
from jax import lax
from jax.experimental import pallas as pl
from jax.experimental.pallas import tpu as pltpu

F32 = jnp.float32
BF16 = jnp.bfloat16

RET_HEADS = 8
RET_HEAD_DIM = 256
D_RET = RET_HEADS * RET_HEAD_DIM
ROT_HALF = RET_HEAD_DIM // 2
CONV_WIDTH = 3
ROPE_BASE = 10000.0
EPS = 1e-6
RET_CHUNK = 256
HALO_ROWS = 8

V7X_VMEM_LIMIT_BYTES = 56 * 1024 * 1024


def _params(*semantics):
    return pltpu.CompilerParams(dimension_semantics=semantics,
                                vmem_limit_bytes=V7X_VMEM_LIMIT_BYTES)


def _sigmoid(v):
    return 0.5 + 0.5 * jnp.tanh(0.5 * v)


def _silu(v):
    h = 0.5 * v
    return h + h * jnp.tanh(h)


def _mod_rmsnorm(src_ref, rows, gain, scale, shift):
    x = src_ref[rows, :]
    r = lax.rsqrt(jnp.mean(x * x, axis=-1, keepdims=True) + EPS)
    return (src_ref[rows, :] * r) * (gain * (1.0 + scale)) + shift


def _causal_conv3(u, prev, w):
    def taps(a):
        return w[0:1, :] * pltpu.roll(a, 2, 0) + w[1:2, :] * pltpu.roll(a, 1, 0) + w[2:3, :] * a

    head = taps(jnp.concatenate([prev, u[:HALO_ROWS, :]], axis=0))[HALO_ROWS:, :]
    return jnp.concatenate([head, taps(u)[HALO_ROWS:, :]], axis=0)


def _side_cast_specs(weights, steps, step_index):
    specs = [pl.BlockSpec((w.shape[0] // steps, w.shape[1]), lambda *g: (step_index(*g), 0)) for w in weights]
    shapes = [jax.ShapeDtypeStruct(w.shape, BF16) for w in weights]
    return specs, shapes


def _side_cast(in_refs, out_refs):
    for src, dst in zip(in_refs, out_refs):
        dst[...] = src[...].astype(BF16)


def _rope_angles(positions, tr):
    inv_freq = ROPE_BASE ** (-jnp.arange(ROT_HALF, dtype=F32) / ROT_HALF)
    return positions.reshape(positions.size // tr, 1, tr), inv_freq.reshape(ROT_HALF, 1)


def _rope_tables(pos_ref, inv_ref):
    ang_t = inv_ref[...] * pos_ref[...].astype(F32)
    return jnp.cos(ang_t).T, jnp.sin(ang_t).T


def _ada_rope_kernel(c_ref, w_ref, b_ref, pos_ref, inv_ref, mod_ref, cos_ref, sin_ref):
    @pl.when(pl.program_id(0) == 0)
    def _():
        mod_ref[...] = jnp.broadcast_to(b_ref[...], mod_ref.shape)

    ca = _silu(c_ref[...]).astype(BF16)
    mod_ref[...] += jnp.dot(ca, w_ref[...].astype(BF16), preferred_element_type=F32)
    cos, sin = _rope_tables(pos_ref, inv_ref)
    cos_ref[0] = cos
    sin_ref[0] = sin
    cos_ref[1] = jnp.zeros_like(cos)
    sin_ref[1] = jnp.zeros_like(sin)


def _ada_rope(c, w_ada, b_ada, positions, steps=8):
    b, d = c.shape
    n = w_ada.shape[1]
    half = positions.size // 2
    tk, tr = d // steps, half // steps
    pos, inv_freq = _rope_angles(positions, tr)
    rope_spec = pl.BlockSpec((2, tr, ROT_HALF), lambda j: (0, j, 0))
    return pl.pallas_call(
        _ada_rope_kernel,
        grid=(steps,),
        in_specs=[pl.BlockSpec((b, tk), lambda j: (0, j)),
                  pl.BlockSpec((tk, n), lambda j: (j, 0)),
                  pl.BlockSpec((1, n), lambda j: (0, 0)),
                  pl.BlockSpec((None, 1, tr), lambda j: (j, 0, 0)),
                  pl.BlockSpec((ROT_HALF, 1), lambda j: (0, 0))],
        out_specs=[pl.BlockSpec((b, n), lambda j: (0, 0)), rope_spec, rope_spec],
        out_shape=[jax.ShapeDtypeStruct((b, n), F32),
                   jax.ShapeDtypeStruct((2, half, ROT_HALF), F32),
                   jax.ShapeDtypeStruct((2, half, ROT_HALF), F32)],
        compiler_params=_params("arbitrary"),
        name="ada_rope",
    )(c, w_ada, b_ada.reshape(1, n), pos, inv_freq)


def _norm1_kernel(x_ref, g_ref, sc_ref, sh_ref, pos_ref, inv_ref, cos_in, sin_in, o_ref, cos_ref, sin_ref):
    del cos_in, sin_in
    o_ref[...] = _mod_rmsnorm(x_ref, slice(None), g_ref[...], sc_ref[...], sh_ref[...]).astype(BF16)
    cos_ref[...], sin_ref[...] = _rope_tables(pos_ref, inv_ref)


def _norm1(x2d, norm_g, mod3, positions, cos, sin, seq, tm=1024):
    t, d = x2d.shape
    tps = seq // tm
    steps = t // tm
    tr = cos.shape[1] // steps
    pos, inv_freq = _rope_angles(positions, tr)
    rope_spec = pl.BlockSpec((None, tr, ROT_HALF), lambda i: (1, i, 0))
    return pl.pallas_call(
        _norm1_kernel,
        grid=(steps,),
        in_specs=[pl.BlockSpec((tm, d), lambda i: (i, 0)),
                  pl.BlockSpec((1, d), lambda i: (0, 0)),
                  pl.BlockSpec((None, 1, d), lambda i: (i // tps, 0, 1)),
                  pl.BlockSpec((None, 1, d), lambda i: (i // tps, 0, 0)),
                  pl.BlockSpec((None, 1, tr), lambda i: (steps + i, 0, 0)),
                  pl.BlockSpec((ROT_HALF, 1), lambda i: (0, 0)),
                  pl.BlockSpec(memory_space=pl.ANY),
                  pl.BlockSpec(memory_space=pl.ANY)],
        out_specs=[pl.BlockSpec((tm, d), lambda i: (i, 0)), rope_spec, rope_spec],
        out_shape=[jax.ShapeDtypeStruct((t, d), BF16),
                   jax.ShapeDtypeStruct(cos.shape, F32), jax.ShapeDtypeStruct(sin.shape, F32)],
        input_output_aliases={6: 1, 7: 2},
        compiler_params=_params("arbitrary"),
        name="norm1",
    )(x2d, norm_g.reshape(1, d), mod3, mod3, pos, inv_freq, cos, sin)


def _inproj_a_kernel(xn_ref, w_ref, cos_ref, sin_ref, *rest, q_tiles, rot_tiles, gate_tile0, n_sub, n_side):
    side_in, (o_ref, *side_out, wbf_ref) = rest[:n_side], rest[n_side:]
    j = pl.program_id(0)

    @pl.when(pl.program_id(1) == 0)
    def _():
        wbf_ref[...] = w_ref[...].astype(BF16)

    _side_cast(side_in, side_out)
    heads = o_ref.shape[0]
    tr = xn_ref.shape[0] // n_sub

    def proj(rows):
        return jnp.dot(xn_ref[rows, :], wbf_ref[...], preferred_element_type=F32)

    @pl.when(j < rot_tiles)
    def _():
        scale = jnp.where(j < q_tiles, RET_HEAD_DIM ** -0.5, 1.0).astype(F32)
        for s in range(n_sub):
            rows = pl.ds(s * tr, tr)
            acc = proj(rows)
            cos = cos_ref[rows, :] * scale
            sin = sin_ref[rows, :] * scale
            for hh in range(heads):
                lo = hh * RET_HEAD_DIM
                x1 = acc[:, lo:lo + ROT_HALF]
                x2 = acc[:, lo + ROT_HALF:lo + RET_HEAD_DIM]
                o_ref[hh, rows, 0:ROT_HALF] = (x1 * cos - x2 * sin).astype(BF16)
                o_ref[hh, rows, ROT_HALF:RET_HEAD_DIM] = (x2 * cos + x1 * sin).astype(BF16)

    def plain(fn):
        for s in range(n_sub):
            rows = pl.ds(s * tr, tr)
            res = fn(proj(rows)).astype(BF16)
            for hh in range(heads):
                o_ref[hh, rows, :] = res[:, hh * RET_HEAD_DIM:(hh + 1) * RET_HEAD_DIM]

    pl.when(jnp.logical_and(j >= rot_tiles, j < gate_tile0))(functools.partial(plain, lambda a: a))
    pl.when(j >= gate_tile0)(functools.partial(plain, _silu))


def _inproj_a(xn, w_in, cos, sin, side_weights, tm=2048, tn=1024, n_sub=4):
    t, d = xn.shape
    n = 4 * D_RET
    heads = tn // RET_HEAD_DIM
    n_i = t // tm
    side_specs, side_shapes = _side_cast_specs(side_weights, (n // tn) * n_i, lambda j, i: j * n_i + i)
    kern = functools.partial(_inproj_a_kernel, q_tiles=D_RET // tn, rot_tiles=2 * D_RET // tn,
                             gate_tile0=3 * D_RET // tn, n_sub=n_sub, n_side=len(side_weights))
    return pl.pallas_call(
        kern,
        grid=(n // tn, n_i),
        in_specs=[pl.BlockSpec((tm, d), lambda j, i: (i, 0)),
                  pl.BlockSpec((d, tn), lambda j, i: (0, j)),
                  pl.BlockSpec((tm, ROT_HALF), lambda j, i: (i, 0)),
                  pl.BlockSpec((tm, ROT_HALF), lambda j, i: (i, 0))] + side_specs,
        out_specs=[pl.BlockSpec((heads, tm, RET_HEAD_DIM), lambda j, i: (j, i, 0))] + side_specs,
        out_shape=[jax.ShapeDtypeStruct((n // RET_HEAD_DIM, t, RET_HEAD_DIM), BF16)] + side_shapes,
        scratch_shapes=[pltpu.VMEM((d, tn), BF16)],
        compiler_params=_params("arbitrary", "arbitrary"),
        name="inproj_a",
    )(xn, w_in, cos, sin, *side_weights)


def _inproj_c_kernel(xn_ref, wb_ref, wc_ref, wx_ref, wsc_ref, side_ref, o_ref, side_out_ref,
                     wb_bf, wc_bf, wx_bf, halo_ref, *, tiles_per_seq):
    j = pl.program_id(0)
    i = pl.program_id(1)

    @pl.when(jnp.logical_and(i == 0, j == 0))
    def _():
        halo_ref[...] = jnp.zeros_like(halo_ref)

    @pl.when(i == 0)
    def _():
        wb_bf[...] = wb_ref[...].astype(BF16)
        wc_bf[...] = wc_ref[...].astype(BF16)
        wx_bf[...] = wx_ref[...].astype(BF16)

    _side_cast([side_ref], [side_out_ref])
    xn = xn_ref[...]
    cg = jnp.dot(xn, wc_bf[...], preferred_element_type=F32)
    xc = jnp.dot(xn, wx_bf[...], preferred_element_type=F32)
    u = cg * xc
    prev = jnp.where(i % tiles_per_seq == 0, 0.0, halo_ref[...])
    conv = _causal_conv3(u, prev, wsc_ref[...])
    halo_ref[...] = u[u.shape[0] - HALO_ROWS:, :]
    bg = jnp.dot(xn, wb_bf[...], preferred_element_type=F32)
    o_ref[...] = (bg * conv).astype(BF16)


def _inproj_c(xn, w_in, w_sc, side_weight, seq, tm=1024, tc=512):
    t, d = xn.shape
    nc = d // tc
    n_i = t // tm
    first = 4 * D_RET // tc
    kern = functools.partial(_inproj_c_kernel, tiles_per_seq=seq // tm)
    wspec = lambda sec: pl.BlockSpec((d, tc), lambda j, i: (0, first + sec * nc + j))
    side_specs, side_shapes = _side_cast_specs([side_weight], nc * n_i, lambda j, i: j * n_i + i)
    return pl.pallas_call(
        kern,
        grid=(nc, n_i),
        in_specs=[pl.BlockSpec((tm, d), lambda j, i: (i, 0)),
                  wspec(0), wspec(1), wspec(2),
                  pl.BlockSpec((CONV_WIDTH, tc), lambda j, i: (0, j))] + side_specs,
        out_specs=[pl.BlockSpec((tm, tc), lambda j, i: (i, j))] + side_specs,
        out_shape=[jax.ShapeDtypeStruct((t, d), BF16)] + side_shapes,
        scratch_shapes=[pltpu.VMEM((d, tc), BF16)] * 3 + [pltpu.VMEM((HALO_ROWS, tc), F32)],
        compiler_params=_params("arbitrary", "arbitrary"),
        name="inproj_c",
    )(xn, w_in, w_in, w_in, w_sc, side_weight)


def _inproj_g_kernel(xn_ref, w_ref, b_ref, side_ref, o_ref, side_out_ref, wbf_ref, *, n_sub):
    @pl.when(pl.program_id(1) == 0)
    def _():
        wbf_ref[...] = w_ref[...].astype(BF16)

    _side_cast([side_ref], [side_out_ref])
    tr = xn_ref.shape[0] // n_sub
    for s in range(n_sub):
        rows = pl.ds(s * tr, tr)
        logits = jnp.dot(xn_ref[rows, :], wbf_ref[...], preferred_element_type=F32) + b_ref[...]
        o_ref[rows, :] = _sigmoid(logits).astype(BF16)


def _inproj_g(xn, w_in, b_gate, side_weight, tm=2048, tn=1024, n_sub=4):
    t, d = xn.shape
    n = b_gate.shape[0]
    n_i = t // tm
    first = (w_in.shape[1] - n) // tn
    side_specs, side_shapes = _side_cast_specs([side_weight], (n // tn) * n_i, lambda j, i: j * n_i + i)
    return pl.pallas_call(
        functools.partial(_inproj_g_kernel, n_sub=n_sub),
        grid=(n // tn, n_i),
        in_specs=[pl.BlockSpec((tm, d), lambda j, i: (i, 0)),
                  pl.BlockSpec((d, tn), lambda j, i: (0, first + j)),
                  pl.BlockSpec((1, tn), lambda j, i: (0, j))] + side_specs,
        out_specs=[pl.BlockSpec((tm, tn), lambda j, i: (i, j))] + side_specs,
        out_shape=[jax.ShapeDtypeStruct((t, n), BF16)] + side_shapes,
        scratch_shapes=[pltpu.VMEM((d, tn), BF16)],
        compiler_params=_params("arbitrary", "arbitrary"),
        name="inproj_g",
    )(xn, w_in, b_gate.reshape(1, n), side_weight)


def _ret_kernel(q_ref, k_ref, v_ref, gs_ref, o_ref):
    heads = q_ref.shape[0]
    first = pl.program_id(1) * heads

    def prepare(hh):
        return _ret_scores_and_states(first + hh, q_ref.at[hh], k_ref.at[hh], v_ref.at[hh])

    prep = prepare(0)
    for hh in range(heads):
        next_prep = prepare(hh + 1) if hh + 1 < heads else None
        cols = pl.ds(hh * RET_HEAD_DIM, RET_HEAD_DIM)
        _ret_outputs(prep, q_ref.at[hh], v_ref.at[hh], gs_ref.at[hh], o_ref.at[:, cols])
        prep = next_prep


def _ret_scores_and_states(head_index, q_ref, k_ref, v_ref):
    c = RET_CHUNK
    n_chunks = q_ref.shape[0] // c
    head = jnp.full((1, RET_HEAD_DIM), head_index, jnp.int32).astype(F32)
    log_gamma = jnp.log(1.0 - jnp.exp2(-5.0 - head))
    row = lax.broadcasted_iota(jnp.int32, (c, c), 0).astype(F32)
    col = lax.broadcasted_iota(jnp.int32, (c, c), 1).astype(F32)
    diff = row - col
    decay = jnp.where(diff >= 0, jnp.exp(log_gamma * jnp.maximum(diff, 0.0)), 0.0)
    q_dec = jnp.exp(log_gamma * (row + 1.0))
    k_dec = jnp.exp(log_gamma * (c - 1.0 - row))
    chunk_dec = jnp.exp(log_gamma * float(c))
    chunks = [pl.ds(n * c, c) for n in range(n_chunks)]

    kvs = []
    for rows in chunks[:-1]:
        kd = (k_ref[rows, :].astype(F32) * k_dec).astype(BF16)
        kvs.append(lax.dot_general(kd, v_ref[rows, :], (((0,), (0,)), ((), ())),
                                   preferred_element_type=F32))
    scores = [lax.dot_general(q_ref[rows, :], k_ref[rows, :], (((1,), (1,)), ((), ())),
                              preferred_element_type=F32) for rows in chunks]
    states = [None]
    for kv in kvs:
        states.append(kv if states[-1] is None else chunk_dec * states[-1] + kv)
    return chunks, scores, states, decay, q_dec


def _ret_outputs(prep, q_ref, v_ref, gs_ref, o_ref):
    chunks, scores, states, decay, q_dec = prep
    for rows, s, state in zip(chunks, scores, states):
        y = jnp.dot((s * decay).astype(BF16), v_ref[rows, :], preferred_element_type=F32)
        if state is not None:
            y = y + q_dec * jnp.dot(q_ref[rows, :], state.astype(BF16), preferred_element_type=F32)
        yn = y * lax.rsqrt(jnp.mean(y * y, axis=-1, keepdims=True) + EPS)
        o_ref[rows, :] = (yn * gs_ref[rows, :].astype(F32)).astype(BF16)


def _retention(qkvg, batch, seq, heads_per_step=4):
    t = qkvg.shape[1]
    groups = RET_HEADS // heads_per_step
    spec = lambda sec: pl.BlockSpec((heads_per_step, seq, RET_HEAD_DIM),
                                    lambda b, g: (sec * groups + g, b, 0))
    return pl.pallas_call(
        _ret_kernel,
        grid=(batch, groups),
        in_specs=[spec(0), spec(1), spec(2), spec(3)],
        out_specs=pl.BlockSpec((seq, heads_per_step * RET_HEAD_DIM), lambda b, g: (b, g)),
        out_shape=jax.ShapeDtypeStruct((t, D_RET), BF16),
        compiler_params=_params("arbitrary", "arbitrary"),
        name="ret",
    )(qkvg, qkvg, qkvg, qkvg)


def _merge_kernel(yr_ref, yc_ref, ga_ref, gb_ref, wr_ref, wc_ref, side_ref, o_ref, side_out_ref):
    _side_cast([side_ref], [side_out_ref])
    ya = jnp.dot(yr_ref[...], wr_ref[...], preferred_element_type=F32)
    yb = jnp.dot(yc_ref[...], wc_ref[...], preferred_element_type=F32)
    o_ref[...] = (ga_ref[...].astype(F32) * ya + gb_ref[...].astype(F32) * yb).astype(BF16)


def _merge(yr, yc, gates, w_ret_o_bf, w_conv_o_bf, side_weight, tm=512):
    t, d = yr.shape
    resident = lambda: pl.BlockSpec((d, d), lambda i: (0, 0), pipeline_mode=pl.Buffered(1))
    side_specs, side_shapes = _side_cast_specs([side_weight], t // tm, lambda i: i)
    return pl.pallas_call(
        _merge_kernel,
        grid=(t // tm,),
        in_specs=[pl.BlockSpec((tm, d), lambda i: (i, 0)),
                  pl.BlockSpec((tm, d), lambda i: (i, 0)),
                  pl.BlockSpec((tm, d), lambda i: (i, 0)),
                  pl.BlockSpec((tm, d), lambda i: (i, 1)),
                  resident(), resident()] + side_specs,
        out_specs=[pl.BlockSpec((tm, d), lambda i: (i, 0))] + side_specs,
        out_shape=[jax.ShapeDtypeStruct((t, d), BF16)] + side_shapes,
        compiler_params=_params("arbitrary"),
        name="merge",
    )(yr, yc, gates, gates, w_ret_o_bf, w_conv_o_bf, side_weight)


def _mixout_kernel(m_ref, w_ref, x_ref, g1_ref, g_ref, sc_ref, sh_ref, h_ref, xn_ref, *, n_sub):
    tr = m_ref.shape[0] // n_sub
    for s in range(n_sub):
        rows = pl.ds(s * tr, tr)
        mixed = jnp.dot(m_ref[rows, :], w_ref[...], preferred_element_type=F32)
        h = x_ref[rows, :] + g1_ref[...] * mixed
        h_ref[rows, :] = h
        xn_ref[rows, :] = _mod_rmsnorm(h_ref, rows, g_ref[...], sc_ref[...], sh_ref[...]).astype(BF16)


def _mixout(m, w_mix_o_bf, x2d, mod3, norm_g, seq, tm=512, n_sub=2):
    t, d = x2d.shape
    tps = seq // tm
    mod_spec = lambda k: pl.BlockSpec((None, 1, d), lambda i: (i // tps, 0, k))
    return pl.pallas_call(
        functools.partial(_mixout_kernel, n_sub=n_sub),
        grid=(t // tm,),
        in_specs=[pl.BlockSpec((tm, d), lambda i: (i, 0)),
                  pl.BlockSpec((d, d), lambda i: (0, 0), pipeline_mode=pl.Buffered(1)),
                  pl.BlockSpec((tm, d), lambda i: (i, 0)),
                  mod_spec(2), pl.BlockSpec((1, d), lambda i: (0, 0)), mod_spec(4), mod_spec(3)],
        out_specs=[pl.BlockSpec((tm, d), lambda i: (i, 0)), pl.BlockSpec((tm, d), lambda i: (i, 0))],
        out_shape=[jax.ShapeDtypeStruct((t, d), F32), jax.ShapeDtypeStruct((t, d), BF16)],
        compiler_params=_params("arbitrary"),
        name="mixout",
    )(m, w_mix_o_bf, x2d, mod3, norm_g.reshape(1, d), mod3, mod3)


def _ffn_kernel(xn_ref, wu_ref, wg_ref, wfc_ref, bfc_ref, wd_ref, h_hbm, g2_ref, fg_ref,
                o_ref, halo_ref, h_sem, *, tiles_per_seq, sub_width, last_width):
    i = pl.program_id(0)
    f = pl.program_id(1)
    last = pl.num_programs(1) - 1
    tm = o_ref.shape[0]

    @pl.when(jnp.logical_and(i == 0, f == 0))
    def _():
        halo_ref[...] = jnp.zeros_like(halo_ref)

    def h1_copy():
        return pltpu.make_async_copy(h_hbm.at[pl.ds(i * tm, tm), :], o_ref, h_sem)

    def step(first, width):
        if first:
            h1_copy().start()
        xn = xn_ref[...]
        not_seq_start = i % tiles_per_seq != 0
        col_blocks = [pl.ds(s * sub_width, sub_width) for s in range(width // sub_width)]
        up_gate = [(jnp.dot(xn, wu_ref[:, cols], preferred_element_type=F32),
                    jnp.dot(xn, wg_ref[:, cols], preferred_element_type=F32)) for cols in col_blocks]
        part = None
        for cols, (up, gpre) in zip(col_blocks, up_gate):
            prev = jnp.where(not_seq_start, halo_ref[f, :, cols], 0.0)
            gt = _causal_conv3(gpre, prev, wfc_ref[:, cols]) + bfc_ref[:, cols]
            halo_ref[f, :, cols] = gpre[gpre.shape[0] - HALO_ROWS:, :]
            act = (_silu(gt) * up).astype(BF16)
            down = jnp.dot(act, wd_ref[cols, :], preferred_element_type=F32)
            part = down if part is None else part + down
        if first:
            h1_copy().wait()
        o_ref[...] += g2_ref[...] * part

    tile_width = wu_ref.shape[1]
    pl.when(f == 0)(functools.partial(step, True, tile_width))
    pl.when(jnp.logical_and(f > 0, f < last))(functools.partial(step, False, tile_width))
    pl.when(f == last)(functools.partial(step, False, last_width))

    @pl.when(f == last)
    def _():
        r = lax.rsqrt(jnp.mean(jnp.square(o_ref[...]), axis=-1, keepdims=True) + EPS)
        o_ref[...] = (o_ref[...] * r) * fg_ref[...]


def _ffn(xn2, w_up_bf, w_gate_bf, w_ffconv, b_ffconv, w_down_bf, h1, mod3, final_g, seq,
         tm=1024, tf=768, sub_width=256):
    t, d = xn2.shape
    dff = w_up_bf.shape[1]
    nf = pl.cdiv(dff, tf)
    tps = seq // tm
    assert nf >= 2 and tf % sub_width == 0 and dff % sub_width == 0
    kern = functools.partial(_ffn_kernel, tiles_per_seq=tps, sub_width=sub_width,
                             last_width=dff - (nf - 1) * tf)
    return pl.pallas_call(
        kern,
        grid=(t // tm, nf),
        in_specs=[pl.BlockSpec((tm, d), lambda i, f: (i, 0)),
                  pl.BlockSpec((d, tf), lambda i, f: (0, f)),
                  pl.BlockSpec((d, tf), lambda i, f: (0, f)),
                  pl.BlockSpec((CONV_WIDTH, tf), lambda i, f: (0, f)),
                  pl.BlockSpec((1, tf), lambda i, f: (0, f)),
                  pl.BlockSpec((tf, d), lambda i, f: (f, 0)),
                  pl.BlockSpec(memory_space=pl.ANY),
                  pl.BlockSpec((None, 1, d), lambda i, f: (i // tps, 0, 5)),
                  pl.BlockSpec((1, d), lambda i, f: (0, 0))],
        out_specs=pl.BlockSpec((tm, d), lambda i, f: (i, 0)),
        out_shape=jax.ShapeDtypeStruct((t, d), F32),
        scratch_shapes=[pltpu.VMEM((nf, HALO_ROWS, tf), F32), pltpu.SemaphoreType.DMA(())],
        compiler_params=_params("arbitrary", "arbitrary"),
        name="ffn",
    )(xn2, w_up_bf, w_gate_bf, w_ffconv, b_ffconv.reshape(1, dff), w_down_bf, h1, mod3,
      final_g.reshape(1, d))


def kernel(x, c, positions, norm1_g, norm2_g, w_ada, b_ada, w_in, b_gate, w_sc, w_ret_o, w_conv_o,
           w_mix_o, w_up, w_gate, w_ffconv, b_ffconv, w_down, final_g):
    batch, seq, d = x.shape
    depth = w_in.shape[0]
    assert depth == 1, "final-norm fusion assumes a single layer"
    assert d == D_RET and w_in.shape[2] == 4 * D_RET + 5 * d, (d, w_in.shape)
    assert seq % 2048 == 0 and seq % RET_CHUNK == 0 and batch % 2 == 0, (batch, seq)
    t = batch * seq
    h = x.reshape(t, d)
    for l in range(depth):
        mod, cos, sin = _ada_rope(c, w_ada[l], b_ada[l], positions)
        mod3 = mod.reshape(batch, 1, 6 * d)
        xn, cos, sin = _norm1(h, norm1_g[l], mod3, positions, cos, sin, seq)
        qkvg, w_ret_o_bf, w_conv_o_bf, w_mix_o_bf = _inproj_a(
            xn, w_in[l], cos.reshape(t, ROT_HALF), sin.reshape(t, ROT_HALF),
            [w_ret_o[l], w_conv_o[l], w_mix_o[l]])
        yc, w_gate_bf = _inproj_c(xn, w_in[l], w_sc[l], w_gate[l], seq)
        gates, w_up_bf = _inproj_g(xn, w_in[l], b_gate[l], w_up[l])
        yr = _retention(qkvg, batch, seq)
        merged, w_down_bf = _merge(yr, yc, gates, w_ret_o_bf, w_conv_o_bf, w_down[l])
        h1, xn2 = _mixout(merged, w_mix_o_bf, h, mod3, norm2_g[l], seq)
        h = _ffn(xn2, w_up_bf, w_gate_bf, w_ffconv[l], b_ffconv[l], w_down_bf, h1, mod3, final_g, seq)
    return h.reshape(batch, seq, d)
```

```python
import functools

import jax
import jax.numpy as jnp
from jax import lax
from jax.experimental import pallas as pl
from jax.experimental.pallas import tpu as pltpu

F32 = jnp.float32
BF16 = jnp.bfloat16

RET_HEADS = 8
RET_HEAD_DIM = 256
D_RET = RET_HEADS * RET_HEAD_DIM
ROT_HALF = RET_HEAD_DIM // 2
CONV_WIDTH = 3
ROPE_BASE = 10000.0
EPS = 1e-6
RET_CHUNK = 256
HALO_ROWS = 8

V7X_VMEM_LIMIT_BYTES = 56 * 1024 * 1024


def _params(*semantics):
    return pltpu.CompilerParams(dimension_semantics=semantics,
                                vmem_limit_bytes=V7X_VMEM_LIMIT_BYTES)


def _sigmoid(v):
    return 0.5 + 0.5 * jnp.tanh(0.5 * v)


def _silu(v):
    h = 0.5 * v
    return h + h * jnp.tanh(h)


def _mod_rmsnorm(src_ref, rows, gain, scale, shift):
    x = src_ref[rows, :]
    r = lax.rsqrt(jnp.mean(x * x, axis=-1, keepdims=True) + EPS)
    return (src_ref[rows, :] * r) * (gain * (1.0 + scale)) + shift


def _causal_conv3(u, prev, w):
    def taps(a):
        return w[0:1, :] * pltpu.roll(a, 2, 0) + w[1:2, :] * pltpu.roll(a, 1, 0) + w[2:3, :] * a

    head = taps(jnp.concatenate([prev, u[:HALO_ROWS, :]], axis=0))[HALO_ROWS:, :]
    return jnp.concatenate([head, taps(u)[HALO_ROWS:, :]], axis=0)


def _side_cast_specs(weights, steps, step_index):
    specs = [pl.BlockSpec((w.shape[0] // steps, w.shape[1]), lambda *g: (step_index(*g), 0)) for w in weights]
    shapes = [jax.ShapeDtypeStruct(w.shape, BF16) for w in weights]
    return specs, shapes


def _side_cast(in_refs, out_refs):
    for src, dst in zip(in_refs, out_refs):
        dst[...] = src[...].astype(BF16)


def _rope_angles(positions, tr):
    inv_freq = ROPE_BASE ** (-jnp.arange(ROT_HALF, dtype=F32) / ROT_HALF)
    return positions.reshape(positions.size // tr, 1, tr), inv_freq.reshape(ROT_HALF, 1)


def _rope_tables(pos_ref, inv_ref):
    ang_t = inv_ref[...] * pos_ref[...].astype(F32)
    return jnp.cos(ang_t).T, jnp.sin(ang_t).T


def _ada_rope_kernel(c_ref, w_ref, b_ref, pos_ref, inv_ref, mod_ref, cos_ref, sin_ref):
    @pl.when(pl.program_id(0) == 0)
    def _():
        mod_ref[...] = jnp.broadcast_to(b_ref[...], mod_ref.shape)

    ca = _silu(c_ref[...]).astype(BF16)
    mod_ref[...] += jnp.dot(ca, w_ref[...].astype(BF16), preferred_element_type=F32)
    cos, sin = _rope_tables(pos_ref, inv_ref)
    cos_ref[0] = cos
    sin_ref[0] = sin
    cos_ref[1] = jnp.zeros_like(cos)
    sin_ref[1] = jnp.zeros_like(sin)


def _ada_rope(c, w_ada, b_ada, positions, steps=8):
    b, d = c.shape
    n = w_ada.shape[1]
    half = positions.size // 2
    tk, tr = d // steps, half // steps
    pos, inv_freq = _rope_angles(positions, tr)
    rope_spec = pl.BlockSpec((2, tr, ROT_HALF), lambda j: (0, j, 0))
    return pl.pallas_call(
        _ada_rope_kernel,
        grid=(steps,),
        in_specs=[pl.BlockSpec((b, tk), lambda j: (0, j)),
                  pl.BlockSpec((tk, n), lambda j: (j, 0)),
                  pl.BlockSpec((1, n), lambda j: (0, 0)),
                  pl.BlockSpec((None, 1, tr), lambda j: (j, 0, 0)),
                  pl.BlockSpec((ROT_HALF, 1), lambda j: (0, 0))],
        out_specs=[pl.BlockSpec((b, n), lambda j: (0, 0)), rope_spec, rope_spec],
        out_shape=[jax.ShapeDtypeStruct((b, n), F32),
                   jax.ShapeDtypeStruct((2, half, ROT_HALF), F32),
                   jax.ShapeDtypeStruct((2, half, ROT_HALF), F32)],
        compiler_params=_params("arbitrary"),
        name="ada_rope",
    )(c, w_ada, b_ada.reshape(1, n), pos, inv_freq)


def _norm1_kernel(x_ref, g_ref, sc_ref, sh_ref, pos_ref, inv_ref, cos_in, sin_in, o_ref, cos_ref, sin_ref):
    del cos_in, sin_in
    o_ref[...] = _mod_rmsnorm(x_ref, slice(None), g_ref[...], sc_ref[...], sh_ref[...]).astype(BF16)
    cos_ref[...], sin_ref[...] = _rope_tables(pos_ref, inv_ref)


def _norm1(x2d, norm_g, mod3, positions, cos, sin, seq, tm=1024):
    t, d = x2d.shape
    tps = seq // tm
    steps = t // tm
    tr = cos.shape[1] // steps
    pos, inv_freq = _rope_angles(positions, tr)
    rope_spec = pl.BlockSpec((None, tr, ROT_HALF), lambda i: (1, i, 0))
    return pl.pallas_call(
        _norm1_kernel,
        grid=(steps,),
        in_specs=[pl.BlockSpec((tm, d), lambda i: (i, 0)),
                  pl.BlockSpec((1, d), lambda i: (0, 0)),
                  pl.BlockSpec((None, 1, d), lambda i: (i // tps, 0, 1)),
                  pl.BlockSpec((None, 1, d), lambda i: (i // tps, 0, 0)),
                  pl.BlockSpec((None, 1, tr), lambda i: (steps + i, 0, 0)),
                  pl.BlockSpec((ROT_HALF, 1), lambda i: (0, 0)),
                  pl.BlockSpec(memory_space=pl.ANY),
                  pl.BlockSpec(memory_space=pl.ANY)],
        out_specs=[pl.BlockSpec((tm, d), lambda i: (i, 0)), rope_spec, rope_spec],
        out_shape=[jax.ShapeDtypeStruct((t, d), BF16),
                   jax.ShapeDtypeStruct(cos.shape, F32), jax.ShapeDtypeStruct(sin.shape, F32)],
        input_output_aliases={6: 1, 7: 2},
        compiler_params=_params("arbitrary"),
        name="norm1",
    )(x2d, norm_g.reshape(1, d), mod3, mod3, pos, inv_freq, cos, sin)


def _inproj_a_kernel(xn_ref, w_ref, cos_ref, sin_ref, *rest, q_tiles, rot_tiles, gate_tile0, n_sub, n_side):
    side_in, (o_ref, *side_out, wbf_ref) = rest[:n_side], rest[n_side:]
    j = pl.program_id(0)

    @pl.when(pl.program_id(1) == 0)
    def _():
        wbf_ref[...] = w_ref[...].astype(BF16)

    _side_cast(side_in, side_out)
    heads = o_ref.shape[0]
    tr = xn_ref.shape[0] // n_sub

    def proj(rows):
        return jnp.dot(xn_ref[rows, :], wbf_ref[...], preferred_element_type=F32)

    @pl.when(j < rot_tiles)
    def _():
        scale = jnp.where(j < q_tiles, RET_HEAD_DIM ** -0.5, 1.0).astype(F32)
        for s in range(n_sub):
            rows = pl.ds(s * tr, tr)
            acc = proj(rows)
            cos = cos_ref[rows, :] * scale
            sin = sin_ref[rows, :] * scale
            for hh in range(heads):
                lo = hh * RET_HEAD_DIM
                x1 = acc[:, lo:lo + ROT_HALF]
                x2 = acc[:, lo + ROT_HALF:lo + RET_HEAD_DIM]
                o_ref[hh, rows, 0:ROT_HALF] = (x1 * cos - x2 * sin).astype(BF16)
                o_ref[hh, rows, ROT_HALF:RET_HEAD_DIM] = (x2 * cos + x1 * sin).astype(BF16)

    def plain(fn):
        for s in range(n_sub):
            rows = pl.ds(s * tr, tr)
            res = fn(proj(rows)).astype(BF16)
            for hh in range(heads):
                o_ref[hh, rows, :] = res[:, hh * RET_HEAD_DIM:(hh + 1) * RET_HEAD_DIM]

    pl.when(jnp.logical_and(j >= rot_tiles, j < gate_tile0))(functools.partial(plain, lambda a: a))
    pl.when(j >= gate_tile0)(functools.partial(plain, _silu))


def _inproj_a(xn, w_in, cos, sin, side_weights, tm=2048, tn=1024, n_sub=4):
    t, d = xn.shape
    n = 4 * D_RET
    heads = tn // RET_HEAD_DIM
    n_i = t // tm
    side_specs, side_shapes = _side_cast_specs(side_weights, (n // tn) * n_i, lambda j, i: j * n_i + i)
    kern = functools.partial(_inproj_a_kernel, q_tiles=D_RET // tn, rot_tiles=2 * D_RET // tn,
                             gate_tile0=3 * D_RET // tn, n_sub=n_sub, n_side=len(side_weights))
    return pl.pallas_call(
        kern,
        grid=(n // tn, n_i),
        in_specs=[pl.BlockSpec((tm, d), lambda j, i: (i, 0)),
                  pl.BlockSpec((d, tn), lambda j, i: (0, j)),
                  pl.BlockSpec((tm, ROT_HALF), lambda j, i: (i, 0)),
                  pl.BlockSpec((tm, ROT_HALF), lambda j, i: (i, 0))] + side_specs,
        out_specs=[pl.BlockSpec((heads, tm, RET_HEAD_DIM), lambda j, i: (j, i, 0))] + side_specs,
        out_shape=[jax.ShapeDtypeStruct((n // RET_HEAD_DIM, t, RET_HEAD_DIM), BF16)] + side_shapes,
        scratch_shapes=[pltpu.VMEM((d, tn), BF16)],
        compiler_params=_params("arbitrary", "arbitrary"),
        name="inproj_a",
    )(xn, w_in, cos, sin, *side_weights)


def _inproj_c_kernel(xn_ref, wb_ref, wc_ref, wx_ref, wsc_ref, side_ref, o_ref, side_out_ref,
                     wb_bf, wc_bf, wx_bf, halo_ref, *, tiles_per_seq):
    j = pl.program_id(0)
    i = pl.program_id(1)

    @pl.when(jnp.logical_and(i == 0, j == 0))
    def _():
        halo_ref[...] = jnp.zeros_like(halo_ref)

    @pl.when(i == 0)
    def _():
        wb_bf[...] = wb_ref[...].astype(BF16)
        wc_bf[...] = wc_ref[...].astype(BF16)
        wx_bf[...] = wx_ref[...].astype(BF16)

    _side_cast([side_ref], [side_out_ref])
    xn = xn_ref[...]
    cg = jnp.dot(xn, wc_bf[...], preferred_element_type=F32)
    xc = jnp.dot(xn, wx_bf[...], preferred_element_type=F32)
    u = cg * xc
    prev = jnp.where(i % tiles_per_seq == 0, 0.0, halo_ref[...])
    conv = _causal_conv3(u, prev, wsc_ref[...])
    halo_ref[...] = u[u.shape[0] - HALO_ROWS:, :]
    bg = jnp.dot(xn, wb_bf[...], preferred_element_type=F32)
    o_ref[...] = (bg * conv).astype(BF16)


def _inproj_c(xn, w_in, w_sc, side_weight, seq, tm=1024, tc=512):
    t, d = xn.shape
    nc = d // tc
    n_i = t // tm
    first = 4 * D_RET // tc
    kern = functools.partial(_inproj_c_kernel, tiles_per_seq=seq // tm)
    wspec = lambda sec: pl.BlockSpec((d, tc), lambda j, i: (0, first + sec * nc + j))
    side_specs, side_shapes = _side_cast_specs([side_weight], nc * n_i, lambda j, i: j * n_i + i)
    return pl.pallas_call(
        kern,
        grid=(nc, n_i),
        in_specs=[pl.BlockSpec((tm, d), lambda j, i: (i, 0)),
                  wspec(0), wspec(1), wspec(2),
                  pl.BlockSpec((CONV_WIDTH, tc), lambda j, i: (0, j))] + side_specs,
        out_specs=[pl.BlockSpec((tm, tc), lambda j, i: (i, j))] + side_specs,
        out_shape=[jax.ShapeDtypeStruct((t, d), BF16)] + side_shapes,
        scratch_shapes=[pltpu.VMEM((d, tc), BF16)] * 3 + [pltpu.VMEM((HALO_ROWS, tc), F32)],
        compiler_params=_params("arbitrary", "arbitrary"),
        name="inproj_c",
    )(xn, w_in, w_in, w_in, w_sc, side_weight)


def _inproj_g_kernel(xn_ref, w_ref, b_ref, side_ref, o_ref, side_out_ref, wbf_ref, *, n_sub):
    @pl.when(pl.program_id(1) == 0)
    def _():
        wbf_ref[...] = w_ref[...].astype(BF16)

    _side_cast([side_ref], [side_out_ref])
    tr = xn_ref.shape[0] // n_sub
    for s in range(n_sub):
        rows = pl.ds(s * tr, tr)
        logits = jnp.dot(xn_ref[rows, :], wbf_ref[...], preferred_element_type=F32) + b_ref[...]
        o_ref[rows, :] = _sigmoid(logits).astype(BF16)


def _inproj_g(xn, w_in, b_gate, side_weight, tm=2048, tn=1024, n_sub=4):
    t, d = xn.shape
    n = b_gate.shape[0]
    n_i = t // tm
    first = (w_in.shape[1] - n) // tn
    side_specs, side_shapes = _side_cast_specs([side_weight], (n // tn) * n_i, lambda j, i: j * n_i + i)
    return pl.pallas_call(
        functools.partial(_inproj_g_kernel, n_sub=n_sub),
        grid=(n // tn, n_i),
        in_specs=[pl.BlockSpec((tm, d), lambda j, i: (i, 0)),
                  pl.BlockSpec((d, tn), lambda j, i: (0, first + j)),
                  pl.BlockSpec((1, tn), lambda j, i: (0, j))] + side_specs,
        out_specs=[pl.BlockSpec((tm, tn), lambda j, i: (i, j))] + side_specs,
        out_shape=[jax.ShapeDtypeStruct((t, n), BF16)] + side_shapes,
        scratch_shapes=[pltpu.VMEM((d, tn), BF16)],
        compiler_params=_params("arbitrary", "arbitrary"),
        name="inproj_g",
    )(xn, w_in, b_gate.reshape(1, n), side_weight)


def _ret_kernel(q_ref, k_ref, v_ref, gs_ref, o_ref):
    heads = q_ref.shape[0]
    first = pl.program_id(1) * heads

    def prepare(hh):
        return _ret_scores_and_states(first + hh, q_ref.at[hh], k_ref.at[hh], v_ref.at[hh])

    prep = prepare(0)
    for hh in range(heads):
        next_prep = prepare(hh + 1) if hh + 1 < heads else None
        cols = pl.ds(hh * RET_HEAD_DIM, RET_HEAD_DIM)
        _ret_outputs(prep, q_ref.at[hh], v_ref.at[hh], gs_ref.at[hh], o_ref.at[:, cols])
        prep = next_prep


def _ret_scores_and_states(head_index, q_ref, k_ref, v_ref):
    c = RET_CHUNK
    n_chunks = q_ref.shape[0] // c
    head = jnp.full((1, RET_HEAD_DIM), head_index, jnp.int32).astype(F32)
    log_gamma = jnp.log(1.0 - jnp.exp2(-5.0 - head))
    row = lax.broadcasted_iota(jnp.int32, (c, c), 0).astype(F32)
    col = lax.broadcasted_iota(jnp.int32, (c, c), 1).astype(F32)
    diff = row - col
    decay = jnp.where(diff >= 0, jnp.exp(log_gamma * jnp.maximum(diff, 0.0)), 0.0)
    q_dec = jnp.exp(log_gamma * (row + 1.0))
    k_dec = jnp.exp(log_gamma * (c - 1.0 - row))
    chunk_dec = jnp.exp(log_gamma * float(c))
    chunks = [pl.ds(n * c, c) for n in range(n_chunks)]

    kvs = []
    for rows in chunks[:-1]:
        kd = (k_ref[rows, :].astype(F32) * k_dec).astype(BF16)
        kvs.append(lax.dot_general(kd, v_ref[rows, :], (((0,), (0,)), ((), ())),
                                   preferred_element_type=F32))
    scores = [lax.dot_general(q_ref[rows, :], k_ref[rows, :], (((1,), (1,)), ((), ())),
                              preferred_element_type=F32) for rows in chunks]
    states = [None]
    for kv in kvs:
        states.append(kv if states[-1] is None else chunk_dec * states[-1] + kv)
    return chunks, scores, states, decay, q_dec


def _ret_outputs(prep, q_ref, v_ref, gs_ref, o_ref):
    chunks, scores, states, decay, q_dec = prep
    for rows, s, state in zip(chunks, scores, states):
        y = jnp.dot((s * decay).astype(BF16), v_ref[rows, :], preferred_element_type=F32)
        if state is not None:
            y = y + q_dec * jnp.dot(q_ref[rows, :], state.astype(BF16), preferred_element_type=F32)
        yn = y * lax.rsqrt(jnp.mean(y * y, axis=-1, keepdims=True) + EPS)
        o_ref[rows, :] = yn.astype(BF16) * gs_ref[rows, :]


def _retention(qkvg, batch, seq, heads_per_step=4):
    t = qkvg.shape[1]
    groups = RET_HEADS // heads_per_step
    spec = lambda sec: pl.BlockSpec((heads_per_step, seq, RET_HEAD_DIM),
                                    lambda b, g: (sec * groups + g, b, 0))
    return pl.pallas_call(
        _ret_kernel,
        grid=(batch, groups),
        in_specs=[spec(0), spec(1), spec(2), spec(3)],
        out_specs=pl.BlockSpec((seq, heads_per_step * RET_HEAD_DIM), lambda b, g: (b, g)),
        out_shape=jax.ShapeDtypeStruct((t, D_RET), BF16),
        compiler_params=_params("arbitrary", "arbitrary"),
        name="ret",
    )(qkvg, qkvg, qkvg, qkvg)


def _merge_kernel(yr_ref, yc_ref, ga_ref, gb_ref, wr_ref, wc_ref, side_ref, o_ref, side_out_ref):
    _side_cast([side_ref], [side_out_ref])
    ya = jnp.dot(yr_ref[...], wr_ref[...], preferred_element_type=F32)
    yb = jnp.dot(yc_ref[...], wc_ref[...], preferred_element_type=F32)
    o_ref[...] = (ga_ref[...].astype(F32) * ya + gb_ref[...].astype(F32) * yb).astype(BF16)


def _merge(yr, yc, gates, w_ret_o_bf, w_conv_o_bf, side_weight, tm=512):
    t, d = yr.shape
    resident = lambda: pl.BlockSpec((d, d), lambda i: (0, 0), pipeline_mode=pl.Buffered(1))
    side_specs, side_shapes = _side_cast_specs([side_weight], t // tm, lambda i: i)
    return pl.pallas_call(
        _merge_kernel,
        grid=(t // tm,),
        in_specs=[pl.BlockSpec((tm, d), lambda i: (i, 0)),
                  pl.BlockSpec((tm, d), lambda i: (i, 0)),
                  pl.BlockSpec((tm, d), lambda i: (i, 0)),
                  pl.BlockSpec((tm, d), lambda i: (i, 1)),
                  resident(), resident()] + side_specs,
        out_specs=[pl.BlockSpec((tm, d), lambda i: (i, 0))] + side_specs,
        out_shape=[jax.ShapeDtypeStruct((t, d), BF16)] + side_shapes,
        compiler_params=_params("arbitrary"),
        name="merge",
    )(yr, yc, gates, gates, w_ret_o_bf, w_conv_o_bf, side_weight)


def _mixout_kernel(m_ref, w_ref, x_ref, g1_ref, g_ref, sc_ref, sh_ref, h_ref, xn_ref, *, n_sub):
    tr = m_ref.shape[0] // n_sub
    for s in range(n_sub):
        rows = pl.ds(s * tr, tr)
        mixed = jnp.dot(m_ref[rows, :], w_ref[...], preferred_element_type=F32)
        h = x_ref[rows, :] + g1_ref[...] * mixed
        h_ref[rows, :] = h
        xn_ref[rows, :] = _mod_rmsnorm(h_ref, rows, g_ref[...], sc_ref[...], sh_ref[...]).astype(BF16)


def _mixout(m, w_mix_o_bf, x2d, mod3, norm_g, seq, tm=512, n_sub=2):
    t, d = x2d.shape
    tps = seq // tm
    mod_spec = lambda k: pl.BlockSpec((None, 1, d), lambda i: (i // tps, 0, k))
    return pl.pallas_call(
        functools.partial(_mixout_kernel, n_sub=n_sub),
        grid=(t // tm,),
        in_specs=[pl.BlockSpec((tm, d), lambda i: (i, 0)),
                  pl.BlockSpec((d, d), lambda i: (0, 0), pipeline_mode=pl.Buffered(1)),
                  pl.BlockSpec((tm, d), lambda i: (i, 0)),
                  mod_spec(2), pl.BlockSpec((1, d), lambda i: (0, 0)), mod_spec(4), mod_spec(3)],
        out_specs=[pl.BlockSpec((tm, d), lambda i: (i, 0)), pl.BlockSpec((tm, d), lambda i: (i, 0))],
        out_shape=[jax.ShapeDtypeStruct((t, d), F32), jax.ShapeDtypeStruct((t, d), BF16)],
        compiler_params=_params("arbitrary"),
        name="mixout",
    )(m, w_mix_o_bf, x2d, mod3, norm_g.reshape(1, d), mod3, mod3)


def _ffn_kernel(xn_ref, wu_ref, wg_ref, wfc_ref, bfc_ref, wd_ref, h_hbm, g2_ref, fg_ref,
                o_ref, halo_ref, h_sem, *, tiles_per_seq, sub_width, last_width):
    i = pl.program_id(0)
    f = pl.program_id(1)
    last = pl.num_programs(1) - 1
    tm = o_ref.shape[0]

    @pl.when(jnp.logical_and(i == 0, f == 0))
    def _():
        halo_ref[...] = jnp.zeros_like(halo_ref)

    def h1_copy():
        return pltpu.make_async_copy(h_hbm.at[pl.ds(i * tm, tm), :], o_ref, h_sem)

    def step(first, width):
        if first:
            h1_copy().start()
        xn = xn_ref[...]
        not_seq_start = i % tiles_per_seq != 0
        col_blocks = [pl.ds(s * sub_width, sub_width) for s in range(width // sub_width)]
        up_gate = [(jnp.dot(xn, wu_ref[:, cols], preferred_element_type=F32),
                    jnp.dot(xn, wg_ref[:, cols], preferred_element_type=F32)) for cols in col_blocks]
        acts = []
        for cols, (up, gpre) in zip(col_blocks, up_gate):
            prev = jnp.where(not_seq_start, halo_ref[f, :, cols], 0.0)
            gt = _causal_conv3(gpre, prev, wfc_ref[:, cols]) + bfc_ref[:, cols]
            halo_ref[f, :, cols] = gpre[gpre.shape[0] - HALO_ROWS:, :]
            acts.append((_silu(gt) * up).astype(BF16))
        n_head = len(acts) - 1
        head = None
        if n_head:
            head = jnp.dot(jnp.concatenate(acts[:-1], axis=1), wd_ref[pl.ds(0, n_head * sub_width), :],
                           preferred_element_type=F32)
        part = jnp.dot(acts[-1], wd_ref[col_blocks[-1], :], preferred_element_type=F32)
        if head is not None:
            part = head + part
        if first:
            h1_copy().wait()
        o_ref[...] += g2_ref[...] * part

    tile_width = wu_ref.shape[1]
    pl.when(f == 0)(functools.partial(step, True, tile_width))
    pl.when(jnp.logical_and(f > 0, f < last))(functools.partial(step, False, tile_width))
    pl.when(f == last)(functools.partial(step, False, last_width))

    @pl.when(f == last)
    def _():
        r = lax.rsqrt(jnp.mean(jnp.square(o_ref[...]), axis=-1, keepdims=True) + EPS)
        o_ref[...] = (o_ref[...] * r) * fg_ref[...]


def _ffn(xn2, w_up_bf, w_gate_bf, w_ffconv, b_ffconv, w_down_bf, h1, mod3, final_g, seq,
         tm=1024, tf=768, sub_width=256):
    t, d = xn2.shape
    dff = w_up_bf.shape[1]
    nf = pl.cdiv(dff, tf)
    tps = seq // tm
    assert nf >= 2 and tf % sub_width == 0 and dff % sub_width == 0
    kern = functools.partial(_ffn_kernel, tiles_per_seq=tps, sub_width=sub_width,
                             last_width=dff - (nf - 1) * tf)
    return pl.pallas_call(
        kern,
        grid=(t // tm, nf),
        in_specs=[pl.BlockSpec((tm, d), lambda i, f: (i, 0)),
                  pl.BlockSpec((d, tf), lambda i, f: (0, f)),
                  pl.BlockSpec((d, tf), lambda i, f: (0, f)),
                  pl.BlockSpec((CONV_WIDTH, tf), lambda i, f: (0, f)),
                  pl.BlockSpec((1, tf), lambda i, f: (0, f)),
                  pl.BlockSpec((tf, d), lambda i, f: (f, 0)),
                  pl.BlockSpec(memory_space=pl.ANY),
                  pl.BlockSpec((None, 1, d), lambda i, f: (i // tps, 0, 5)),
                  pl.BlockSpec((1, d), lambda i, f: (0, 0))],
        out_specs=pl.BlockSpec((tm, d), lambda i, f: (i, 0)),
        out_shape=jax.ShapeDtypeStruct((t, d), F32),
        scratch_shapes=[pltpu.VMEM((nf, HALO_ROWS, tf), F32), pltpu.SemaphoreType.DMA(())],
        compiler_params=_params("arbitrary", "arbitrary"),
        name="ffn",
    )(xn2, w_up_bf, w_gate_bf, w_ffconv, b_ffconv.reshape(1, dff), w_down_bf, h1, mod3,
      final_g.reshape(1, d))


def kernel(x, c, positions, norm1_g, norm2_g, w_ada, b_ada, w_in, b_gate, w_sc, w_ret_o, w_conv_o,
           w_mix_o, w_up, w_gate, w_ffconv, b_ffconv, w_down, final_g):
    batch, seq, d = x.shape
    depth = w_in.shape[0]
    assert depth == 1, "final-norm fusion assumes a single layer"
    assert d == D_RET and w_in.shape[2] == 4 * D_RET + 5 * d, (d, w_in.shape)
    assert seq % 2048 == 0 and seq % RET_CHUNK == 0 and batch % 2 == 0, (batch, seq)
    t = batch * seq
    h = x.reshape(t, d)
    for l in range(depth):
        mod, cos, sin = _ada_rope(c, w_ada[l], b_ada[l], positions)
        mod3 = mod.reshape(batch, 1, 6 * d)
        xn, cos, sin = _norm1(h, norm1_g[l], mod3, positions, cos, sin, seq)
        qkvg, w_ret_o_bf, w_conv_o_bf, w_mix_o_bf = _inproj_a(
            xn, w_in[l], cos.reshape(t, ROT_HALF), sin.reshape(t, ROT_HALF),
            [w_ret_o[l], w_conv_o[l], w_mix_o[l]])
        yc, w_gate_bf = _inproj_c(xn, w_in[l], w_sc[l], w_gate[l], seq)
        gates, w_up_bf = _inproj_g(xn, w_in[l], b_gate[l], w_up[l])
        yr = _retention(qkvg, batch, seq)
        merged, w_down_bf = _merge(yr, yc, gates, w_ret_o_bf, w_conv_o_bf, w_down[l])
        h1, xn2 = _mixout(merged, w_mix_o_bf, h, mod3, norm2_g[l], seq)
        h = _ffn(xn2, w_up_bf, w_gate_bf, w_ffconv[l], b_ffconv[l], w_down_bf, h1, mod3, final_g, seq)
    return h.reshape(batch, seq, d)
```

```python
import functools

import jax
import jax.numpy as jnp
from jax import lax
from jax.experimental import pallas as pl
from jax.experimental.pallas import tpu as pltpu

F32 = jnp.float32
BF16 = jnp.bfloat16

RET_HEADS = 8
RET_HEAD_DIM = 256
D_RET = RET_HEADS * RET_HEAD_DIM
ROT_HALF = RET_HEAD_DIM // 2
CONV_WIDTH = 3
ROPE_BASE = 10000.0
EPS = 1e-6
RET_CHUNK = 256
HALO_ROWS = 8

V7X_VMEM_LIMIT_BYTES = 56 * 1024 * 1024


def _params(*semantics):
    return pltpu.CompilerParams(dimension_semantics=semantics,
                                vmem_limit_bytes=V7X_VMEM_LIMIT_BYTES)


def _sigmoid_of_half(h):
    return 0.5 + 0.5 * jnp.tanh(h)


def _silu_of_half(h):
    return h + h * jnp.tanh(h)


def _silu(v):
    return _silu_of_half(0.5 * v)


def _mod_rmsnorm(src_ref, rows, gain, scale, shift):
    x = src_ref[rows, :]
    r = lax.rsqrt(jnp.mean(x * x, axis=-1, keepdims=True) + EPS)
    return (src_ref[rows, :] * r) * (gain * (1.0 + scale)) + shift


def _causal_conv3(u, prev, w):
    def taps(a):
        return w[0:1, :] * pltpu.roll(a, 2, 0) + w[1:2, :] * pltpu.roll(a, 1, 0) + w[2:3, :] * a

    head = taps(jnp.concatenate([prev, u[:HALO_ROWS, :]], axis=0))[HALO_ROWS:, :]
    return jnp.concatenate([head, taps(u)[HALO_ROWS:, :]], axis=0)


def _side_cast_specs(weights, steps, step_index):
    specs = [pl.BlockSpec((w.shape[0] // steps, w.shape[1]), lambda *g: (step_index(*g), 0)) for w in weights]
    shapes = [jax.ShapeDtypeStruct(w.shape, BF16) for w in weights]
    return specs, shapes


def _side_cast(in_refs, out_refs, scale=None):
    for src, dst in zip(in_refs, out_refs):
        w = src[...] if scale is None else src[...] * scale
        dst[...] = w.astype(BF16)


def _rope_angles(positions, tr):
    inv_freq = ROPE_BASE ** (-jnp.arange(ROT_HALF, dtype=F32) / ROT_HALF)
    return positions.reshape(positions.size // tr, 1, tr), inv_freq.reshape(ROT_HALF, 1)


def _rope_tables(pos_ref, inv_ref):
    ang_t = inv_ref[...] * pos_ref[...].astype(F32)
    return jnp.cos(ang_t).T, jnp.sin(ang_t).T


def _ada_rope_kernel(c_ref, w_ref, b_ref, pos_ref, inv_ref, mod_ref, cos_ref, sin_ref):
    @pl.when(pl.program_id(0) == 0)
    def _():
        mod_ref[...] = jnp.broadcast_to(b_ref[...], mod_ref.shape)

    ca = _silu(c_ref[...]).astype(BF16)
    mod_ref[...] += jnp.dot(ca, w_ref[...].astype(BF16), preferred_element_type=F32)
    cos, sin = _rope_tables(pos_ref, inv_ref)
    cos_ref[0] = cos
    sin_ref[0] = sin
    cos_ref[1] = jnp.zeros_like(cos)
    sin_ref[1] = jnp.zeros_like(sin)


def _ada_rope(c, w_ada, b_ada, positions, steps=8):
    b, d = c.shape
    n = w_ada.shape[1]
    half = positions.size // 2
    tk, tr = d // steps, half // steps
    pos, inv_freq = _rope_angles(positions, tr)
    rope_spec = pl.BlockSpec((2, tr, ROT_HALF), lambda j: (0, j, 0))
    return pl.pallas_call(
        _ada_rope_kernel,
        grid=(steps,),
        in_specs=[pl.BlockSpec((b, tk), lambda j: (0, j)),
                  pl.BlockSpec((tk, n), lambda j: (j, 0)),
                  pl.BlockSpec((1, n), lambda j: (0, 0)),
                  pl.BlockSpec((None, 1, tr), lambda j: (j, 0, 0)),
                  pl.BlockSpec((ROT_HALF, 1), lambda j: (0, 0))],
        out_specs=[pl.BlockSpec((b, n), lambda j: (0, 0)), rope_spec, rope_spec],
        out_shape=[jax.ShapeDtypeStruct((b, n), F32),
                   jax.ShapeDtypeStruct((2, half, ROT_HALF), F32),
                   jax.ShapeDtypeStruct((2, half, ROT_HALF), F32)],
        compiler_params=_params("arbitrary"),
        name="ada_rope",
    )(c, w_ada, b_ada.reshape(1, n), pos, inv_freq)


def _norm1_kernel(x_ref, g_ref, sc_ref, sh_ref, pos_ref, inv_ref, cos_in, sin_in, o_ref, cos_ref, sin_ref):
    del cos_in, sin_in
    o_ref[...] = _mod_rmsnorm(x_ref, slice(None), g_ref[...], sc_ref[...], sh_ref[...]).astype(BF16)
    cos_ref[...], sin_ref[...] = _rope_tables(pos_ref, inv_ref)


def _norm1(x2d, norm_g, mod3, positions, cos, sin, seq, tm=1024):
    t, d = x2d.shape
    tps = seq // tm
    steps = t // tm
    tr = cos.shape[1] // steps
    pos, inv_freq = _rope_angles(positions, tr)
    rope_spec = pl.BlockSpec((None, tr, ROT_HALF), lambda i: (1, i, 0))
    return pl.pallas_call(
        _norm1_kernel,
        grid=(steps,),
        in_specs=[pl.BlockSpec((tm, d), lambda i: (i, 0)),
                  pl.BlockSpec((1, d), lambda i: (0, 0)),
                  pl.BlockSpec((None, 1, d), lambda i: (i // tps, 0, 1)),
                  pl.BlockSpec((None, 1, d), lambda i: (i // tps, 0, 0)),
                  pl.BlockSpec((None, 1, tr), lambda i: (steps + i, 0, 0)),
                  pl.BlockSpec((ROT_HALF, 1), lambda i: (0, 0)),
                  pl.BlockSpec(memory_space=pl.ANY),
                  pl.BlockSpec(memory_space=pl.ANY)],
        out_specs=[pl.BlockSpec((tm, d), lambda i: (i, 0)), rope_spec, rope_spec],
        out_shape=[jax.ShapeDtypeStruct((t, d), BF16),
                   jax.ShapeDtypeStruct(cos.shape, F32), jax.ShapeDtypeStruct(sin.shape, F32)],
        input_output_aliases={6: 1, 7: 2},
        compiler_params=_params("arbitrary"),
        name="norm1",
    )(x2d, norm_g.reshape(1, d), mod3, mod3, pos, inv_freq, cos, sin)


def _inproj_a_kernel(xn_ref, w_ref, cos_ref, sin_ref, *rest, q_tiles, rot_tiles, gate_tile0, n_sub, n_side):
    side_in, (o_ref, *side_out, wbf_ref) = rest[:n_side], rest[n_side:]
    j = pl.program_id(0)

    @pl.when(pl.program_id(1) == 0)
    def _():
        scale = jnp.where(j >= gate_tile0, 0.5, 1.0).astype(F32)
        wbf_ref[...] = (w_ref[...] * scale).astype(BF16)

    _side_cast(side_in, side_out)
    heads = o_ref.shape[0]
    tr = xn_ref.shape[0] // n_sub

    def proj(rows):
        return jnp.dot(xn_ref[rows, :], wbf_ref[...], preferred_element_type=F32)

    @pl.when(j < rot_tiles)
    def _():
        scale = jnp.where(j < q_tiles, RET_HEAD_DIM ** -0.5, 1.0).astype(F32)
        for s in range(n_sub):
            rows = pl.ds(s * tr, tr)
            acc = proj(rows)
            cos = cos_ref[rows, :] * scale
            sin = sin_ref[rows, :] * scale
            for hh in range(heads):
                lo = hh * RET_HEAD_DIM
                x1 = acc[:, lo:lo + ROT_HALF]
                x2 = acc[:, lo + ROT_HALF:lo + RET_HEAD_DIM]
                o_ref[hh, rows, 0:ROT_HALF] = (x1 * cos - x2 * sin).astype(BF16)
                o_ref[hh, rows, ROT_HALF:RET_HEAD_DIM] = (x2 * cos + x1 * sin).astype(BF16)

    def plain(fn):
        for s in range(n_sub):
            rows = pl.ds(s * tr, tr)
            res = fn(proj(rows)).astype(BF16)
            for hh in range(heads):
                o_ref[hh, rows, :] = res[:, hh * RET_HEAD_DIM:(hh + 1) * RET_HEAD_DIM]

    pl.when(jnp.logical_and(j >= rot_tiles, j < gate_tile0))(functools.partial(plain, lambda a: a))
    pl.when(j >= gate_tile0)(functools.partial(plain, _silu_of_half))


def _inproj_a(xn, w_in, cos, sin, side_weights, tm=2048, tn=1024, n_sub=4):
    t, d = xn.shape
    n = 4 * D_RET
    heads = tn // RET_HEAD_DIM
    n_i = t // tm
    side_specs, side_shapes = _side_cast_specs(side_weights, (n // tn) * n_i, lambda j, i: j * n_i + i)
    kern = functools.partial(_inproj_a_kernel, q_tiles=D_RET // tn, rot_tiles=2 * D_RET // tn,
                             gate_tile0=3 * D_RET // tn, n_sub=n_sub, n_side=len(side_weights))
    return pl.pallas_call(
        kern,
        grid=(n // tn, n_i),
        in_specs=[pl.BlockSpec((tm, d), lambda j, i: (i, 0)),
                  pl.BlockSpec((d, tn), lambda j, i: (0, j)),
                  pl.BlockSpec((tm, ROT_HALF), lambda j, i: (i, 0)),
                  pl.BlockSpec((tm, ROT_HALF), lambda j, i: (i, 0))] + side_specs,
        out_specs=[pl.BlockSpec((heads, tm, RET_HEAD_DIM), lambda j, i: (j, i, 0))] + side_specs,
        out_shape=[jax.ShapeDtypeStruct((n // RET_HEAD_DIM, t, RET_HEAD_DIM), BF16)] + side_shapes,
        scratch_shapes=[pltpu.VMEM((d, tn), BF16)],
        compiler_params=_params("arbitrary", "arbitrary"),
        name="inproj_a",
    )(xn, w_in, cos, sin, *side_weights)


def _inproj_c_kernel(xn_ref, wb_ref, wc_ref, wx_ref, wsc_ref, side_ref, o_ref, side_out_ref,
                     wb_bf, wc_bf, wx_bf, halo_ref, *, tiles_per_seq, side_scale):
    j = pl.program_id(0)
    i = pl.program_id(1)

    @pl.when(jnp.logical_and(i == 0, j == 0))
    def _():
        halo_ref[...] = jnp.zeros_like(halo_ref)

    @pl.when(i == 0)
    def _():
        wb_bf[...] = wb_ref[...].astype(BF16)
        wc_bf[...] = wc_ref[...].astype(BF16)
        wx_bf[...] = wx_ref[...].astype(BF16)

    _side_cast([side_ref], [side_out_ref], scale=side_scale)
    xn = xn_ref[...]
    cg = jnp.dot(xn, wc_bf[...], preferred_element_type=F32)
    xc = jnp.dot(xn, wx_bf[...], preferred_element_type=F32)
    u = cg * xc
    prev = jnp.where(i % tiles_per_seq == 0, 0.0, halo_ref[...])
    conv = _causal_conv3(u, prev, wsc_ref[...])
    halo_ref[...] = u[u.shape[0] - HALO_ROWS:, :]
    bg = jnp.dot(xn, wb_bf[...], preferred_element_type=F32)
    o_ref[...] = (bg * conv).astype(BF16)


def _inproj_c(xn, w_in, w_sc, side_weight, side_scale, seq, tm=1024, tc=512):
    t, d = xn.shape
    nc = d // tc
    n_i = t // tm
    first = 4 * D_RET // tc
    kern = functools.partial(_inproj_c_kernel, tiles_per_seq=seq // tm, side_scale=side_scale)
    wspec = lambda sec: pl.BlockSpec((d, tc), lambda j, i: (0, first + sec * nc + j))
    side_specs, side_shapes = _side_cast_specs([side_weight], nc * n_i, lambda j, i: j * n_i + i)
    return pl.pallas_call(
        kern,
        grid=(nc, n_i),
        in_specs=[pl.BlockSpec((tm, d), lambda j, i: (i, 0)),
                  wspec(0), wspec(1), wspec(2),
                  pl.BlockSpec((CONV_WIDTH, tc), lambda j, i: (0, j))] + side_specs,
        out_specs=[pl.BlockSpec((tm, tc), lambda j, i: (i, j))] + side_specs,
        out_shape=[jax.ShapeDtypeStruct((t, d), BF16)] + side_shapes,
        scratch_shapes=[pltpu.VMEM((d, tc), BF16)] * 3 + [pltpu.VMEM((HALO_ROWS, tc), F32)],
        compiler_params=_params("arbitrary", "arbitrary"),
        name="inproj_c",
    )(xn, w_in, w_in, w_in, w_sc, side_weight)


def _inproj_g_kernel(xn_ref, w_ref, b_ref, side_ref, o_ref, side_out_ref, wbf_ref, *, n_sub):
    @pl.when(pl.program_id(1) == 0)
    def _():
        wbf_ref[...] = (w_ref[...] * 0.5).astype(BF16)

    _side_cast([side_ref], [side_out_ref])
    tr = xn_ref.shape[0] // n_sub
    half_bias = 0.5 * b_ref[...]
    for s in range(n_sub):
        rows = pl.ds(s * tr, tr)
        half_logits = jnp.dot(xn_ref[rows, :], wbf_ref[...], preferred_element_type=F32) + half_bias
        o_ref[rows, :] = _sigmoid_of_half(half_logits).astype(BF16)


def _inproj_g(xn, w_in, b_gate, side_weight, tm=2048, tn=1024, n_sub=4):
    t, d = xn.shape
    n = b_gate.shape[0]
    n_i = t // tm
    first = (w_in.shape[1] - n) // tn
    side_specs, side_shapes = _side_cast_specs([side_weight], (n // tn) * n_i, lambda j, i: j * n_i + i)
    return pl.pallas_call(
        functools.partial(_inproj_g_kernel, n_sub=n_sub),
        grid=(n // tn, n_i),
        in_specs=[pl.BlockSpec((tm, d), lambda j, i: (i, 0)),
                  pl.BlockSpec((d, tn), lambda j, i: (0, first + j)),
                  pl.BlockSpec((1, tn), lambda j, i: (0, j))] + side_specs,
        out_specs=[pl.BlockSpec((tm, tn), lambda j, i: (i, j))] + side_specs,
        out_shape=[jax.ShapeDtypeStruct((t, n), BF16)] + side_shapes,
        scratch_shapes=[pltpu.VMEM((d, tn), BF16)],
        compiler_params=_params("arbitrary", "arbitrary"),
        name="inproj_g",
    )(xn, w_in, b_gate.reshape(1, n), side_weight)


def _ret_kernel(q_ref, k_ref, v_ref, gs_ref, o_ref):
    heads = q_ref.shape[0]
    first = pl.program_id(1) * heads

    def prepare(hh):
        return _ret_scores_and_states(first + hh, q_ref.at[hh], k_ref.at[hh], v_ref.at[hh])

    prep = prepare(0)
    for hh in range(heads):
        next_prep = prepare(hh + 1) if hh + 1 < heads else None
        cols = pl.ds(hh * RET_HEAD_DIM, RET_HEAD_DIM)
        _ret_outputs(prep, q_ref.at[hh], v_ref.at[hh], gs_ref.at[hh], o_ref.at[:, cols])
        prep = next_prep


def _ret_scores_and_states(head_index, q_ref, k_ref, v_ref):
    c = RET_CHUNK
    n_chunks = q_ref.shape[0] // c
    head = jnp.full((1, RET_HEAD_DIM), head_index, jnp.int32).astype(F32)
    log_gamma = jnp.log(1.0 - jnp.exp2(-5.0 - head))
    row = lax.broadcasted_iota(jnp.int32, (c, c), 0).astype(F32)
    col = lax.broadcasted_iota(jnp.int32, (c, c), 1).astype(F32)
    diff = row - col
    decay = jnp.where(diff >= 0, jnp.exp(log_gamma * jnp.maximum(diff, 0.0)), 0.0)
    q_dec = jnp.exp(log_gamma * (row + 1.0))
    k_dec = jnp.exp(log_gamma * (c - 1.0 - row))
    chunk_dec = jnp.exp(log_gamma * float(c))
    chunks = [pl.ds(n * c, c) for n in range(n_chunks)]

    kvs = []
    for rows in chunks[:-1]:
        kd = (k_ref[rows, :].astype(F32) * k_dec).astype(BF16)
        kvs.append(lax.dot_general(kd, v_ref[rows, :], (((0,), (0,)), ((), ())),
                                   preferred_element_type=F32))
    scores = [lax.dot_general(q_ref[rows, :], k_ref[rows, :], (((1,), (1,)), ((), ())),
                              preferred_element_type=F32) for rows in chunks]
    states = [None]
    for kv in kvs:
        states.append(kv if states[-1] is None else chunk_dec * states[-1] + kv)
    return chunks, scores, states, decay, q_dec


def _ret_outputs(prep, q_ref, v_ref, gs_ref, o_ref):
    chunks, scores, states, decay, q_dec = prep
    for rows, s, state in zip(chunks, scores, states):
        y = jnp.dot((s * decay).astype(BF16), v_ref[rows, :], preferred_element_type=F32)
        if state is not None:
            y = y + q_dec * jnp.dot(q_ref[rows, :], state.astype(BF16), preferred_element_type=F32)
        yn = y * lax.rsqrt(jnp.mean(y * y, axis=-1, keepdims=True) + EPS)
        o_ref[rows, :] = yn.astype(BF16) * gs_ref[rows, :]


def _retention(qkvg, batch, seq, heads_per_step=4):
    t = qkvg.shape[1]
    groups = RET_HEADS // heads_per_step
    spec = lambda sec: pl.BlockSpec((heads_per_step, seq, RET_HEAD_DIM),
                                    lambda b, g: (sec * groups + g, b, 0))
    return pl.pallas_call(
        _ret_kernel,
        grid=(batch, groups),
        in_specs=[spec(0), spec(1), spec(2), spec(3)],
        out_specs=pl.BlockSpec((seq, heads_per_step * RET_HEAD_DIM), lambda b, g: (b, g)),
        out_shape=jax.ShapeDtypeStruct((t, D_RET), BF16),
        compiler_params=_params("arbitrary", "arbitrary"),
        name="ret",
    )(qkvg, qkvg, qkvg, qkvg)


def _merge_kernel(yr_ref, yc_ref, ga_ref, gb_ref, wr_ref, wc_ref, side_ref, o_ref, side_out_ref):
    _side_cast([side_ref], [side_out_ref])
    ya = jnp.dot(yr_ref[...], wr_ref[...], preferred_element_type=F32)
    yb = jnp.dot(yc_ref[...], wc_ref[...], preferred_element_type=F32)
    o_ref[...] = (ga_ref[...].astype(F32) * ya + gb_ref[...].astype(F32) * yb).astype(BF16)


def _merge(yr, yc, gates, w_ret_o_bf, w_conv_o_bf, side_weight, tm=512):
    t, d = yr.shape
    resident = lambda: pl.BlockSpec((d, d), lambda i: (0, 0), pipeline_mode=pl.Buffered(1))
    side_specs, side_shapes = _side_cast_specs([side_weight], t // tm, lambda i: i)
    return pl.pallas_call(
        _merge_kernel,
        grid=(t // tm,),
        in_specs=[pl.BlockSpec((tm, d), lambda i: (i, 0)),
                  pl.BlockSpec((tm, d), lambda i: (i, 0)),
                  pl.BlockSpec((tm, d), lambda i: (i, 0)),
                  pl.BlockSpec((tm, d), lambda i: (i, 1)),
                  resident(), resident()] + side_specs,
        out_specs=[pl.BlockSpec((tm, d), lambda i: (i, 0))] + side_specs,
        out_shape=[jax.ShapeDtypeStruct((t, d), BF16)] + side_shapes,
        compiler_params=_params("arbitrary"),
        name="merge",
    )(yr, yc, gates, gates, w_ret_o_bf, w_conv_o_bf, side_weight)


def _mixout_kernel(m_ref, w_ref, x_ref, g1_ref, g_ref, sc_ref, sh_ref, h_ref, xn_ref, *, n_sub):
    tr = m_ref.shape[0] // n_sub
    for s in range(n_sub):
        rows = pl.ds(s * tr, tr)
        mixed = jnp.dot(m_ref[rows, :], w_ref[...], preferred_element_type=F32)
        h = x_ref[rows, :] + g1_ref[...] * mixed
        h_ref[rows, :] = h
        xn_ref[rows, :] = _mod_rmsnorm(h_ref, rows, g_ref[...], sc_ref[...], sh_ref[...]).astype(BF16)


def _mixout(m, w_mix_o_bf, x2d, mod3, norm_g, seq, tm=512, n_sub=2):
    t, d = x2d.shape
    tps = seq // tm
    mod_spec = lambda k: pl.BlockSpec((None, 1, d), lambda i: (i // tps, 0, k))
    return pl.pallas_call(
        functools.partial(_mixout_kernel, n_sub=n_sub),
        grid=(t // tm,),
        in_specs=[pl.BlockSpec((tm, d), lambda i: (i, 0)),
                  pl.BlockSpec((d, d), lambda i: (0, 0), pipeline_mode=pl.Buffered(1)),
                  pl.BlockSpec((tm, d), lambda i: (i, 0)),
                  mod_spec(2), pl.BlockSpec((1, d), lambda i: (0, 0)), mod_spec(4), mod_spec(3)],
        out_specs=[pl.BlockSpec((tm, d), lambda i: (i, 0)), pl.BlockSpec((tm, d), lambda i: (i, 0))],
        out_shape=[jax.ShapeDtypeStruct((t, d), F32), jax.ShapeDtypeStruct((t, d), BF16)],
        compiler_params=_params("arbitrary"),
        name="mixout",
    )(m, w_mix_o_bf, x2d, mod3, norm_g.reshape(1, d), mod3, mod3)


def _ffn_kernel(xn_ref, wu_ref, wg_ref, wfc_ref, bfc_ref, wd_ref, h_hbm, g2_ref, fg_ref,
                o_ref, halo_ref, h_sem, *, tiles_per_seq, sub_width, last_width):
    i = pl.program_id(0)
    f = pl.program_id(1)
    last = pl.num_programs(1) - 1
    tm = o_ref.shape[0]

    @pl.when(jnp.logical_and(i == 0, f == 0))
    def _():
        halo_ref[...] = jnp.zeros_like(halo_ref)

    def h1_copy():
        return pltpu.make_async_copy(h_hbm.at[pl.ds(i * tm, tm), :], o_ref, h_sem)

    def step(first, width):
        if first:
            h1_copy().start()
        xn = xn_ref[...]
        not_seq_start = i % tiles_per_seq != 0
        col_blocks = [pl.ds(s * sub_width, sub_width) for s in range(width // sub_width)]
        up_gate = [(jnp.dot(xn, wu_ref[:, cols], preferred_element_type=F32),
                    jnp.dot(xn, wg_ref[:, cols], preferred_element_type=F32)) for cols in col_blocks]
        acts = []
        for cols, (up, gpre) in zip(col_blocks, up_gate):
            prev = jnp.where(not_seq_start, halo_ref[f, :, cols], 0.0)
            half_gt = _causal_conv3(gpre, prev, wfc_ref[:, cols]) + 0.5 * bfc_ref[:, cols]
            halo_ref[f, :, cols] = gpre[gpre.shape[0] - HALO_ROWS:, :]
            acts.append((_silu_of_half(half_gt) * up).astype(BF16))
        n_head = len(acts) - 1
        head = None
        if n_head:
            head = jnp.dot(jnp.concatenate(acts[:-1], axis=1), wd_ref[pl.ds(0, n_head * sub_width), :],
                           preferred_element_type=F32)
        part = jnp.dot(acts[-1], wd_ref[col_blocks[-1], :], preferred_element_type=F32)
        if head is not None:
            part = head + part
        if first:
            h1_copy().wait()
        o_ref[...] += g2_ref[...] * part

    tile_width = wu_ref.shape[1]
    pl.when(f == 0)(functools.partial(step, True, tile_width))
    pl.when(jnp.logical_and(f > 0, f < last))(functools.partial(step, False, tile_width))
    pl.when(f == last)(functools.partial(step, False, last_width))

    @pl.when(f == last)
    def _():
        r = lax.rsqrt(jnp.mean(jnp.square(o_ref[...]), axis=-1, keepdims=True) + EPS)
        o_ref[...] = (o_ref[...] * r) * fg_ref[...]


def _ffn(xn2, w_up_bf, w_gate_half_bf, w_ffconv, b_ffconv, w_down_bf, h1, mod3, final_g, seq,
         tm=1024, tf=768, sub_width=256):
    t, d = xn2.shape
    dff = w_up_bf.shape[1]
    nf = pl.cdiv(dff, tf)
    tps = seq // tm
    assert nf >= 2 and tf % sub_width == 0 and dff % sub_width == 0
    kern = functools.partial(_ffn_kernel, tiles_per_seq=tps, sub_width=sub_width,
                             last_width=dff - (nf - 1) * tf)
    return pl.pallas_call(
        kern,
        grid=(t // tm, nf),
        in_specs=[pl.BlockSpec((tm, d), lambda i, f: (i, 0)),
                  pl.BlockSpec((d, tf), lambda i, f: (0, f)),
                  pl.BlockSpec((d, tf), lambda i, f: (0, f)),
                  pl.BlockSpec((CONV_WIDTH, tf), lambda i, f: (0, f)),
                  pl.BlockSpec((1, tf), lambda i, f: (0, f)),
                  pl.BlockSpec((tf, d), lambda i, f: (f, 0)),
                  pl.BlockSpec(memory_space=pl.ANY),
                  pl.BlockSpec((None, 1, d), lambda i, f: (i // tps, 0, 5)),
                  pl.BlockSpec((1, d), lambda i, f: (0, 0))],
        out_specs=pl.BlockSpec((tm, d), lambda i, f: (i, 0)),
        out_shape=jax.ShapeDtypeStruct((t, d), F32),
        scratch_shapes=[pltpu.VMEM((nf, HALO_ROWS, tf), F32), pltpu.SemaphoreType.DMA(())],
        compiler_params=_params("arbitrary", "arbitrary"),
        name="ffn",
    )(xn2, w_up_bf, w_gate_half_bf, w_ffconv, b_ffconv.reshape(1, dff), w_down_bf, h1, mod3,
      final_g.reshape(1, d))


def kernel(x, c, positions, norm1_g, norm2_g, w_ada, b_ada, w_in, b_gate, w_sc, w_ret_o, w_conv_o,
           w_mix_o, w_up, w_gate, w_ffconv, b_ffconv, w_down, final_g):
    batch, seq, d = x.shape
    depth = w_in.shape[0]
    assert depth == 1, "final-norm fusion assumes a single layer"
    assert d == D_RET and w_in.shape[2] == 4 * D_RET + 5 * d, (d, w_in.shape)
    assert seq % 2048 == 0 and seq % RET_CHUNK == 0 and batch % 2 == 0, (batch, seq)
    t = batch * seq
    h = x.reshape(t, d)
    for l in range(depth):
        mod, cos, sin = _ada_rope(c, w_ada[l], b_ada[l], positions)
        mod3 = mod.reshape(batch, 1, 6 * d)
        xn, cos, sin = _norm1(h, norm1_g[l], mod3, positions, cos, sin, seq)
        qkvg, w_ret_o_bf, w_conv_o_bf, w_mix_o_bf = _inproj_a(
            xn, w_in[l], cos.reshape(t, ROT_HALF), sin.reshape(t, ROT_HALF),
            [w_ret_o[l], w_conv_o[l], w_mix_o[l]])
        yc, w_gate_half_bf = _inproj_c(xn, w_in[l], w_sc[l], w_gate[l], 0.5, seq)
        gates, w_up_bf = _inproj_g(xn, w_in[l], b_gate[l], w_up[l])
        yr = _retention(qkvg, batch, seq)
        merged, w_down_bf = _merge(yr, yc, gates, w_ret_o_bf, w_conv_o_bf, w_down[l])
        h1, xn2 = _mixout(merged, w_mix_o_bf, h, mod3, norm2_g[l], seq)
        h = _ffn(xn2, w_up_bf, w_gate_half_bf, w_ffconv[l], b_ffconv[l], w_down_bf, h1, mod3, final_g, seq)
    return h.reshape(batch, seq, d)
```

```python
import functools

import jax
import jax.numpy as jnp
from jax import lax
from jax.experimental import pallas as pl
from jax.experimental.pallas import tpu as pltpu

F32 = jnp.float32
BF16 = jnp.bfloat16

RET_HEADS = 8
RET_HEAD_DIM = 256
D_RET = RET_HEADS * RET_HEAD_DIM
ROT_HALF = RET_HEAD_DIM // 2
CONV_WIDTH = 3
ROPE_BASE = 10000.0
EPS = 1e-6
RET_CHUNK = 256
HALO_ROWS = 8

V7X_VMEM_LIMIT_BYTES = 56 * 1024 * 1024


def _params(*semantics):
    return pltpu.CompilerParams(dimension_semantics=semantics,
                                vmem_limit_bytes=V7X_VMEM_LIMIT_BYTES)


def _twice_sigmoid_of_half(h):
    return 1.0 + jnp.tanh(h)


def _silu_of_half(h):
    return h + h * jnp.tanh(h)


def _silu(v):
    return _silu_of_half(0.5 * v)


def _mod_rmsnorm(src_ref, rows, gain, scale, shift):
    x = src_ref[rows, :]
    r = lax.rsqrt(jnp.mean(x * x, axis=-1, keepdims=True) + EPS)
    return (src_ref[rows, :] * r) * (gain * (1.0 + scale)) + shift


def _causal_conv3(u, prev, w):
    def taps(a):
        return w[0:1, :] * pltpu.roll(a, 2, 0) + w[1:2, :] * pltpu.roll(a, 1, 0) + w[2:3, :] * a

    head = taps(jnp.concatenate([prev, u[:HALO_ROWS, :]], axis=0))[HALO_ROWS:, :]
    return jnp.concatenate([head, taps(u)[HALO_ROWS:, :]], axis=0)


def _side_cast_specs(weights, steps, step_index):
    specs = [pl.BlockSpec((w.shape[0] // steps, w.shape[1]), lambda *g: (step_index(*g), 0)) for w in weights]
    shapes = [jax.ShapeDtypeStruct(w.shape, BF16) for w in weights]
    return specs, shapes


def _side_cast(in_refs, out_refs, scales=None):
    for src, dst, scale in zip(in_refs, out_refs, scales or [None] * len(in_refs)):
        w = src[...] if scale is None else src[...] * scale
        dst[...] = w.astype(BF16)


def _rope_angles(positions, tr):
    inv_freq = ROPE_BASE ** (-jnp.arange(ROT_HALF, dtype=F32) / ROT_HALF)
    return positions.reshape(positions.size // tr, 1, tr), inv_freq.reshape(ROT_HALF, 1)


def _rope_tables(pos_ref, inv_ref):
    ang_t = inv_ref[...] * pos_ref[...].astype(F32)
    return jnp.cos(ang_t).T, jnp.sin(ang_t).T


def _ada_rope_kernel(c_ref, w_ref, b_ref, pos_ref, inv_ref, mod_ref, cos_ref, sin_ref):
    @pl.when(pl.program_id(0) == 0)
    def _():
        mod_ref[...] = jnp.broadcast_to(b_ref[...], mod_ref.shape)

    ca = _silu(c_ref[...]).astype(BF16)
    mod_ref[...] += jnp.dot(ca, w_ref[...].astype(BF16), preferred_element_type=F32)
    cos, sin = _rope_tables(pos_ref, inv_ref)
    cos_ref[0] = cos
    sin_ref[0] = sin
    cos_ref[1] = jnp.zeros_like(cos)
    sin_ref[1] = jnp.zeros_like(sin)


def _ada_rope(c, w_ada, b_ada, positions, steps=8):
    b, d = c.shape
    n = w_ada.shape[1]
    half = positions.size // 2
    tk, tr = d // steps, half // steps
    pos, inv_freq = _rope_angles(positions, tr)
    rope_spec = pl.BlockSpec((2, tr, ROT_HALF), lambda j: (0, j, 0))
    return pl.pallas_call(
        _ada_rope_kernel,
        grid=(steps,),
        in_specs=[pl.BlockSpec((b, tk), lambda j: (0, j)),
                  pl.BlockSpec((tk, n), lambda j: (j, 0)),
                  pl.BlockSpec((1, n), lambda j: (0, 0)),
                  pl.BlockSpec((None, 1, tr), lambda j: (j, 0, 0)),
                  pl.BlockSpec((ROT_HALF, 1), lambda j: (0, 0))],
        out_specs=[pl.BlockSpec((b, n), lambda j: (0, 0)), rope_spec, rope_spec],
        out_shape=[jax.ShapeDtypeStruct((b, n), F32),
                   jax.ShapeDtypeStruct((2, half, ROT_HALF), F32),
                   jax.ShapeDtypeStruct((2, half, ROT_HALF), F32)],
        compiler_params=_params("arbitrary"),
        name="ada_rope",
    )(c, w_ada, b_ada.reshape(1, n), pos, inv_freq)


def _norm1_kernel(x_ref, g_ref, sc_ref, sh_ref, pos_ref, inv_ref, cos_in, sin_in, o_ref, cos_ref, sin_ref):
    del cos_in, sin_in
    o_ref[...] = _mod_rmsnorm(x_ref, slice(None), g_ref[...], sc_ref[...], sh_ref[...]).astype(BF16)
    cos_ref[...], sin_ref[...] = _rope_tables(pos_ref, inv_ref)


def _norm1(x2d, norm_g, mod3, positions, cos, sin, seq, tm=1024):
    t, d = x2d.shape
    tps = seq // tm
    steps = t // tm
    tr = cos.shape[1] // steps
    pos, inv_freq = _rope_angles(positions, tr)
    rope_spec = pl.BlockSpec((None, tr, ROT_HALF), lambda i: (1, i, 0))
    return pl.pallas_call(
        _norm1_kernel,
        grid=(steps,),
        in_specs=[pl.BlockSpec((tm, d), lambda i: (i, 0)),
                  pl.BlockSpec((1, d), lambda i: (0, 0)),
                  pl.BlockSpec((None, 1, d), lambda i: (i // tps, 0, 1)),
                  pl.BlockSpec((None, 1, d), lambda i: (i // tps, 0, 0)),
                  pl.BlockSpec((None, 1, tr), lambda i: (steps + i, 0, 0)),
                  pl.BlockSpec((ROT_HALF, 1), lambda i: (0, 0)),
                  pl.BlockSpec(memory_space=pl.ANY),
                  pl.BlockSpec(memory_space=pl.ANY)],
        out_specs=[pl.BlockSpec((tm, d), lambda i: (i, 0)), rope_spec, rope_spec],
        out_shape=[jax.ShapeDtypeStruct((t, d), BF16),
                   jax.ShapeDtypeStruct(cos.shape, F32), jax.ShapeDtypeStruct(sin.shape, F32)],
        input_output_aliases={6: 1, 7: 2},
        compiler_params=_params("arbitrary"),
        name="norm1",
    )(x2d, norm_g.reshape(1, d), mod3, mod3, pos, inv_freq, cos, sin)


def _inproj_a_kernel(xn_ref, w_ref, cos_ref, sin_ref, *rest, q_tiles, rot_tiles, gate_tile0, n_sub,
                     side_scales):
    n_side = len(side_scales)
    side_in, (o_ref, *side_out, wbf_ref) = rest[:n_side], rest[n_side:]
    j = pl.program_id(0)

    @pl.when(pl.program_id(1) == 0)
    def _():
        scale = jnp.where(j >= gate_tile0, 0.5, 1.0).astype(F32)
        wbf_ref[...] = (w_ref[...] * scale).astype(BF16)

    _side_cast(side_in, side_out, side_scales)
    heads = o_ref.shape[0]
    tr = xn_ref.shape[0] // n_sub

    def proj(rows):
        return jnp.dot(xn_ref[rows, :], wbf_ref[...], preferred_element_type=F32)

    @pl.when(j < rot_tiles)
    def _():
        scale = jnp.where(j < q_tiles, RET_HEAD_DIM ** -0.5, 1.0).astype(F32)
        for s in range(n_sub):
            rows = pl.ds(s * tr, tr)
            acc = proj(rows)
            cos = cos_ref[rows, :] * scale
            sin = sin_ref[rows, :] * scale
            for hh in range(heads):
                lo = hh * RET_HEAD_DIM
                x1 = acc[:, lo:lo + ROT_HALF]
                x2 = acc[:, lo + ROT_HALF:lo + RET_HEAD_DIM]
                o_ref[hh, rows, 0:ROT_HALF] = (x1 * cos - x2 * sin).astype(BF16)
                o_ref[hh, rows, ROT_HALF:RET_HEAD_DIM] = (x2 * cos + x1 * sin).astype(BF16)

    def plain(fn):
        for s in range(n_sub):
            rows = pl.ds(s * tr, tr)
            res = fn(proj(rows)).astype(BF16)
            for hh in range(heads):
                o_ref[hh, rows, :] = res[:, hh * RET_HEAD_DIM:(hh + 1) * RET_HEAD_DIM]

    pl.when(jnp.logical_and(j >= rot_tiles, j < gate_tile0))(functools.partial(plain, lambda a: a))
    pl.when(j >= gate_tile0)(functools.partial(plain, _silu_of_half))


def _inproj_a(xn, w_in, cos, sin, side_weights, side_scales, tm=2048, tn=1024, n_sub=4):
    t, d = xn.shape
    n = 4 * D_RET
    heads = tn // RET_HEAD_DIM
    n_i = t // tm
    side_specs, side_shapes = _side_cast_specs(side_weights, (n // tn) * n_i, lambda j, i: j * n_i + i)
    kern = functools.partial(_inproj_a_kernel, q_tiles=D_RET // tn, rot_tiles=2 * D_RET // tn,
                             gate_tile0=3 * D_RET // tn, n_sub=n_sub, side_scales=tuple(side_scales))
    return pl.pallas_call(
        kern,
        grid=(n // tn, n_i),
        in_specs=[pl.BlockSpec((tm, d), lambda j, i: (i, 0)),
                  pl.BlockSpec((d, tn), lambda j, i: (0, j)),
                  pl.BlockSpec((tm, ROT_HALF), lambda j, i: (i, 0)),
                  pl.BlockSpec((tm, ROT_HALF), lambda j, i: (i, 0))] + side_specs,
        out_specs=[pl.BlockSpec((heads, tm, RET_HEAD_DIM), lambda j, i: (j, i, 0))] + side_specs,
        out_shape=[jax.ShapeDtypeStruct((n // RET_HEAD_DIM, t, RET_HEAD_DIM), BF16)] + side_shapes,
        scratch_shapes=[pltpu.VMEM((d, tn), BF16)],
        compiler_params=_params("arbitrary", "arbitrary"),
        name="inproj_a",
    )(xn, w_in, cos, sin, *side_weights)


def _inproj_c_kernel(xn_ref, wb_ref, wc_ref, wx_ref, wsc_ref, side_ref, o_ref, side_out_ref,
                     wb_bf, wc_bf, wx_bf, halo_ref, *, tiles_per_seq, side_scale):
    j = pl.program_id(0)
    i = pl.program_id(1)

    @pl.when(jnp.logical_and(i == 0, j == 0))
    def _():
        halo_ref[...] = jnp.zeros_like(halo_ref)

    @pl.when(i == 0)
    def _():
        wb_bf[...] = wb_ref[...].astype(BF16)
        wc_bf[...] = wc_ref[...].astype(BF16)
        wx_bf[...] = wx_ref[...].astype(BF16)

    _side_cast([side_ref], [side_out_ref], [side_scale])
    xn = xn_ref[...]
    cg = jnp.dot(xn, wc_bf[...], preferred_element_type=F32)
    xc = jnp.dot(xn, wx_bf[...], preferred_element_type=F32)
    u = cg * xc
    prev = jnp.where(i % tiles_per_seq == 0, 0.0, halo_ref[...])
    conv = _causal_conv3(u, prev, wsc_ref[...])
    halo_ref[...] = u[u.shape[0] - HALO_ROWS:, :]
    bg = jnp.dot(xn, wb_bf[...], preferred_element_type=F32)
    o_ref[...] = (bg * conv).astype(BF16)


def _inproj_c(xn, w_in, w_sc, side_weight, side_scale, seq, tm=1024, tc=512):
    t, d = xn.shape
    nc = d // tc
    n_i = t // tm
    first = 4 * D_RET // tc
    kern = functools.partial(_inproj_c_kernel, tiles_per_seq=seq // tm, side_scale=side_scale)
    wspec = lambda sec: pl.BlockSpec((d, tc), lambda j, i: (0, first + sec * nc + j))
    side_specs, side_shapes = _side_cast_specs([side_weight], nc * n_i, lambda j, i: j * n_i + i)
    return pl.pallas_call(
        kern,
        grid=(nc, n_i),
        in_specs=[pl.BlockSpec((tm, d), lambda j, i: (i, 0)),
                  wspec(0), wspec(1), wspec(2),
                  pl.BlockSpec((CONV_WIDTH, tc), lambda j, i: (0, j))] + side_specs,
        out_specs=[pl.BlockSpec((tm, tc), lambda j, i: (i, j))] + side_specs,
        out_shape=[jax.ShapeDtypeStruct((t, d), BF16)] + side_shapes,
        scratch_shapes=[pltpu.VMEM((d, tc), BF16)] * 3 + [pltpu.VMEM((HALO_ROWS, tc), F32)],
        compiler_params=_params("arbitrary", "arbitrary"),
        name="inproj_c",
    )(xn, w_in, w_in, w_in, w_sc, side_weight)


def _inproj_g_kernel(xn_ref, w_ref, b_ref, side_ref, o_ref, side_out_ref, wbf_ref, *, n_sub):
    @pl.when(pl.program_id(1) == 0)
    def _():
        wbf_ref[...] = (w_ref[...] * 0.5).astype(BF16)

    _side_cast([side_ref], [side_out_ref])
    tr = xn_ref.shape[0] // n_sub
    half_bias = 0.5 * b_ref[...]
    for s in range(n_sub):
        rows = pl.ds(s * tr, tr)
        half_logits = jnp.dot(xn_ref[rows, :], wbf_ref[...], preferred_element_type=F32) + half_bias
        o_ref[rows, :] = _twice_sigmoid_of_half(half_logits).astype(BF16)


def _inproj_g(xn, w_in, b_gate, side_weight, tm=2048, tn=1024, n_sub=4):
    t, d = xn.shape
    n = b_gate.shape[0]
    n_i = t // tm
    first = (w_in.shape[1] - n) // tn
    side_specs, side_shapes = _side_cast_specs([side_weight], (n // tn) * n_i, lambda j, i: j * n_i + i)
    return pl.pallas_call(
        functools.partial(_inproj_g_kernel, n_sub=n_sub),
        grid=(n // tn, n_i),
        in_specs=[pl.BlockSpec((tm, d), lambda j, i: (i, 0)),
                  pl.BlockSpec((d, tn), lambda j, i: (0, first + j)),
                  pl.BlockSpec((1, tn), lambda j, i: (0, j))] + side_specs,
        out_specs=[pl.BlockSpec((tm, tn), lambda j, i: (i, j))] + side_specs,
        out_shape=[jax.ShapeDtypeStruct((t, n), BF16)] + side_shapes,
        scratch_shapes=[pltpu.VMEM((d, tn), BF16)],
        compiler_params=_params("arbitrary", "arbitrary"),
        name="inproj_g",
    )(xn, w_in, b_gate.reshape(1, n), side_weight)


def _ret_kernel(q_ref, k_ref, v_ref, gs_ref, o_ref):
    heads = q_ref.shape[0]
    first = pl.program_id(1) * heads

    def prepare(hh):
        return _ret_scores_and_states(first + hh, q_ref.at[hh], k_ref.at[hh], v_ref.at[hh])

    prep = prepare(0)
    for hh in range(heads):
        next_prep = prepare(hh + 1) if hh + 1 < heads else None
        cols = pl.ds(hh * RET_HEAD_DIM, RET_HEAD_DIM)
        _ret_outputs(prep, q_ref.at[hh], v_ref.at[hh], gs_ref.at[hh], o_ref.at[:, cols])
        prep = next_prep


def _ret_scores_and_states(head_index, q_ref, k_ref, v_ref):
    c = RET_CHUNK
    n_chunks = q_ref.shape[0] // c
    head = jnp.full((1, RET_HEAD_DIM), head_index, jnp.int32).astype(F32)
    log_gamma = jnp.log(1.0 - jnp.exp2(-5.0 - head))
    row = lax.broadcasted_iota(jnp.int32, (c, c), 0).astype(F32)
    col = lax.broadcasted_iota(jnp.int32, (c, c), 1).astype(F32)
    diff = row - col
    decay = jnp.where(diff >= 0, jnp.exp(log_gamma * jnp.maximum(diff, 0.0)), 0.0)
    q_dec = jnp.exp(log_gamma * (row + 1.0))
    k_dec = jnp.exp(log_gamma * (c - 1.0 - row))
    chunk_dec = jnp.exp(log_gamma * float(c))
    chunks = [pl.ds(n * c, c) for n in range(n_chunks)]

    kvs = []
    for rows in chunks[:-1]:
        kd = (k_ref[rows, :].astype(F32) * k_dec).astype(BF16)
        kvs.append(lax.dot_general(kd, v_ref[rows, :], (((0,), (0,)), ((), ())),
                                   preferred_element_type=F32))
    scores = [lax.dot_general(q_ref[rows, :], k_ref[rows, :], (((1,), (1,)), ((), ())),
                              preferred_element_type=F32) for rows in chunks]
    states = [None]
    for kv in kvs:
        states.append(kv if states[-1] is None else chunk_dec * states[-1] + kv)
    return chunks, scores, states, decay, q_dec


def _ret_outputs(prep, q_ref, v_ref, gs_ref, o_ref):
    chunks, scores, states, decay, q_dec = prep
    for rows, s, state in zip(chunks, scores, states):
        y = jnp.dot((s * decay).astype(BF16), v_ref[rows, :], preferred_element_type=F32)
        if state is not None:
            y = y + q_dec * jnp.dot(q_ref[rows, :], state.astype(BF16), preferred_element_type=F32)
        yn = y * lax.rsqrt(jnp.mean(y * y, axis=-1, keepdims=True) + EPS)
        o_ref[rows, :] = yn.astype(BF16) * gs_ref[rows, :]


def _retention(qkvg, batch, seq, heads_per_step=4):
    t = qkvg.shape[1]
    groups = RET_HEADS // heads_per_step
    spec = lambda sec: pl.BlockSpec((heads_per_step, seq, RET_HEAD_DIM),
                                    lambda b, g: (sec * groups + g, b, 0))
    return pl.pallas_call(
        _ret_kernel,
        grid=(batch, groups),
        in_specs=[spec(0), spec(1), spec(2), spec(3)],
        out_specs=pl.BlockSpec((seq, heads_per_step * RET_HEAD_DIM), lambda b, g: (b, g)),
        out_shape=jax.ShapeDtypeStruct((t, D_RET), BF16),
        compiler_params=_params("arbitrary", "arbitrary"),
        name="ret",
    )(qkvg, qkvg, qkvg, qkvg)


def _merge_kernel(yr_ref, yc_ref, ga_ref, gb_ref, wr_ref, wc_ref, side_ref, o_ref, side_out_ref):
    _side_cast([side_ref], [side_out_ref])
    ya = jnp.dot(yr_ref[...], wr_ref[...], preferred_element_type=F32)
    yb = jnp.dot(yc_ref[...], wc_ref[...], preferred_element_type=F32)
    o_ref[...] = (ga_ref[...].astype(F32) * ya + gb_ref[...].astype(F32) * yb).astype(BF16)


def _merge(yr, yc, gates, w_ret_o_bf, w_conv_o_bf, side_weight, tm=512):
    t, d = yr.shape
    resident = lambda: pl.BlockSpec((d, d), lambda i: (0, 0), pipeline_mode=pl.Buffered(1))
    side_specs, side_shapes = _side_cast_specs([side_weight], t // tm, lambda i: i)
    return pl.pallas_call(
        _merge_kernel,
        grid=(t // tm,),
        in_specs=[pl.BlockSpec((tm, d), lambda i: (i, 0)),
                  pl.BlockSpec((tm, d), lambda i: (i, 0)),
                  pl.BlockSpec((tm, d), lambda i: (i, 0)),
                  pl.BlockSpec((tm, d), lambda i: (i, 1)),
                  resident(), resident()] + side_specs,
        out_specs=[pl.BlockSpec((tm, d), lambda i: (i, 0))] + side_specs,
        out_shape=[jax.ShapeDtypeStruct((t, d), BF16)] + side_shapes,
        compiler_params=_params("arbitrary"),
        name="merge",
    )(yr, yc, gates, gates, w_ret_o_bf, w_conv_o_bf, side_weight)


def _mixout_kernel(m_ref, w_ref, x_ref, g1_ref, g_ref, sc_ref, sh_ref, h_ref, xn_ref, *, n_sub):
    tr = m_ref.shape[0] // n_sub
    for s in range(n_sub):
        rows = pl.ds(s * tr, tr)
        mixed = jnp.dot(m_ref[rows, :], w_ref[...], preferred_element_type=F32)
        h = x_ref[rows, :] + g1_ref[...] * mixed
        h_ref[rows, :] = h
        xn_ref[rows, :] = _mod_rmsnorm(h_ref, rows, g_ref[...], sc_ref[...], sh_ref[...]).astype(BF16)


def _mixout(m, w_mix_o_bf, x2d, mod3, norm_g, seq, tm=512, n_sub=2):
    t, d = x2d.shape
    tps = seq // tm
    mod_spec = lambda k: pl.BlockSpec((None, 1, d), lambda i: (i // tps, 0, k))
    return pl.pallas_call(
        functools.partial(_mixout_kernel, n_sub=n_sub),
        grid=(t // tm,),
        in_specs=[pl.BlockSpec((tm, d), lambda i: (i, 0)),
                  pl.BlockSpec((d, d), lambda i: (0, 0), pipeline_mode=pl.Buffered(1)),
                  pl.BlockSpec((tm, d), lambda i: (i, 0)),
                  mod_spec(2), pl.BlockSpec((1, d), lambda i: (0, 0)), mod_spec(4), mod_spec(3)],
        out_specs=[pl.BlockSpec((tm, d), lambda i: (i, 0)), pl.BlockSpec((tm, d), lambda i: (i, 0))],
        out_shape=[jax.ShapeDtypeStruct((t, d), F32), jax.ShapeDtypeStruct((t, d), BF16)],
        compiler_params=_params("arbitrary"),
        name="mixout",
    )(m, w_mix_o_bf, x2d, mod3, norm_g.reshape(1, d), mod3, mod3)


def _ffn_kernel(xn_ref, wu_ref, wg_ref, wfc_ref, bfc_ref, wd_ref, h_hbm, g2_ref, fg_ref,
                o_ref, halo_ref, h_sem, *, tiles_per_seq, sub_width, last_width):
    i = pl.program_id(0)
    f = pl.program_id(1)
    last = pl.num_programs(1) - 1
    tm = o_ref.shape[0]

    @pl.when(jnp.logical_and(i == 0, f == 0))
    def _():
        halo_ref[...] = jnp.zeros_like(halo_ref)

    def h1_copy():
        return pltpu.make_async_copy(h_hbm.at[pl.ds(i * tm, tm), :], o_ref, h_sem)

    def step(first, width):
        if first:
            h1_copy().start()
        xn = xn_ref[...]
        not_seq_start = i % tiles_per_seq != 0
        col_blocks = [pl.ds(s * sub_width, sub_width) for s in range(width // sub_width)]
        up_gate = []
        for s, cols in enumerate(col_blocks):
            up_gate.append((jnp.dot(xn, wu_ref[:, cols], preferred_element_type=F32),
                            jnp.dot(xn, wg_ref[:, cols], preferred_element_type=F32)))
            if first and s == 0:
                h1_copy().wait()
        acts = []
        for cols, (up, gpre) in zip(col_blocks, up_gate):
            prev = jnp.where(not_seq_start, halo_ref[f, :, cols], 0.0)
            half_gt = _causal_conv3(gpre, prev, wfc_ref[:, cols]) + 0.5 * bfc_ref[:, cols]
            halo_ref[f, :, cols] = gpre[gpre.shape[0] - HALO_ROWS:, :]
            acts.append((_silu_of_half(half_gt) * up).astype(BF16))
        n_head = len(acts) - 1
        head = None
        if n_head:
            head = jnp.dot(jnp.concatenate(acts[:-1], axis=1), wd_ref[pl.ds(0, n_head * sub_width), :],
                           preferred_element_type=F32)
        part = jnp.dot(acts[-1], wd_ref[col_blocks[-1], :], preferred_element_type=F32)
        if head is not None:
            part = head + part
        o_ref[...] += g2_ref[...] * part

    tile_width = wu_ref.shape[1]
    pl.when(f == 0)(functools.partial(step, True, tile_width))
    pl.when(jnp.logical_and(f > 0, f < last))(functools.partial(step, False, tile_width))
    pl.when(f == last)(functools.partial(step, False, last_width))

    @pl.when(f == last)
    def _():
        r = lax.rsqrt(jnp.mean(jnp.square(o_ref[...]), axis=-1, keepdims=True) + EPS)
        o_ref[...] = (o_ref[...] * r) * fg_ref[...]


def _ffn(xn2, w_up_bf, w_gate_half_bf, w_ffconv, b_ffconv, w_down_bf, h1, mod3, final_g, seq,
         tm=1024, tf=768, sub_width=256):
    t, d = xn2.shape
    dff = w_up_bf.shape[1]
    nf = pl.cdiv(dff, tf)
    tps = seq // tm
    assert nf >= 2 and tf % sub_width == 0 and dff % sub_width == 0
    kern = functools.partial(_ffn_kernel, tiles_per_seq=tps, sub_width=sub_width,
                             last_width=dff - (nf - 1) * tf)
    return pl.pallas_call(
        kern,
        grid=(t // tm, nf),
        in_specs=[pl.BlockSpec((tm, d), lambda i, f: (i, 0)),
                  pl.BlockSpec((d, tf), lambda i, f: (0, f)),
                  pl.BlockSpec((d, tf), lambda i, f: (0, f)),
                  pl.BlockSpec((CONV_WIDTH, tf), lambda i, f: (0, f)),
                  pl.BlockSpec((1, tf), lambda i, f: (0, f)),
                  pl.BlockSpec((tf, d), lambda i, f: (f, 0)),
                  pl.BlockSpec(memory_space=pl.ANY),
                  pl.BlockSpec((None, 1, d), lambda i, f: (i // tps, 0, 5)),
                  pl.BlockSpec((1, d), lambda i, f: (0, 0))],
        out_specs=pl.BlockSpec((tm, d), lambda i, f: (i, 0)),
        out_shape=jax.ShapeDtypeStruct((t, d), F32),
        scratch_shapes=[pltpu.VMEM((nf, HALO_ROWS, tf), F32), pltpu.SemaphoreType.DMA(())],
        compiler_params=_params("arbitrary", "arbitrary"),
        name="ffn",
    )(xn2, w_up_bf, w_gate_half_bf, w_ffconv, b_ffconv.reshape(1, dff), w_down_bf, h1, mod3,
      final_g.reshape(1, d))


def kernel(x, c, positions, norm1_g, norm2_g, w_ada, b_ada, w_in, b_gate, w_sc, w_ret_o, w_conv_o,
           w_mix_o, w_up, w_gate, w_ffconv, b_ffconv, w_down, final_g):
    batch, seq, d = x.shape
    depth = w_in.shape[0]
    assert depth == 1, "final-norm fusion assumes a single layer"
    assert d == D_RET and w_in.shape[2] == 4 * D_RET + 5 * d, (d, w_in.shape)
    assert seq % 2048 == 0 and seq % RET_CHUNK == 0 and batch % 2 == 0, (batch, seq)
    t = batch * seq
    h = x.reshape(t, d)
    for l in range(depth):
        mod, cos, sin = _ada_rope(c, w_ada[l], b_ada[l], positions)
        mod3 = mod.reshape(batch, 1, 6 * d)
        xn, cos, sin = _norm1(h, norm1_g[l], mod3, positions, cos, sin, seq)
        qkvg, w_ret_o_bf, w_conv_o_bf, w_mix_o_half_bf = _inproj_a(
            xn, w_in[l], cos.reshape(t, ROT_HALF), sin.reshape(t, ROT_HALF),
            [w_ret_o[l], w_conv_o[l], w_mix_o[l]], [None, None, 0.5])
        yc, w_gate_half_bf = _inproj_c(xn, w_in[l], w_sc[l], w_gate[l], 0.5, seq)
        gates_x2, w_up_bf = _inproj_g(xn, w_in[l], b_gate[l], w_up[l])
        yr = _retention(qkvg, batch, seq)
        merged_x2, w_down_bf = _merge(yr, yc, gates_x2, w_ret_o_bf, w_conv_o_bf, w_down[l])
        h1, xn2 = _mixout(merged_x2, w_mix_o_half_bf, h, mod3, norm2_g[l], seq)
        h = _ffn(xn2, w_up_bf, w_gate_half_bf, w_ffconv[l], b_ffconv[l], w_down_bf, h1, mod3, final_g, seq)
    return h.reshape(batch, seq, d)
```

```python
import functools

import jax
import jax.numpy as jnp
from jax import lax
from jax.experimental import pallas as pl
from jax.experimental.pallas import tpu as pltpu

F32 = jnp.float32
BF16 = jnp.bfloat16

RET_HEADS = 8
RET_HEAD_DIM = 256
D_RET = RET_HEADS * RET_HEAD_DIM
ROT_HALF = RET_HEAD_DIM // 2
CONV_WIDTH = 3
ROPE_BASE = 10000.0
EPS = 1e-6
RET_CHUNK = 256
HALO_ROWS = 8

V7X_VMEM_LIMIT_BYTES = 56 * 1024 * 1024


def _params(*semantics):
    return pltpu.CompilerParams(dimension_semantics=semantics,
                                vmem_limit_bytes=V7X_VMEM_LIMIT_BYTES)


def _twice_sigmoid_of_half(h):
    return 1.0 + jnp.tanh(h)


def _silu_of_half(h):
    return h + h * jnp.tanh(h)


def _silu(v):
    return _silu_of_half(0.5 * v)


def _mod_rmsnorm(src_ref, rows, gain, scale, shift):
    x = src_ref[rows, :]
    r = lax.rsqrt(jnp.mean(x * x, axis=-1, keepdims=True) + EPS)
    return (src_ref[rows, :] * r) * (gain * (1.0 + scale)) + shift


def _causal_conv3(u, prev, w):
    def taps(a):
        return w[0:1, :] * pltpu.roll(a, 2, 0) + w[1:2, :] * pltpu.roll(a, 1, 0) + w[2:3, :] * a

    head = taps(jnp.concatenate([prev, u[:HALO_ROWS, :]], axis=0))[HALO_ROWS:, :]
    return jnp.concatenate([head, taps(u)[HALO_ROWS:, :]], axis=0)


def _side_cast_specs(weights, steps, step_index):
    specs = [pl.BlockSpec((w.shape[0] // steps, w.shape[1]), lambda *g: (step_index(*g), 0)) for w in weights]
    shapes = [jax.ShapeDtypeStruct(w.shape, BF16) for w in weights]
    return specs, shapes


def _side_cast(in_refs, out_refs, scales=None):
    for src, dst, scale in zip(in_refs, out_refs, scales or [None] * len(in_refs)):
        w = src[...] if scale is None else src[...] * scale
        dst[...] = w.astype(BF16)


def _rope_angles(positions, tr):
    inv_freq = ROPE_BASE ** (-jnp.arange(ROT_HALF, dtype=F32) / ROT_HALF)
    return positions.reshape(positions.size // tr, 1, tr), inv_freq.reshape(ROT_HALF, 1)


def _rope_tables(pos_ref, inv_ref):
    ang_t = inv_ref[...] * pos_ref[...].astype(F32)
    return jnp.cos(ang_t).T, jnp.sin(ang_t).T


def _ada_rope_kernel(c_ref, w_ref, b_ref, pos_ref, inv_ref, mod_ref, cos_ref, sin_ref):
    @pl.when(pl.program_id(0) == 0)
    def _():
        mod_ref[...] = jnp.broadcast_to(b_ref[...], mod_ref.shape)

    ca = _silu(c_ref[...]).astype(BF16)
    mod_ref[...] += jnp.dot(ca, w_ref[...].astype(BF16), preferred_element_type=F32)
    cos, sin = _rope_tables(pos_ref, inv_ref)
    cos_ref[0] = cos
    sin_ref[0] = sin
    cos_ref[1] = jnp.zeros_like(cos)
    sin_ref[1] = jnp.zeros_like(sin)


def _ada_rope(c, w_ada, b_ada, positions, steps=8):
    b, d = c.shape
    n = w_ada.shape[1]
    half = positions.size // 2
    tk, tr = d // steps, half // steps
    pos, inv_freq = _rope_angles(positions, tr)
    rope_spec = pl.BlockSpec((2, tr, ROT_HALF), lambda j: (0, j, 0))
    return pl.pallas_call(
        _ada_rope_kernel,
        grid=(steps,),
        in_specs=[pl.BlockSpec((b, tk), lambda j: (0, j)),
                  pl.BlockSpec((tk, n), lambda j: (j, 0)),
                  pl.BlockSpec((1, n), lambda j: (0, 0)),
                  pl.BlockSpec((None, 1, tr), lambda j: (j, 0, 0)),
                  pl.BlockSpec((ROT_HALF, 1), lambda j: (0, 0))],
        out_specs=[pl.BlockSpec((b, n), lambda j: (0, 0)), rope_spec, rope_spec],
        out_shape=[jax.ShapeDtypeStruct((b, n), F32),
                   jax.ShapeDtypeStruct((2, half, ROT_HALF), F32),
                   jax.ShapeDtypeStruct((2, half, ROT_HALF), F32)],
        compiler_params=_params("arbitrary"),
        name="ada_rope",
    )(c, w_ada, b_ada.reshape(1, n), pos, inv_freq)


def _norm1_kernel(x_ref, g_ref, sc_ref, sh_ref, pos_ref, inv_ref, cos_in, sin_in, o_ref, cos_ref, sin_ref):
    del cos_in, sin_in
    o_ref[...] = _mod_rmsnorm(x_ref, slice(None), g_ref[...], sc_ref[...], sh_ref[...]).astype(BF16)
    cos_ref[...], sin_ref[...] = _rope_tables(pos_ref, inv_ref)


def _norm1(x2d, norm_g, mod3, positions, cos, sin, seq, tm=1024):
    t, d = x2d.shape
    tps = seq // tm
    steps = t // tm
    tr = cos.shape[1] // steps
    pos, inv_freq = _rope_angles(positions, tr)
    rope_spec = pl.BlockSpec((None, tr, ROT_HALF), lambda i: (1, i, 0))
    return pl.pallas_call(
        _norm1_kernel,
        grid=(steps,),
        in_specs=[pl.BlockSpec((tm, d), lambda i: (i, 0)),
                  pl.BlockSpec((1, d), lambda i: (0, 0)),
                  pl.BlockSpec((None, 1, d), lambda i: (i // tps, 0, 1)),
                  pl.BlockSpec((None, 1, d), lambda i: (i // tps, 0, 0)),
                  pl.BlockSpec((None, 1, tr), lambda i: (steps + i, 0, 0)),
                  pl.BlockSpec((ROT_HALF, 1), lambda i: (0, 0)),
                  pl.BlockSpec(memory_space=pl.ANY),
                  pl.BlockSpec(memory_space=pl.ANY)],
        out_specs=[pl.BlockSpec((tm, d), lambda i: (i, 0)), rope_spec, rope_spec],
        out_shape=[jax.ShapeDtypeStruct((t, d), BF16),
                   jax.ShapeDtypeStruct(cos.shape, F32), jax.ShapeDtypeStruct(sin.shape, F32)],
        input_output_aliases={6: 1, 7: 2},
        compiler_params=_params("arbitrary"),
        name="norm1",
    )(x2d, norm_g.reshape(1, d), mod3, mod3, pos, inv_freq, cos, sin)


def _inproj_a_kernel(xn_ref, w_ref, cos_ref, sin_ref, *rest, q_tiles, rot_tiles, gate_tile0, n_sub,
                     side_scales):
    n_side = len(side_scales)
    side_in, (o_ref, *side_out, wbf_ref) = rest[:n_side], rest[n_side:]
    j = pl.program_id(0)

    @pl.when(pl.program_id(1) == 0)
    def _():
        scale = jnp.where(j >= gate_tile0, 0.5, 1.0).astype(F32)
        wbf_ref[...] = (w_ref[...] * scale).astype(BF16)

    _side_cast(side_in, side_out, side_scales)
    heads = o_ref.shape[0]
    tr = xn_ref.shape[0] // n_sub

    def proj(rows):
        return jnp.dot(xn_ref[rows, :], wbf_ref[...], preferred_element_type=F32)

    @pl.when(j < rot_tiles)
    def _():
        scale = jnp.where(j < q_tiles, RET_HEAD_DIM ** -0.5, 1.0).astype(F32)
        for s in range(n_sub):
            rows = pl.ds(s * tr, tr)
            acc = proj(rows)
            cos = cos_ref[rows, :] * scale
            sin = sin_ref[rows, :] * scale
            for hh in range(heads):
                lo = hh * RET_HEAD_DIM
                x1 = acc[:, lo:lo + ROT_HALF]
                x2 = acc[:, lo + ROT_HALF:lo + RET_HEAD_DIM]
                o_ref[hh, rows, 0:ROT_HALF] = (x1 * cos - x2 * sin).astype(BF16)
                o_ref[hh, rows, ROT_HALF:RET_HEAD_DIM] = (x2 * cos + x1 * sin).astype(BF16)

    def plain(fn):
        for s in range(n_sub):
            rows = pl.ds(s * tr, tr)
            res = fn(proj(rows)).astype(BF16)
            for hh in range(heads):
                o_ref[hh, rows, :] = res[:, hh * RET_HEAD_DIM:(hh + 1) * RET_HEAD_DIM]

    pl.when(jnp.logical_and(j >= rot_tiles, j < gate_tile0))(functools.partial(plain, lambda a: a))
    pl.when(j >= gate_tile0)(functools.partial(plain, _silu_of_half))


def _inproj_a(xn, w_in, cos, sin, side_weights, side_scales, tm=2048, tn=1024, n_sub=4):
    t, d = xn.shape
    n = 4 * D_RET
    heads = tn // RET_HEAD_DIM
    n_i = t // tm
    side_specs, side_shapes = _side_cast_specs(side_weights, (n // tn) * n_i, lambda j, i: j * n_i + i)
    kern = functools.partial(_inproj_a_kernel, q_tiles=D_RET // tn, rot_tiles=2 * D_RET // tn,
                             gate_tile0=3 * D_RET // tn, n_sub=n_sub, side_scales=tuple(side_scales))
    return pl.pallas_call(
        kern,
        grid=(n // tn, n_i),
        in_specs=[pl.BlockSpec((tm, d), lambda j, i: (i, 0)),
                  pl.BlockSpec((d, tn), lambda j, i: (0, j)),
                  pl.BlockSpec((tm, ROT_HALF), lambda j, i: (i, 0)),
                  pl.BlockSpec((tm, ROT_HALF), lambda j, i: (i, 0))] + side_specs,
        out_specs=[pl.BlockSpec((heads, tm, RET_HEAD_DIM), lambda j, i: (j, i, 0))] + side_specs,
        out_shape=[jax.ShapeDtypeStruct((n // RET_HEAD_DIM, t, RET_HEAD_DIM), BF16)] + side_shapes,
        scratch_shapes=[pltpu.VMEM((d, tn), BF16)],
        compiler_params=_params("arbitrary", "arbitrary"),
        name="inproj_a",
    )(xn, w_in, cos, sin, *side_weights)


def _inproj_c_kernel(xn_ref, wb_ref, wc_ref, wx_ref, wsc_ref, side_ref, o_ref, side_out_ref,
                     wb_bf, wc_bf, wx_bf, halo_ref, *, tiles_per_seq, side_scale):
    j = pl.program_id(0)
    i = pl.program_id(1)

    @pl.when(jnp.logical_and(i == 0, j == 0))
    def _():
        halo_ref[...] = jnp.zeros_like(halo_ref)

    @pl.when(i == 0)
    def _():
        wb_bf[...] = wb_ref[...].astype(BF16)
        wc_bf[...] = wc_ref[...].astype(BF16)
        wx_bf[...] = wx_ref[...].astype(BF16)

    _side_cast([side_ref], [side_out_ref], [side_scale])
    xn = xn_ref[...]
    cg = jnp.dot(xn, wc_bf[...], preferred_element_type=F32)
    xc = jnp.dot(xn, wx_bf[...], preferred_element_type=F32)
    u = cg * xc
    prev = jnp.where(i % tiles_per_seq == 0, 0.0, halo_ref[...])
    conv = _causal_conv3(u, prev, wsc_ref[...])
    halo_ref[...] = u[u.shape[0] - HALO_ROWS:, :]
    bg = jnp.dot(xn, wb_bf[...], preferred_element_type=F32)
    o_ref[...] = (bg * conv).astype(BF16)


def _inproj_c(xn, w_in, w_sc, side_weight, side_scale, seq, tm=1024, tc=512):
    t, d = xn.shape
    nc = d // tc
    n_i = t // tm
    first = 4 * D_RET // tc
    kern = functools.partial(_inproj_c_kernel, tiles_per_seq=seq // tm, side_scale=side_scale)
    wspec = lambda sec: pl.BlockSpec((d, tc), lambda j, i: (0, first + sec * nc + j))
    side_specs, side_shapes = _side_cast_specs([side_weight], nc * n_i, lambda j, i: j * n_i + i)
    return pl.pallas_call(
        kern,
        grid=(nc, n_i),
        in_specs=[pl.BlockSpec((tm, d), lambda j, i: (i, 0)),
                  wspec(0), wspec(1), wspec(2),
                  pl.BlockSpec((CONV_WIDTH, tc), lambda j, i: (0, j))] + side_specs,
        out_specs=[pl.BlockSpec((tm, tc), lambda j, i: (i, j))] + side_specs,
        out_shape=[jax.ShapeDtypeStruct((t, d), BF16)] + side_shapes,
        scratch_shapes=[pltpu.VMEM((d, tc), BF16)] * 3 + [pltpu.VMEM((HALO_ROWS, tc), F32)],
        compiler_params=_params("arbitrary", "arbitrary"),
        name="inproj_c",
    )(xn, w_in, w_in, w_in, w_sc, side_weight)


def _inproj_g_kernel(xn_ref, w_ref, b_ref, side_ref, o_ref, side_out_ref, wbf_ref, *, n_sub):
    @pl.when(pl.program_id(1) == 0)
    def _():
        wbf_ref[...] = (w_ref[...] * 0.5).astype(BF16)

    _side_cast([side_ref], [side_out_ref])
    tr = xn_ref.shape[0] // n_sub
    half_bias = 0.5 * b_ref[...]
    for s in range(n_sub):
        rows = pl.ds(s * tr, tr)
        half_logits = jnp.dot(xn_ref[rows, :], wbf_ref[...], preferred_element_type=F32) + half_bias
        o_ref[rows, :] = _twice_sigmoid_of_half(half_logits).astype(BF16)


def _inproj_g(xn, w_in, b_gate, side_weight, tm=2048, tn=1024, n_sub=4):
    t, d = xn.shape
    n = b_gate.shape[0]
    n_i = t // tm
    first = (w_in.shape[1] - n) // tn
    side_specs, side_shapes = _side_cast_specs([side_weight], (n // tn) * n_i, lambda j, i: j * n_i + i)
    return pl.pallas_call(
        functools.partial(_inproj_g_kernel, n_sub=n_sub),
        grid=(n // tn, n_i),
        in_specs=[pl.BlockSpec((tm, d), lambda j, i: (i, 0)),
                  pl.BlockSpec((d, tn), lambda j, i: (0, first + j)),
                  pl.BlockSpec((1, tn), lambda j, i: (0, j))] + side_specs,
        out_specs=[pl.BlockSpec((tm, tn), lambda j, i: (i, j))] + side_specs,
        out_shape=[jax.ShapeDtypeStruct((t, n), BF16)] + side_shapes,
        scratch_shapes=[pltpu.VMEM((d, tn), BF16)],
        compiler_params=_params("arbitrary", "arbitrary"),
        name="inproj_g",
    )(xn, w_in, b_gate.reshape(1, n), side_weight)


def _ret_kernel(q_ref, k_ref, v_ref, gs_ref, o_ref):
    heads = q_ref.shape[0]
    first = pl.program_id(1) * heads

    def prepare(hh):
        return _ret_scores_and_states(first + hh, q_ref.at[hh], k_ref.at[hh], v_ref.at[hh])

    prep = prepare(0)
    for hh in range(heads):
        next_prep = prepare(hh + 1) if hh + 1 < heads else None
        cols = pl.ds(hh * RET_HEAD_DIM, RET_HEAD_DIM)
        _ret_outputs(prep, q_ref.at[hh], v_ref.at[hh], gs_ref.at[hh], o_ref.at[:, cols])
        prep = next_prep


def _ret_scores_and_states(head_index, q_ref, k_ref, v_ref):
    c = RET_CHUNK
    n_chunks = q_ref.shape[0] // c
    head = jnp.full((1, RET_HEAD_DIM), head_index, jnp.int32).astype(F32)
    log_gamma = jnp.log(1.0 - jnp.exp2(-5.0 - head))
    row = lax.broadcasted_iota(jnp.int32, (c, c), 0).astype(F32)
    col = lax.broadcasted_iota(jnp.int32, (c, c), 1).astype(F32)
    diff = row - col
    decay = jnp.where(diff >= 0, jnp.exp(log_gamma * jnp.maximum(diff, 0.0)), 0.0)
    q_dec = jnp.exp(log_gamma * (row + 1.0))
    k_dec = jnp.exp(log_gamma * (c - 1.0 - row))
    chunk_dec = jnp.exp(log_gamma * float(c))
    chunks = [pl.ds(n * c, c) for n in range(n_chunks)]

    kvs = []
    for rows in chunks[:-1]:
        kd = (k_ref[rows, :].astype(F32) * k_dec).astype(BF16)
        kvs.append(lax.dot_general(kd, v_ref[rows, :], (((0,), (0,)), ((), ())),
                                   preferred_element_type=F32))
    scores = [lax.dot_general(q_ref[rows, :], k_ref[rows, :], (((1,), (1,)), ((), ())),
                              preferred_element_type=F32) for rows in chunks]
    states = [None]
    for kv in kvs:
        states.append(kv if states[-1] is None else chunk_dec * states[-1] + kv)
    return chunks, scores, states, decay, q_dec


def _ret_outputs(prep, q_ref, v_ref, gs_ref, o_ref):
    chunks, scores, states, decay, q_dec = prep
    for rows, s, state in zip(chunks, scores, states):
        y = jnp.dot((s * decay).astype(BF16), v_ref[rows, :], preferred_element_type=F32)
        if state is not None:
            y = y + q_dec * jnp.dot(q_ref[rows, :], state.astype(BF16), preferred_element_type=F32)
        yn = y * lax.rsqrt(jnp.mean(y * y, axis=-1, keepdims=True) + EPS)
        o_ref[rows, :] = yn.astype(BF16) * gs_ref[rows, :]


def _retention(qkvg, batch, seq, heads_per_step=4):
    t = qkvg.shape[1]
    groups = RET_HEADS // heads_per_step
    spec = lambda sec: pl.BlockSpec((heads_per_step, seq, RET_HEAD_DIM),
                                    lambda b, g: (sec * groups + g, b, 0))
    return pl.pallas_call(
        _ret_kernel,
        grid=(batch, groups),
        in_specs=[spec(0), spec(1), spec(2), spec(3)],
        out_specs=pl.BlockSpec((seq, heads_per_step * RET_HEAD_DIM), lambda b, g: (b, g)),
        out_shape=jax.ShapeDtypeStruct((t, D_RET), BF16),
        compiler_params=_params("arbitrary", "arbitrary"),
        name="ret",
    )(qkvg, qkvg, qkvg, qkvg)


def _merge_kernel(yr_ref, yc_ref, ga_ref, gb_ref, wr_ref, wc_ref, side_ref, o_ref, side_out_ref):
    _side_cast([side_ref], [side_out_ref])
    ya = jnp.dot(yr_ref[...], wr_ref[...], preferred_element_type=F32)
    yb = jnp.dot(yc_ref[...], wc_ref[...], preferred_element_type=F32)
    o_ref[...] = (ga_ref[...].astype(F32) * ya + gb_ref[...].astype(F32) * yb).astype(BF16)


def _merge(yr, yc, gates, w_ret_o_bf, w_conv_o_bf, side_weight, tm=512):
    t, d = yr.shape
    resident = lambda: pl.BlockSpec((d, d), lambda i: (0, 0), pipeline_mode=pl.Buffered(1))
    side_specs, side_shapes = _side_cast_specs([side_weight], t // tm, lambda i: i)
    return pl.pallas_call(
        _merge_kernel,
        grid=(t // tm,),
        in_specs=[pl.BlockSpec((tm, d), lambda i: (i, 0)),
                  pl.BlockSpec((tm, d), lambda i: (i, 0)),
                  pl.BlockSpec((tm, d), lambda i: (i, 0)),
                  pl.BlockSpec((tm, d), lambda i: (i, 1)),
                  resident(), resident()] + side_specs,
        out_specs=[pl.BlockSpec((tm, d), lambda i: (i, 0))] + side_specs,
        out_shape=[jax.ShapeDtypeStruct((t, d), BF16)] + side_shapes,
        compiler_params=_params("arbitrary"),
        name="merge",
    )(yr, yc, gates, gates, w_ret_o_bf, w_conv_o_bf, side_weight)


def _mixout_kernel(m_ref, w_ref, x_ref, g1_ref, g_ref, sc_ref, sh_ref, h_ref, xn_ref, *, n_sub):
    tr = m_ref.shape[0] // n_sub
    for s in range(n_sub):
        rows = pl.ds(s * tr, tr)
        mixed = jnp.dot(m_ref[rows, :], w_ref[...], preferred_element_type=F32)
        h = x_ref[rows, :] + g1_ref[...] * mixed
        h_ref[rows, :] = h
        xn_ref[rows, :] = _mod_rmsnorm(h_ref, rows, g_ref[...], sc_ref[...], sh_ref[...]).astype(BF16)


def _mixout(m, w_mix_o_bf, x2d, mod3, norm_g, seq, tm=512, n_sub=2):
    t, d = x2d.shape
    tps = seq // tm
    mod_spec = lambda k: pl.BlockSpec((None, 1, d), lambda i: (i // tps, 0, k))
    return pl.pallas_call(
        functools.partial(_mixout_kernel, n_sub=n_sub),
        grid=(t // tm,),
        in_specs=[pl.BlockSpec((tm, d), lambda i: (i, 0)),
                  pl.BlockSpec((d, d), lambda i: (0, 0), pipeline_mode=pl.Buffered(1)),
                  pl.BlockSpec((tm, d), lambda i: (i, 0)),
                  mod_spec(2), pl.BlockSpec((1, d), lambda i: (0, 0)), mod_spec(4), mod_spec(3)],
        out_specs=[pl.BlockSpec((tm, d), lambda i: (i, 0)), pl.BlockSpec((tm, d), lambda i: (i, 0))],
        out_shape=[jax.ShapeDtypeStruct((t, d), F32), jax.ShapeDtypeStruct((t, d), BF16)],
        compiler_params=_params("arbitrary"),
        name="mixout",
    )(m, w_mix_o_bf, x2d, mod3, norm_g.reshape(1, d), mod3, mod3)


def _ffn_kernel(xn_ref, wu_ref, wg_ref, wfc_ref, bfc_ref, wd_ref, h_hbm, g2_ref, fg_ref,
                o_ref, halo_ref, h_sem, *, tiles_per_seq, sub_width, last_width):
    i = pl.program_id(0)
    f = pl.program_id(1)
    last = pl.num_programs(1) - 1
    tm = o_ref.shape[0]

    @pl.when(jnp.logical_and(i == 0, f == 0))
    def _():
        halo_ref[...] = jnp.zeros_like(halo_ref)

    def h1_copy():
        return pltpu.make_async_copy(h_hbm.at[pl.ds(i * tm, tm), :], o_ref, h_sem)

    def step(first, width):
        if first:
            h1_copy().start()
        xn = xn_ref[...]
        not_seq_start = i % tiles_per_seq != 0
        col_blocks = [pl.ds(s * sub_width, sub_width) for s in range(width // sub_width)]
        up_gate = [(jnp.dot(xn, wu_ref[:, cols], preferred_element_type=F32),
                    jnp.dot(xn, wg_ref[:, cols], preferred_element_type=F32)) for cols in col_blocks]
        acts = []
        for cols, (up, gpre) in zip(col_blocks, up_gate):
            prev = jnp.where(not_seq_start, halo_ref[f, :, cols], 0.0)
            half_gt = _causal_conv3(gpre, prev, wfc_ref[:, cols]) + 0.5 * bfc_ref[:, cols]
            halo_ref[f, :, cols] = gpre[gpre.shape[0] - HALO_ROWS:, :]
            acts.append((_silu_of_half(half_gt) * up).astype(BF16))
        n_head = len(acts) - 1
        head = None
        if n_head:
            head = jnp.dot(jnp.concatenate(acts[:-1], axis=1), wd_ref[pl.ds(0, n_head * sub_width), :],
                           preferred_element_type=F32)
        part = jnp.dot(acts[-1], wd_ref[col_blocks[-1], :], preferred_element_type=F32)
        if head is not None:
            part = head + part
        if first:
            h1_copy().wait()
        o_ref[...] += g2_ref[...] * part

    tile_width = wu_ref.shape[1]
    pl.when(f == 0)(functools.partial(step, True, tile_width))
    pl.when(jnp.logical_and(f > 0, f < last))(functools.partial(step, False, tile_width))
    pl.when(f == last)(functools.partial(step, False, last_width))

    @pl.when(f == last)
    def _():
        r = lax.rsqrt(jnp.mean(jnp.square(o_ref[...]), axis=-1, keepdims=True) + EPS)
        o_ref[...] = (o_ref[...] * r) * fg_ref[...]


def _ffn(xn2, w_up_bf, w_gate_half_bf, w_ffconv, b_ffconv, w_down_bf, h1, mod3, final_g, seq,
         tm=1024, tf=768, sub_width=256):
    t, d = xn2.shape
    dff = w_up_bf.shape[1]
    nf = pl.cdiv(dff, tf)
    tps = seq // tm
    assert nf >= 2 and tf % sub_width == 0 and dff % sub_width == 0
    kern = functools.partial(_ffn_kernel, tiles_per_seq=tps, sub_width=sub_width,
                             last_width=dff - (nf - 1) * tf)
    return pl.pallas_call(
        kern,
        grid=(t // tm, nf),
        in_specs=[pl.BlockSpec((tm, d), lambda i, f: (i, 0)),
                  pl.BlockSpec((d, tf), lambda i, f: (0, f)),
                  pl.BlockSpec((d, tf), lambda i, f: (0, f)),
                  pl.BlockSpec((CONV_WIDTH, tf), lambda i, f: (0, f)),
                  pl.BlockSpec((1, tf), lambda i, f: (0, f)),
                  pl.BlockSpec((tf, d), lambda i, f: (f, 0)),
                  pl.BlockSpec(memory_space=pl.ANY),
                  pl.BlockSpec((None, 1, d), lambda i, f: (i // tps, 0, 5)),
                  pl.BlockSpec((1, d), lambda i, f: (0, 0))],
        out_specs=pl.BlockSpec((tm, d), lambda i, f: (i, 0)),
        out_shape=jax.ShapeDtypeStruct((t, d), F32),
        scratch_shapes=[pltpu.VMEM((nf, HALO_ROWS, tf), F32), pltpu.SemaphoreType.DMA(())],
        compiler_params=_params("arbitrary", "arbitrary"),
        name="ffn",
    )(xn2, w_up_bf, w_gate_half_bf, w_ffconv, b_ffconv.reshape(1, dff), w_down_bf, h1, mod3,
      final_g.reshape(1, d))


def kernel(x, c, positions, norm1_g, norm2_g, w_ada, b_ada, w_in, b_gate, w_sc, w_ret_o, w_conv_o,
           w_mix_o, w_up, w_gate, w_ffconv, b_ffconv, w_down, final_g):
    batch, seq, d = x.shape
    depth = w_in.shape[0]
    assert depth == 1, "final-norm fusion assumes a single layer"
    assert d == D_RET and w_in.shape[2] == 4 * D_RET + 5 * d, (d, w_in.shape)
    assert seq % 2048 == 0 and seq % RET_CHUNK == 0 and batch % 2 == 0, (batch, seq)
    t = batch * seq
    h = x.reshape(t, d)
    for l in range(depth):
        mod, cos, sin = _ada_rope(c, w_ada[l], b_ada[l], positions)
        mod3 = mod.reshape(batch, 1, 6 * d)
        xn, cos, sin = _norm1(h, norm1_g[l], mod3, positions, cos, sin, seq)
        qkvg, w_ret_o_bf, w_conv_o_bf, w_mix_o_half_bf = _inproj_a(
            xn, w_in[l], cos.reshape(t, ROT_HALF), sin.reshape(t, ROT_HALF),
            [w_ret_o[l], w_conv_o[l], w_mix_o[l]], [None, None, 0.5])
        yc, w_gate_half_bf = _inproj_c(xn, w_in[l], w_sc[l], w_gate[l], 0.5, seq)
        gates_x2, w_up_bf = _inproj_g(xn, w_in[l], b_gate[l], w_up[l])
        yr = _retention(qkvg, batch, seq)
        merged_x2, w_down_bf = _merge(yr, yc, gates_x2, w_ret_o_bf, w_conv_o_bf, w_down[l])
        h1, xn2 = _mixout(merged_x2, w_mix_o_half_bf, h, mod3, norm2_g[l], seq)
        h = _ffn(xn2, w_up_bf, w_gate_half_bf, w_ffconv[l], b_ffconv[l], w_down_bf, h1, mod3, final_g, seq)
    return h.reshape(batch, seq, d)
```

```python
import functools

import jax
import jax.numpy as jnp
from jax import lax
from jax.experimental import pallas as pl
from jax.experimental.pallas import tpu as pltpu

F32 = jnp.float32
BF16 = jnp.bfloat16

RET_HEADS = 8
RET_HEAD_DIM = 256
D_RET = RET_HEADS * RET_HEAD_DIM
ROT_HALF = RET_HEAD_DIM // 2
CONV_WIDTH = 3
ROPE_BASE = 10000.0
EPS = 1e-6
RET_CHUNK = 256
HALO_ROWS = 8

V7X_VMEM_LIMIT_BYTES = 56 * 1024 * 1024


def _params(*semantics):
    return pltpu.CompilerParams(dimension_semantics=semantics,
                                vmem_limit_bytes=V7X_VMEM_LIMIT_BYTES)


def _sigmoid_of_half(h):
    return 0.5 + 0.5 * jnp.tanh(h)


def _silu_of_half(h):
    return h + h * jnp.tanh(h)


def _silu(v):
    return _silu_of_half(0.5 * v)


def _mod_rmsnorm(src_ref, rows, gain, scale, shift):
    x = src_ref[rows, :]
    r = lax.rsqrt(jnp.mean(x * x, axis=-1, keepdims=True) + EPS)
    return (src_ref[rows, :] * r) * (gain * (1.0 + scale)) + shift


def _causal_conv3(u, prev, w):
    def taps(a):
        return w[0:1, :] * pltpu.roll(a, 2, 0) + w[1:2, :] * pltpu.roll(a, 1, 0) + w[2:3, :] * a

    head = taps(jnp.concatenate([prev, u[:HALO_ROWS, :]], axis=0))[HALO_ROWS:, :]
    return jnp.concatenate([head, taps(u)[HALO_ROWS:, :]], axis=0)


def _side_cast_specs(weights, steps, step_index):
    specs = [pl.BlockSpec((w.shape[0] // steps, w.shape[1]), lambda *g: (step_index(*g), 0)) for w in weights]
    shapes = [jax.ShapeDtypeStruct(w.shape, BF16) for w in weights]
    return specs, shapes


def _side_cast(in_refs, out_refs, scale=None):
    for src, dst in zip(in_refs, out_refs):
        w = src[...] if scale is None else src[...] * scale
        dst[...] = w.astype(BF16)


def _rope_angles(positions, tr):
    inv_freq = ROPE_BASE ** (-jnp.arange(ROT_HALF, dtype=F32) / ROT_HALF)
    return positions.reshape(positions.size // tr, 1, tr), inv_freq.reshape(ROT_HALF, 1)


def _rope_tables(pos_ref, inv_ref):
    ang_t = inv_ref[...] * pos_ref[...].astype(F32)
    return jnp.cos(ang_t).T, jnp.sin(ang_t).T


def _ada_rope_kernel(c_ref, w_ref, b_ref, pos_ref, inv_ref, mod_ref, cos_ref, sin_ref):
    @pl.when(pl.program_id(0) == 0)
    def _():
        mod_ref[...] = jnp.broadcast_to(b_ref[...], mod_ref.shape)

    ca = _silu(c_ref[...]).astype(BF16)
    mod_ref[...] += jnp.dot(ca, w_ref[...].astype(BF16), preferred_element_type=F32)
    cos, sin = _rope_tables(pos_ref, inv_ref)
    cos_ref[0] = cos
    sin_ref[0] = sin
    cos_ref[1] = jnp.zeros_like(cos)
    sin_ref[1] = jnp.zeros_like(sin)


def _ada_rope(c, w_ada, b_ada, positions, steps=8):
    b, d = c.shape
    n = w_ada.shape[1]
    half = positions.size // 2
    tk, tr = d // steps, half // steps
    pos, inv_freq = _rope_angles(positions, tr)
    rope_spec = pl.BlockSpec((2, tr, ROT_HALF), lambda j: (0, j, 0))
    return pl.pallas_call(
        _ada_rope_kernel,
        grid=(steps,),
        in_specs=[pl.BlockSpec((b, tk), lambda j: (0, j)),
                  pl.BlockSpec((tk, n), lambda j: (j, 0)),
                  pl.BlockSpec((1, n), lambda j: (0, 0)),
                  pl.BlockSpec((None, 1, tr), lambda j: (j, 0, 0)),
                  pl.BlockSpec((ROT_HALF, 1), lambda j: (0, 0))],
        out_specs=[pl.BlockSpec((b, n), lambda j: (0, 0)), rope_spec, rope_spec],
        out_shape=[jax.ShapeDtypeStruct((b, n), F32),
                   jax.ShapeDtypeStruct((2, half, ROT_HALF), F32),
                   jax.ShapeDtypeStruct((2, half, ROT_HALF), F32)],
        compiler_params=_params("arbitrary"),
        name="ada_rope",
    )(c, w_ada, b_ada.reshape(1, n), pos, inv_freq)


def _norm1_kernel(x_ref, g_ref, sc_ref, sh_ref, pos_ref, inv_ref, cos_in, sin_in, o_ref, cos_ref, sin_ref):
    del cos_in, sin_in
    o_ref[...] = _mod_rmsnorm(x_ref, slice(None), g_ref[...], sc_ref[...], sh_ref[...]).astype(BF16)
    cos_ref[...], sin_ref[...] = _rope_tables(pos_ref, inv_ref)


def _norm1(x2d, norm_g, mod3, positions, cos, sin, seq, tm=1024):
    t, d = x2d.shape
    tps = seq // tm
    steps = t // tm
    tr = cos.shape[1] // steps
    pos, inv_freq = _rope_angles(positions, tr)
    rope_spec = pl.BlockSpec((None, tr, ROT_HALF), lambda i: (1, i, 0))
    return pl.pallas_call(
        _norm1_kernel,
        grid=(steps,),
        in_specs=[pl.BlockSpec((tm, d), lambda i: (i, 0)),
                  pl.BlockSpec((1, d), lambda i: (0, 0)),
                  pl.BlockSpec((None, 1, d), lambda i: (i // tps, 0, 1)),
                  pl.BlockSpec((None, 1, d), lambda i: (i // tps, 0, 0)),
                  pl.BlockSpec((None, 1, tr), lambda i: (steps + i, 0, 0)),
                  pl.BlockSpec((ROT_HALF, 1), lambda i: (0, 0)),
                  pl.BlockSpec(memory_space=pl.ANY),
                  pl.BlockSpec(memory_space=pl.ANY)],
        out_specs=[pl.BlockSpec((tm, d), lambda i: (i, 0)), rope_spec, rope_spec],
        out_shape=[jax.ShapeDtypeStruct((t, d), BF16),
                   jax.ShapeDtypeStruct(cos.shape, F32), jax.ShapeDtypeStruct(sin.shape, F32)],
        input_output_aliases={6: 1, 7: 2},
        compiler_params=_params("arbitrary"),
        name="norm1",
    )(x2d, norm_g.reshape(1, d), mod3, mod3, pos, inv_freq, cos, sin)


def _inproj_a_kernel(xn_ref, w_ref, cos_ref, sin_ref, *rest, q_tiles, rot_tiles, gate_tile0, n_sub, n_side):
    side_in, (o_ref, *side_out, wbf_ref) = rest[:n_side], rest[n_side:]
    j = pl.program_id(0)

    @pl.when(pl.program_id(1) == 0)
    def _():
        scale = jnp.where(j >= gate_tile0, 0.5, 1.0).astype(F32)
        wbf_ref[...] = (w_ref[...] * scale).astype(BF16)

    _side_cast(side_in, side_out)
    heads = o_ref.shape[0]
    tr = xn_ref.shape[0] // n_sub

    def proj(rows):
        return jnp.dot(xn_ref[rows, :], wbf_ref[...], preferred_element_type=F32)

    @pl.when(j < rot_tiles)
    def _():
        is_q = j < q_tiles
        scale = jnp.where(is_q, RET_HEAD_DIM ** -0.5, 1.0).astype(F32)
        sign = jnp.where(is_q, 1.0, -1.0).astype(F32)
        first_head = (j % q_tiles) * heads
        in_chunk = (lax.broadcasted_iota(jnp.int32, (tr, ROT_HALF), 0) % RET_CHUNK).astype(F32)
        decays = [scale * jnp.exp((sign * _ret_log_gamma(first_head + hh, ROT_HALF)) * in_chunk)
                  for hh in range(heads)]
        for s in range(n_sub):
            rows = pl.ds(s * tr, tr)
            acc = proj(rows)
            for hh in range(heads):
                cos = cos_ref[rows, :] * decays[hh]
                sin = sin_ref[rows, :] * decays[hh]
                lo = hh * RET_HEAD_DIM
                x1 = acc[:, lo:lo + ROT_HALF]
                x2 = acc[:, lo + ROT_HALF:lo + RET_HEAD_DIM]
                o_ref[hh, rows, 0:ROT_HALF] = (x1 * cos - x2 * sin).astype(BF16)
                o_ref[hh, rows, ROT_HALF:RET_HEAD_DIM] = (x2 * cos + x1 * sin).astype(BF16)

    def plain(fn):
        for s in range(n_sub):
            rows = pl.ds(s * tr, tr)
            res = fn(proj(rows)).astype(BF16)
            for hh in range(heads):
                o_ref[hh, rows, :] = res[:, hh * RET_HEAD_DIM:(hh + 1) * RET_HEAD_DIM]

    pl.when(jnp.logical_and(j >= rot_tiles, j < gate_tile0))(functools.partial(plain, lambda a: a))
    pl.when(j >= gate_tile0)(functools.partial(plain, _silu_of_half))


def _inproj_a(xn, w_in, cos, sin, side_weights, tm=2048, tn=1024, n_sub=4):
    t, d = xn.shape
    n = 4 * D_RET
    assert (tm // n_sub) % RET_CHUNK == 0, "row sub-blocks must hold whole retention chunks"
    heads = tn // RET_HEAD_DIM
    n_i = t // tm
    side_specs, side_shapes = _side_cast_specs(side_weights, (n // tn) * n_i, lambda j, i: j * n_i + i)
    kern = functools.partial(_inproj_a_kernel, q_tiles=D_RET // tn, rot_tiles=2 * D_RET // tn,
                             gate_tile0=3 * D_RET // tn, n_sub=n_sub, n_side=len(side_weights))
    return pl.pallas_call(
        kern,
        grid=(n // tn, n_i),
        in_specs=[pl.BlockSpec((tm, d), lambda j, i: (i, 0)),
                  pl.BlockSpec((d, tn), lambda j, i: (0, j)),
                  pl.BlockSpec((tm, ROT_HALF), lambda j, i: (i, 0)),
                  pl.BlockSpec((tm, ROT_HALF), lambda j, i: (i, 0))] + side_specs,
        out_specs=[pl.BlockSpec((heads, tm, RET_HEAD_DIM), lambda j, i: (j, i, 0))] + side_specs,
        out_shape=[jax.ShapeDtypeStruct((n // RET_HEAD_DIM, t, RET_HEAD_DIM), BF16)] + side_shapes,
        scratch_shapes=[pltpu.VMEM((d, tn), BF16)],
        compiler_params=_params("arbitrary", "arbitrary"),
        name="inproj_a",
    )(xn, w_in, cos, sin, *side_weights)


def _inproj_c_kernel(xn_ref, wb_ref, wc_ref, wx_ref, wsc_ref, side_ref, o_ref, side_out_ref,
                     wb_bf, wc_bf, wx_bf, halo_ref, *, tiles_per_seq, side_scale):
    j = pl.program_id(0)
    i = pl.program_id(1)

    @pl.when(jnp.logical_and(i == 0, j == 0))
    def _():
        halo_ref[...] = jnp.zeros_like(halo_ref)

    @pl.when(i == 0)
    def _():
        wb_bf[...] = wb_ref[...].astype(BF16)
        wc_bf[...] = wc_ref[...].astype(BF16)
        wx_bf[...] = wx_ref[...].astype(BF16)

    _side_cast([side_ref], [side_out_ref], scale=side_scale)
    xn = xn_ref[...]
    cg = jnp.dot(xn, wc_bf[...], preferred_element_type=F32)
    xc = jnp.dot(xn, wx_bf[...], preferred_element_type=F32)
    u = cg * xc
    prev = jnp.where(i % tiles_per_seq == 0, 0.0, halo_ref[...])
    conv = _causal_conv3(u, prev, wsc_ref[...])
    halo_ref[...] = u[u.shape[0] - HALO_ROWS:, :]
    bg = jnp.dot(xn, wb_bf[...], preferred_element_type=F32)
    o_ref[...] = (bg * conv).astype(BF16)


def _inproj_c(xn, w_in, w_sc, side_weight, side_scale, seq, tm=1024, tc=512):
    t, d = xn.shape
    nc = d // tc
    n_i = t // tm
    first = 4 * D_RET // tc
    kern = functools.partial(_inproj_c_kernel, tiles_per_seq=seq // tm, side_scale=side_scale)
    wspec = lambda sec: pl.BlockSpec((d, tc), lambda j, i: (0, first + sec * nc + j))
    side_specs, side_shapes = _side_cast_specs([side_weight], nc * n_i, lambda j, i: j * n_i + i)
    return pl.pallas_call(
        kern,
        grid=(nc, n_i),
        in_specs=[pl.BlockSpec((tm, d), lambda j, i: (i, 0)),
                  wspec(0), wspec(1), wspec(2),
                  pl.BlockSpec((CONV_WIDTH, tc), lambda j, i: (0, j))] + side_specs,
        out_specs=[pl.BlockSpec((tm, tc), lambda j, i: (i, j))] + side_specs,
        out_shape=[jax.ShapeDtypeStruct((t, d), BF16)] + side_shapes,
        scratch_shapes=[pltpu.VMEM((d, tc), BF16)] * 3 + [pltpu.VMEM((HALO_ROWS, tc), F32)],
        compiler_params=_params("arbitrary", "arbitrary"),
        name="inproj_c",
    )(xn, w_in, w_in, w_in, w_sc, side_weight)


def _inproj_g_kernel(xn_ref, w_ref, b_ref, side_ref, o_ref, side_out_ref, wbf_ref, *, n_sub):
    @pl.when(pl.program_id(1) == 0)
    def _():
        wbf_ref[...] = (w_ref[...] * 0.5).astype(BF16)

    _side_cast([side_ref], [side_out_ref])
    tr = xn_ref.shape[0] // n_sub
    half_bias = 0.5 * b_ref[...]
    for s in range(n_sub):
        rows = pl.ds(s * tr, tr)
        half_logits = jnp.dot(xn_ref[rows, :], wbf_ref[...], preferred_element_type=F32) + half_bias
        o_ref[rows, :] = _sigmoid_of_half(half_logits).astype(BF16)


def _inproj_g(xn, w_in, b_gate, side_weight, tm=2048, tn=1024, n_sub=4):
    t, d = xn.shape
    n = b_gate.shape[0]
    n_i = t // tm
    first = (w_in.shape[1] - n) // tn
    side_specs, side_shapes = _side_cast_specs([side_weight], (n // tn) * n_i, lambda j, i: j * n_i + i)
    return pl.pallas_call(
        functools.partial(_inproj_g_kernel, n_sub=n_sub),
        grid=(n // tn, n_i),
        in_specs=[pl.BlockSpec((tm, d), lambda j, i: (i, 0)),
                  pl.BlockSpec((d, tn), lambda j, i: (0, first + j)),
                  pl.BlockSpec((1, tn), lambda j, i: (0, j))] + side_specs,
        out_specs=[pl.BlockSpec((tm, tn), lambda j, i: (i, j))] + side_specs,
        out_shape=[jax.ShapeDtypeStruct((t, n), BF16)] + side_shapes,
        scratch_shapes=[pltpu.VMEM((d, tn), BF16)],
        compiler_params=_params("arbitrary", "arbitrary"),
        name="inproj_g",
    )(xn, w_in, b_gate.reshape(1, n), side_weight)


def _ret_kernel(q_ref, k_ref, v_ref, gs_ref, o_ref):
    heads = q_ref.shape[0]
    first = pl.program_id(1) * heads

    def prepare(hh):
        return _ret_scores_and_states(first + hh, q_ref.at[hh], k_ref.at[hh], v_ref.at[hh])

    prep = prepare(0)
    for hh in range(heads):
        next_prep = prepare(hh + 1) if hh + 1 < heads else None
        cols = pl.ds(hh * RET_HEAD_DIM, RET_HEAD_DIM)
        _ret_outputs(prep, q_ref.at[hh], v_ref.at[hh], gs_ref.at[hh], o_ref.at[:, cols])
        prep = next_prep


def _ret_log_gamma(head_index, width):
    head = jnp.full((1, width), head_index, jnp.int32).astype(F32)
    return jnp.log(1.0 - jnp.exp2(-5.0 - head))


def _ret_scores_and_states(head_index, q_ref, k_ref, v_ref):
    c = RET_CHUNK
    n_chunks = q_ref.shape[0] // c
    chunk_dec = jnp.exp(_ret_log_gamma(head_index, RET_HEAD_DIM) * float(c))
    causal = (lax.broadcasted_iota(jnp.int32, (c, c), 0) >= lax.broadcasted_iota(jnp.int32, (c, c), 1))
    chunks = [pl.ds(n * c, c) for n in range(n_chunks)]

    kvs = [lax.dot_general(k_ref[rows, :], v_ref[rows, :], (((0,), (0,)), ((), ())),
                           preferred_element_type=F32) for rows in chunks[:-1]]
    scores = [lax.dot_general(q_ref[rows, :], k_ref[rows, :], (((1,), (1,)), ((), ())),
                              preferred_element_type=F32) for rows in chunks]
    states = [None]
    for kv in kvs:
        states.append(chunk_dec * (kv if states[-1] is None else states[-1] + kv))
    return chunks, scores, states, causal


def _ret_outputs(prep, q_ref, v_ref, gs_ref, o_ref):
    chunks, scores, states, causal = prep
    for rows, s, state in zip(chunks, scores, states):
        y = jnp.dot(jnp.where(causal, s, 0.0).astype(BF16), v_ref[rows, :], preferred_element_type=F32)
        if state is not None:
            y = y + jnp.dot(q_ref[rows, :], state.astype(BF16), preferred_element_type=F32)
        yn = y * lax.rsqrt(jnp.mean(y * y, axis=-1, keepdims=True) + EPS)
        o_ref[rows, :] = yn.astype(BF16) * gs_ref[rows, :]


def _retention(qkvg, batch, seq, heads_per_step=4):
    t = qkvg.shape[1]
    groups = RET_HEADS // heads_per_step
    spec = lambda sec: pl.BlockSpec((heads_per_step, seq, RET_HEAD_DIM),
                                    lambda b, g: (sec * groups + g, b, 0))
    return pl.pallas_call(
        _ret_kernel,
        grid=(batch, groups),
        in_specs=[spec(0), spec(1), spec(2), spec(3)],
        out_specs=pl.BlockSpec((seq, heads_per_step * RET_HEAD_DIM), lambda b, g: (b, g)),
        out_shape=jax.ShapeDtypeStruct((t, D_RET), BF16),
        compiler_params=_params("arbitrary", "arbitrary"),
        name="ret",
    )(qkvg, qkvg, qkvg, qkvg)


def _merge_kernel(yr_ref, yc_ref, ga_ref, gb_ref, wr_ref, wc_ref, side_ref, o_ref, side_out_ref):
    _side_cast([side_ref], [side_out_ref])
    ya = jnp.dot(yr_ref[...], wr_ref[...], preferred_element_type=F32)
    yb = jnp.dot(yc_ref[...], wc_ref[...], preferred_element_type=F32)
    o_ref[...] = (ga_ref[...].astype(F32) * ya + gb_ref[...].astype(F32) * yb).astype(BF16)


def _merge(yr, yc, gates, w_ret_o_bf, w_conv_o_bf, side_weight, tm=512):
    t, d = yr.shape
    resident = lambda: pl.BlockSpec((d, d), lambda i: (0, 0), pipeline_mode=pl.Buffered(1))
    side_specs, side_shapes = _side_cast_specs([side_weight], t // tm, lambda i: i)
    return pl.pallas_call(
        _merge_kernel,
        grid=(t // tm,),
        in_specs=[pl.BlockSpec((tm, d), lambda i: (i, 0)),
                  pl.BlockSpec((tm, d), lambda i: (i, 0)),
                  pl.BlockSpec((tm, d), lambda i: (i, 0)),
                  pl.BlockSpec((tm, d), lambda i: (i, 1)),
                  resident(), resident()] + side_specs,
        out_specs=[pl.BlockSpec((tm, d), lambda i: (i, 0))] + side_specs,
        out_shape=[jax.ShapeDtypeStruct((t, d), BF16)] + side_shapes,
        compiler_params=_params("arbitrary"),
        name="merge",
    )(yr, yc, gates, gates, w_ret_o_bf, w_conv_o_bf, side_weight)


def _mixout_kernel(m_ref, w_ref, x_ref, g1_ref, g_ref, sc_ref, sh_ref, h_ref, xn_ref, *, n_sub):
    tr = m_ref.shape[0] // n_sub
    for s in range(n_sub):
        rows = pl.ds(s * tr, tr)
        mixed = jnp.dot(m_ref[rows, :], w_ref[...], preferred_element_type=F32)
        h = x_ref[rows, :] + g1_ref[...] * mixed
        h_ref[rows, :] = h
        xn_ref[rows, :] = _mod_rmsnorm(h_ref, rows, g_ref[...], sc_ref[...], sh_ref[...]).astype(BF16)


def _mixout(m, w_mix_o_bf, x2d, mod3, norm_g, seq, tm=512, n_sub=2):
    t, d = x2d.shape
    tps = seq // tm
    mod_spec = lambda k: pl.BlockSpec((None, 1, d), lambda i: (i // tps, 0, k))
    return pl.pallas_call(
        functools.partial(_mixout_kernel, n_sub=n_sub),
        grid=(t // tm,),
        in_specs=[pl.BlockSpec((tm, d), lambda i: (i, 0)),
                  pl.BlockSpec((d, d), lambda i: (0, 0), pipeline_mode=pl.Buffered(1)),
                  pl.BlockSpec((tm, d), lambda i: (i, 0)),
                  mod_spec(2), pl.BlockSpec((1, d), lambda i: (0, 0)), mod_spec(4), mod_spec(3)],
        out_specs=[pl.BlockSpec((tm, d), lambda i: (i, 0)), pl.BlockSpec((tm, d), lambda i: (i, 0))],
        out_shape=[jax.ShapeDtypeStruct((t, d), F32), jax.ShapeDtypeStruct((t, d), BF16)],
        compiler_params=_params("arbitrary"),
        name="mixout",
    )(m, w_mix_o_bf, x2d, mod3, norm_g.reshape(1, d), mod3, mod3)


def _ffn_kernel(xn_ref, wu_ref, wg_ref, wfc_ref, bfc_ref, wd_ref, h_hbm, g2_ref, fg_ref,
                o_ref, halo_ref, h_sem, *, tiles_per_seq, sub_width, last_width):
    i = pl.program_id(0)
    f = pl.program_id(1)
    last = pl.num_programs(1) - 1
    tm = o_ref.shape[0]

    @pl.when(jnp.logical_and(i == 0, f == 0))
    def _():
        halo_ref[...] = jnp.zeros_like(halo_ref)

    def h1_copy():
        return pltpu.make_async_copy(h_hbm.at[pl.ds(i * tm, tm), :], o_ref, h_sem)

    def step(first, width):
        if first:
            h1_copy().start()
        xn = xn_ref[...]
        not_seq_start = i % tiles_per_seq != 0
        col_blocks = [pl.ds(s * sub_width, sub_width) for s in range(width // sub_width)]
        up_gate = [(jnp.dot(xn, wu_ref[:, cols], preferred_element_type=F32),
                    jnp.dot(xn, wg_ref[:, cols], preferred_element_type=F32)) for cols in col_blocks]
        acts = []
        for cols, (up, gpre) in zip(col_blocks, up_gate):
            prev = jnp.where(not_seq_start, halo_ref[f, :, cols], 0.0)
            half_gt = _causal_conv3(gpre, prev, wfc_ref[:, cols]) + 0.5 * bfc_ref[:, cols]
            halo_ref[f, :, cols] = gpre[gpre.shape[0] - HALO_ROWS:, :]
            acts.append((_silu_of_half(half_gt) * up).astype(BF16))
        n_head = len(acts) - 1
        head = None
        if n_head:
            head = jnp.dot(jnp.concatenate(acts[:-1], axis=1), wd_ref[pl.ds(0, n_head * sub_width), :],
                           preferred_element_type=F32)
        part = jnp.dot(acts[-1], wd_ref[col_blocks[-1], :], preferred_element_type=F32)
        if head is not None:
            part = head + part
        if first:
            h1_copy().wait()
        o_ref[...] += g2_ref[...] * part

    tile_width = wu_ref.shape[1]
    pl.when(f == 0)(functools.partial(step, True, tile_width))
    pl.when(jnp.logical_and(f > 0, f < last))(functools.partial(step, False, tile_width))
    pl.when(f == last)(functools.partial(step, False, last_width))

    @pl.when(f == last)
    def _():
        r = lax.rsqrt(jnp.mean(jnp.square(o_ref[...]), axis=-1, keepdims=True) + EPS)
        o_ref[...] = (o_ref[...] * r) * fg_ref[...]


def _ffn(xn2, w_up_bf, w_gate_half_bf, w_ffconv, b_ffconv, w_down_bf, h1, mod3, final_g, seq,
         tm=1024, tf=768, sub_width=256):
    t, d = xn2.shape
    dff = w_up_bf.shape[1]
    nf = pl.cdiv(dff, tf)
    tps = seq // tm
    assert nf >= 2 and tf % sub_width == 0 and dff % sub_width == 0
    kern = functools.partial(_ffn_kernel, tiles_per_seq=tps, sub_width=sub_width,
                             last_width=dff - (nf - 1) * tf)
    return pl.pallas_call(
        kern,
        grid=(t // tm, nf),
        in_specs=[pl.BlockSpec((tm, d), lambda i, f: (i, 0)),
                  pl.BlockSpec((d, tf), lambda i, f: (0, f)),
                  pl.BlockSpec((d, tf), lambda i, f: (0, f)),
                  pl.BlockSpec((CONV_WIDTH, tf), lambda i, f: (0, f)),
                  pl.BlockSpec((1, tf), lambda i, f: (0, f)),
                  pl.BlockSpec((tf, d), lambda i, f: (f, 0)),
                  pl.BlockSpec(memory_space=pl.ANY),
                  pl.BlockSpec((None, 1, d), lambda i, f: (i // tps, 0, 5)),
                  pl.BlockSpec((1, d), lambda i, f: (0, 0))],
        out_specs=pl.BlockSpec((tm, d), lambda i, f: (i, 0)),
        out_shape=jax.ShapeDtypeStruct((t, d), F32),
        scratch_shapes=[pltpu.VMEM((nf, HALO_ROWS, tf), F32), pltpu.SemaphoreType.DMA(())],
        compiler_params=_params("arbitrary", "arbitrary"),
        name="ffn",
    )(xn2, w_up_bf, w_gate_half_bf, w_ffconv, b_ffconv.reshape(1, dff), w_down_bf, h1, mod3,
      final_g.reshape(1, d))


def kernel(x, c, positions, norm1_g, norm2_g, w_ada, b_ada, w_in, b_gate, w_sc, w_ret_o, w_conv_o,
           w_mix_o, w_up, w_gate, w_ffconv, b_ffconv, w_down, final_g):
    batch, seq, d = x.shape
    depth = w_in.shape[0]
    assert depth == 1, "final-norm fusion assumes a single layer"
    assert d == D_RET and w_in.shape[2] == 4 * D_RET + 5 * d, (d, w_in.shape)
    assert seq % 2048 == 0 and seq % RET_CHUNK == 0 and batch % 2 == 0, (batch, seq)
    t = batch * seq
    h = x.reshape(t, d)
    for l in range(depth):
        mod, cos, sin = _ada_rope(c, w_ada[l], b_ada[l], positions)
        mod3 = mod.reshape(batch, 1, 6 * d)
        xn, cos, sin = _norm1(h, norm1_g[l], mod3, positions, cos, sin, seq)
        qkvg, w_ret_o_bf, w_conv_o_bf, w_mix_o_bf = _inproj_a(
            xn, w_in[l], cos.reshape(t, ROT_HALF), sin.reshape(t, ROT_HALF),
            [w_ret_o[l], w_conv_o[l], w_mix_o[l]])
        yc, w_gate_half_bf = _inproj_c(xn, w_in[l], w_sc[l], w_gate[l], 0.5, seq)
        gates, w_up_bf = _inproj_g(xn, w_in[l], b_gate[l], w_up[l])
        yr = _retention(qkvg, batch, seq)
        merged, w_down_bf = _merge(yr, yc, gates, w_ret_o_bf, w_conv_o_bf, w_down[l])
        h1, xn2 = _mixout(merged, w_mix_o_bf, h, mod3, norm2_g[l], seq)
        h = _ffn(xn2, w_up_bf, w_gate_half_bf, w_ffconv[l], b_ffconv[l], w_down_bf, h1, mod3, final_g, seq)
    return h.reshape(batch, seq, d)
```

```python
import functools

import jax
import jax.numpy as jnp
from jax import lax
from jax.experimental import pallas as pl
from jax.experimental.pallas import tpu as pltpu

F32 = jnp.float32
BF16 = jnp.bfloat16

RET_HEADS = 8
RET_HEAD_DIM = 256
D_RET = RET_HEADS * RET_HEAD_DIM
ROT_HALF = RET_HEAD_DIM // 2
CONV_WIDTH = 3
ROPE_BASE = 10000.0
EPS = 1e-6
RET_CHUNK = 256
HALO_ROWS = 8

V7X_VMEM_LIMIT_BYTES = 56 * 1024 * 1024


def _params(*semantics):
    return pltpu.CompilerParams(dimension_semantics=semantics,
                                vmem_limit_bytes=V7X_VMEM_LIMIT_BYTES)


def _sigmoid_of_half(h):
    return 0.5 + 0.5 * jnp.tanh(h)


def _silu_of_half(h):
    return h + h * jnp.tanh(h)


def _silu(v):
    return _silu_of_half(0.5 * v)


def _mod_rmsnorm(src_ref, rows, gain, scale, shift):
    x = src_ref[rows, :]
    r = lax.rsqrt(jnp.mean(x * x, axis=-1, keepdims=True) + EPS)
    return (src_ref[rows, :] * r) * (gain * (1.0 + scale)) + shift


def _causal_conv3(u, prev, w):
    def taps(a):
        return w[0:1, :] * pltpu.roll(a, 2, 0) + w[1:2, :] * pltpu.roll(a, 1, 0) + w[2:3, :] * a

    head = taps(jnp.concatenate([prev, u[:HALO_ROWS, :]], axis=0))[HALO_ROWS:, :]
    return jnp.concatenate([head, taps(u)[HALO_ROWS:, :]], axis=0)


def _side_cast_specs(weights, steps, step_index):
    specs = [pl.BlockSpec((w.shape[0] // steps, w.shape[1]), lambda *g: (step_index(*g), 0)) for w in weights]
    shapes = [jax.ShapeDtypeStruct(w.shape, BF16) for w in weights]
    return specs, shapes


def _side_cast(in_refs, out_refs, scale=None):
    for src, dst in zip(in_refs, out_refs):
        w = src[...] if scale is None else src[...] * scale
        dst[...] = w.astype(BF16)


def _rope_angles(positions, tr):
    inv_freq = ROPE_BASE ** (-jnp.arange(ROT_HALF, dtype=F32) / ROT_HALF)
    return positions.reshape(positions.size // tr, 1, tr), inv_freq.reshape(ROT_HALF, 1)


def _rope_tables(pos_ref, inv_ref):
    ang_t = inv_ref[...] * pos_ref[...].astype(F32)
    return jnp.cos(ang_t).T, jnp.sin(ang_t).T


def _ada_rope_kernel(c_ref, w_ref, b_ref, pos_ref, inv_ref, mod_ref, cos_ref, sin_ref):
    @pl.when(pl.program_id(0) == 0)
    def _():
        mod_ref[...] = jnp.broadcast_to(b_ref[...], mod_ref.shape)

    ca = _silu(c_ref[...]).astype(BF16)
    mod_ref[...] += jnp.dot(ca, w_ref[...].astype(BF16), preferred_element_type=F32)
    cos, sin = _rope_tables(pos_ref, inv_ref)
    cos_ref[0] = cos
    sin_ref[0] = sin
    cos_ref[1] = jnp.zeros_like(cos)
    sin_ref[1] = jnp.zeros_like(sin)


def _ada_rope(c, w_ada, b_ada, positions, steps=8):
    b, d = c.shape
    n = w_ada.shape[1]
    half = positions.size // 2
    tk, tr = d // steps, half // steps
    pos, inv_freq = _rope_angles(positions, tr)
    rope_spec = pl.BlockSpec((2, tr, ROT_HALF), lambda j: (0, j, 0))
    return pl.pallas_call(
        _ada_rope_kernel,
        grid=(steps,),
        in_specs=[pl.BlockSpec((b, tk), lambda j: (0, j)),
                  pl.BlockSpec((tk, n), lambda j: (j, 0)),
                  pl.BlockSpec((1, n), lambda j: (0, 0)),
                  pl.BlockSpec((None, 1, tr), lambda j: (j, 0, 0)),
                  pl.BlockSpec((ROT_HALF, 1), lambda j: (0, 0))],
        out_specs=[pl.BlockSpec((b, n), lambda j: (0, 0)), rope_spec, rope_spec],
        out_shape=[jax.ShapeDtypeStruct((b, n), F32),
                   jax.ShapeDtypeStruct((2, half, ROT_HALF), F32),
                   jax.ShapeDtypeStruct((2, half, ROT_HALF), F32)],
        compiler_params=_params("arbitrary"),
        name="ada_rope",
    )(c, w_ada, b_ada.reshape(1, n), pos, inv_freq)


def _norm1_kernel(x_ref, g_ref, sc_ref, sh_ref, pos_ref, inv_ref, cos_in, sin_in, o_ref, cos_ref, sin_ref):
    del cos_in, sin_in
    o_ref[...] = _mod_rmsnorm(x_ref, slice(None), g_ref[...], sc_ref[...], sh_ref[...]).astype(BF16)
    cos_ref[...], sin_ref[...] = _rope_tables(pos_ref, inv_ref)


def _norm1(x2d, norm_g, mod3, positions, cos, sin, seq, tm=1024):
    t, d = x2d.shape
    tps = seq // tm
    steps = t // tm
    tr = cos.shape[1] // steps
    pos, inv_freq = _rope_angles(positions, tr)
    rope_spec = pl.BlockSpec((None, tr, ROT_HALF), lambda i: (1, i, 0))
    return pl.pallas_call(
        _norm1_kernel,
        grid=(steps,),
        in_specs=[pl.BlockSpec((tm, d), lambda i: (i, 0)),
                  pl.BlockSpec((1, d), lambda i: (0, 0)),
                  pl.BlockSpec((None, 1, d), lambda i: (i // tps, 0, 1)),
                  pl.BlockSpec((None, 1, d), lambda i: (i // tps, 0, 0)),
                  pl.BlockSpec((None, 1, tr), lambda i: (steps + i, 0, 0)),
                  pl.BlockSpec((ROT_HALF, 1), lambda i: (0, 0)),
                  pl.BlockSpec(memory_space=pl.ANY),
                  pl.BlockSpec(memory_space=pl.ANY)],
        out_specs=[pl.BlockSpec((tm, d), lambda i: (i, 0)), rope_spec, rope_spec],
        out_shape=[jax.ShapeDtypeStruct((t, d), BF16),
                   jax.ShapeDtypeStruct(cos.shape, F32), jax.ShapeDtypeStruct(sin.shape, F32)],
        input_output_aliases={6: 1, 7: 2},
        compiler_params=_params("arbitrary"),
        name="norm1",
    )(x2d, norm_g.reshape(1, d), mod3, mod3, pos, inv_freq, cos, sin)


def _inproj_a_kernel(xn_ref, w_ref, cos_ref, sin_ref, *rest, q_tiles, rot_tiles, gate_tile0, n_sub, n_side):
    side_in, (o_ref, *side_out, wbf_ref) = rest[:n_side], rest[n_side:]
    j = pl.program_id(0)

    @pl.when(pl.program_id(1) == 0)
    def _():
        scale = jnp.where(j >= gate_tile0, 0.5, 1.0).astype(F32)
        wbf_ref[...] = (w_ref[...] * scale).astype(BF16)

    _side_cast(side_in, side_out)
    heads = o_ref.shape[0]
    tr = xn_ref.shape[0] // n_sub

    def proj(rows):
        return jnp.dot(xn_ref[rows, :], wbf_ref[...], preferred_element_type=F32)

    @pl.when(j < rot_tiles)
    def _():
        scale = jnp.where(j < q_tiles, RET_HEAD_DIM ** -0.5, 1.0).astype(F32)
        for s in range(n_sub):
            rows = pl.ds(s * tr, tr)
            acc = proj(rows)
            cos = cos_ref[rows, :] * scale
            sin = sin_ref[rows, :] * scale
            for hh in range(heads):
                lo = hh * RET_HEAD_DIM
                x1 = acc[:, lo:lo + ROT_HALF]
                x2 = acc[:, lo + ROT_HALF:lo + RET_HEAD_DIM]
                o_ref[hh, rows, 0:ROT_HALF] = (x1 * cos - x2 * sin).astype(BF16)
                o_ref[hh, rows, ROT_HALF:RET_HEAD_DIM] = (x2 * cos + x1 * sin).astype(BF16)

    def plain(fn):
        for s in range(n_sub):
            rows = pl.ds(s * tr, tr)
            res = fn(proj(rows)).astype(BF16)
            for hh in range(heads):
                o_ref[hh, rows, :] = res[:, hh * RET_HEAD_DIM:(hh + 1) * RET_HEAD_DIM]

    pl.when(jnp.logical_and(j >= rot_tiles, j < gate_tile0))(functools.partial(plain, lambda a: a))
    pl.when(j >= gate_tile0)(functools.partial(plain, _silu_of_half))


def _inproj_a(xn, w_in, cos, sin, side_weights, tm=2048, tn=1024, n_sub=4):
    t, d = xn.shape
    n = 4 * D_RET
    heads = tn // RET_HEAD_DIM
    n_i = t // tm
    side_specs, side_shapes = _side_cast_specs(side_weights, (n // tn) * n_i, lambda j, i: j * n_i + i)
    kern = functools.partial(_inproj_a_kernel, q_tiles=D_RET // tn, rot_tiles=2 * D_RET // tn,
                             gate_tile0=3 * D_RET // tn, n_sub=n_sub, n_side=len(side_weights))
    return pl.pallas_call(
        kern,
        grid=(n // tn, n_i),
        in_specs=[pl.BlockSpec((tm, d), lambda j, i: (i, 0)),
                  pl.BlockSpec((d, tn), lambda j, i: (0, j)),
                  pl.BlockSpec((tm, ROT_HALF), lambda j, i: (i, 0)),
                  pl.BlockSpec((tm, ROT_HALF), lambda j, i: (i, 0))] + side_specs,
        out_specs=[pl.BlockSpec((heads, tm, RET_HEAD_DIM), lambda j, i: (j, i, 0))] + side_specs,
        out_shape=[jax.ShapeDtypeStruct((n // RET_HEAD_DIM, t, RET_HEAD_DIM), BF16)] + side_shapes,
        scratch_shapes=[pltpu.VMEM((d, tn), BF16)],
        compiler_params=_params("arbitrary", "arbitrary"),
        name="inproj_a",
    )(xn, w_in, cos, sin, *side_weights)


def _inproj_c_kernel(xn_ref, wb_ref, wc_ref, wx_ref, wsc_ref, side_ref, o_ref, side_out_ref,
                     wb_bf, wc_bf, wx_bf, halo_ref, *, tiles_per_seq, side_scale):
    j = pl.program_id(0)
    i = pl.program_id(1)

    @pl.when(jnp.logical_and(i == 0, j == 0))
    def _():
        halo_ref[...] = jnp.zeros_like(halo_ref)

    @pl.when(i == 0)
    def _():
        wb_bf[...] = wb_ref[...].astype(BF16)
        wc_bf[...] = wc_ref[...].astype(BF16)
        wx_bf[...] = wx_ref[...].astype(BF16)

    _side_cast([side_ref], [side_out_ref], scale=side_scale)
    xn = xn_ref[...]
    cg = jnp.dot(xn, wc_bf[...], preferred_element_type=F32)
    xc = jnp.dot(xn, wx_bf[...], preferred_element_type=F32)
    u = cg * xc
    prev = jnp.where(i % tiles_per_seq == 0, 0.0, halo_ref[...])
    conv = _causal_conv3(u, prev, wsc_ref[...])
    halo_ref[...] = u[u.shape[0] - HALO_ROWS:, :]
    bg = jnp.dot(xn, wb_bf[...], preferred_element_type=F32)
    o_ref[...] = (bg * conv).astype(BF16)


def _inproj_c(xn, w_in, w_sc, side_weight, side_scale, seq, tm=1024, tc=512):
    t, d = xn.shape
    nc = d // tc
    n_i = t // tm
    first = 4 * D_RET // tc
    kern = functools.partial(_inproj_c_kernel, tiles_per_seq=seq // tm, side_scale=side_scale)
    wspec = lambda sec: pl.BlockSpec((d, tc), lambda j, i: (0, first + sec * nc + j))
    side_specs, side_shapes = _side_cast_specs([side_weight], nc * n_i, lambda j, i: j * n_i + i)
    return pl.pallas_call(
        kern,
        grid=(nc, n_i),
        in_specs=[pl.BlockSpec((tm, d), lambda j, i: (i, 0)),
                  wspec(0), wspec(1), wspec(2),
                  pl.BlockSpec((CONV_WIDTH, tc), lambda j, i: (0, j))] + side_specs,
        out_specs=[pl.BlockSpec((tm, tc), lambda j, i: (i, j))] + side_specs,
        out_shape=[jax.ShapeDtypeStruct((t, d), BF16)] + side_shapes,
        scratch_shapes=[pltpu.VMEM((d, tc), BF16)] * 3 + [pltpu.VMEM((HALO_ROWS, tc), F32)],
        compiler_params=_params("arbitrary", "arbitrary"),
        name="inproj_c",
    )(xn, w_in, w_in, w_in, w_sc, side_weight)


def _inproj_g_kernel(xn_ref, w_ref, b_ref, side_ref, o_ref, side_out_ref, wbf_ref, *, n_sub):
    @pl.when(pl.program_id(1) == 0)
    def _():
        wbf_ref[...] = (w_ref[...] * 0.5).astype(BF16)

    _side_cast([side_ref], [side_out_ref])
    tr = xn_ref.shape[0] // n_sub
    half_bias = 0.5 * b_ref[...]
    for s in range(n_sub):
        rows = pl.ds(s * tr, tr)
        half_logits = jnp.dot(xn_ref[rows, :], wbf_ref[...], preferred_element_type=F32) + half_bias
        o_ref[rows, :] = _sigmoid_of_half(half_logits).astype(BF16)


def _inproj_g(xn, w_in, b_gate, side_weight, tm=2048, tn=1024, n_sub=4):
    t, d = xn.shape
    n = b_gate.shape[0]
    n_i = t // tm
    first = (w_in.shape[1] - n) // tn
    side_specs, side_shapes = _side_cast_specs([side_weight], (n // tn) * n_i, lambda j, i: j * n_i + i)
    return pl.pallas_call(
        functools.partial(_inproj_g_kernel, n_sub=n_sub),
        grid=(n // tn, n_i),
        in_specs=[pl.BlockSpec((tm, d), lambda j, i: (i, 0)),
                  pl.BlockSpec((d, tn), lambda j, i: (0, first + j)),
                  pl.BlockSpec((1, tn), lambda j, i: (0, j))] + side_specs,
        out_specs=[pl.BlockSpec((tm, tn), lambda j, i: (i, j))] + side_specs,
        out_shape=[jax.ShapeDtypeStruct((t, n), BF16)] + side_shapes,
        scratch_shapes=[pltpu.VMEM((d, tn), BF16)],
        compiler_params=_params("arbitrary", "arbitrary"),
        name="inproj_g",
    )(xn, w_in, b_gate.reshape(1, n), side_weight)


def _ret_kernel(q_ref, k_ref, v_ref, o_ref):
    heads = q_ref.shape[0]
    first = pl.program_id(1) * heads

    def prepare(hh):
        return _ret_scores_and_states(first + hh, q_ref.at[hh], k_ref.at[hh], v_ref.at[hh])

    prep = prepare(0)
    for hh in range(heads):
        next_prep = prepare(hh + 1) if hh + 1 < heads else None
        cols = pl.ds(hh * RET_HEAD_DIM, RET_HEAD_DIM)
        _ret_outputs(prep, q_ref.at[hh], v_ref.at[hh], o_ref.at[:, cols])
        prep = next_prep


def _ret_scores_and_states(head_index, q_ref, k_ref, v_ref):
    c = RET_CHUNK
    n_chunks = q_ref.shape[0] // c
    head = jnp.full((1, RET_HEAD_DIM), head_index, jnp.int32).astype(F32)
    log_gamma = jnp.log(1.0 - jnp.exp2(-5.0 - head))
    row = lax.broadcasted_iota(jnp.int32, (c, c), 0).astype(F32)
    col = lax.broadcasted_iota(jnp.int32, (c, c), 1).astype(F32)
    diff = row - col
    decay = jnp.where(diff >= 0, jnp.exp(log_gamma * jnp.maximum(diff, 0.0)), 0.0)
    q_dec = jnp.exp(log_gamma * (row + 1.0))
    k_dec = jnp.exp(log_gamma * (c - 1.0 - row))
    chunk_dec = jnp.exp(log_gamma * float(c))
    chunks = [pl.ds(n * c, c) for n in range(n_chunks)]

    kvs = []
    for rows in chunks[:-1]:
        kd = (k_ref[rows, :].astype(F32) * k_dec).astype(BF16)
        kvs.append(lax.dot_general(kd, v_ref[rows, :], (((0,), (0,)), ((), ())),
                                   preferred_element_type=F32))
    scores = [lax.dot_general(q_ref[rows, :], k_ref[rows, :], (((1,), (1,)), ((), ())),
                              preferred_element_type=F32) for rows in chunks]
    states = [None]
    for kv in kvs:
        states.append(kv if states[-1] is None else chunk_dec * states[-1] + kv)
    return chunks, scores, states, decay, q_dec


def _ret_outputs(prep, q_ref, v_ref, o_ref):
    chunks, scores, states, decay, q_dec = prep
    for rows, s, state in zip(chunks, scores, states):
        y = jnp.dot((s * decay).astype(BF16), v_ref[rows, :], preferred_element_type=F32)
        if state is not None:
            y = y + q_dec * jnp.dot(q_ref[rows, :], state.astype(BF16), preferred_element_type=F32)
        yn = y * lax.rsqrt(jnp.mean(y * y, axis=-1, keepdims=True) + EPS)
        o_ref[rows, :] = yn.astype(BF16)


def _retention(qkvg, batch, seq, heads_per_step=4):
    t = qkvg.shape[1]
    groups = RET_HEADS // heads_per_step
    spec = lambda sec: pl.BlockSpec((heads_per_step, seq, RET_HEAD_DIM),
                                    lambda b, g: (sec * groups + g, b, 0))
    return pl.pallas_call(
        _ret_kernel,
        grid=(batch, groups),
        in_specs=[spec(0), spec(1), spec(2)],
        out_specs=pl.BlockSpec((seq, heads_per_step * RET_HEAD_DIM), lambda b, g: (b, g)),
        out_shape=jax.ShapeDtypeStruct((t, D_RET), BF16),
        compiler_params=_params("arbitrary", "arbitrary"),
        name="ret",
    )(qkvg, qkvg, qkvg)


def _merge_kernel(yr_ref, yc_ref, ga_ref, gb_ref, wr_ref, wc_ref, side_ref, *rest):
    gs_refs, (o_ref, side_out_ref) = rest[:RET_HEADS], rest[RET_HEADS:]
    _side_cast([side_ref], [side_out_ref])
    yb = jnp.dot(yc_ref[...], wc_ref[...], preferred_element_type=F32)
    y_ret = yr_ref[...] * jnp.concatenate([g[...] for g in gs_refs], axis=1)
    ya = jnp.dot(y_ret, wr_ref[...], preferred_element_type=F32)
    o_ref[...] = (ga_ref[...].astype(F32) * ya + gb_ref[...].astype(F32) * yb).astype(BF16)


def _merge(yr, qkvg, yc, gates, w_ret_o_bf, w_conv_o_bf, side_weight, tm=512):
    t, d = yr.shape
    resident = lambda: pl.BlockSpec((d, d), lambda i: (0, 0), pipeline_mode=pl.Buffered(1))
    side_specs, side_shapes = _side_cast_specs([side_weight], t // tm, lambda i: i)
    gate0 = qkvg.shape[0] - RET_HEADS
    gs_specs = [pl.BlockSpec((None, tm, RET_HEAD_DIM), lambda i, hh=hh: (gate0 + hh, i, 0))
                for hh in range(RET_HEADS)]
    return pl.pallas_call(
        _merge_kernel,
        grid=(t // tm,),
        in_specs=[pl.BlockSpec((tm, d), lambda i: (i, 0)),
                  pl.BlockSpec((tm, d), lambda i: (i, 0)),
                  pl.BlockSpec((tm, d), lambda i: (i, 0)),
                  pl.BlockSpec((tm, d), lambda i: (i, 1)),
                  resident(), resident()] + side_specs + gs_specs,
        out_specs=[pl.BlockSpec((tm, d), lambda i: (i, 0))] + side_specs,
        out_shape=[jax.ShapeDtypeStruct((t, d), BF16)] + side_shapes,
        compiler_params=_params("arbitrary"),
        name="merge",
    )(yr, yc, gates, gates, w_ret_o_bf, w_conv_o_bf, side_weight, *([qkvg] * RET_HEADS))


def _mixout_kernel(m_ref, w_ref, x_ref, g1_ref, g_ref, sc_ref, sh_ref, h_ref, xn_ref, *, n_sub):
    tr = m_ref.shape[0] // n_sub
    for s in range(n_sub):
        rows = pl.ds(s * tr, tr)
        mixed = jnp.dot(m_ref[rows, :], w_ref[...], preferred_element_type=F32)
        h = x_ref[rows, :] + g1_ref[...] * mixed
        h_ref[rows, :] = h
        xn_ref[rows, :] = _mod_rmsnorm(h_ref, rows, g_ref[...], sc_ref[...], sh_ref[...]).astype(BF16)


def _mixout(m, w_mix_o_bf, x2d, mod3, norm_g, seq, tm=512, n_sub=2):
    t, d = x2d.shape
    tps = seq // tm
    mod_spec = lambda k: pl.BlockSpec((None, 1, d), lambda i: (i // tps, 0, k))
    return pl.pallas_call(
        functools.partial(_mixout_kernel, n_sub=n_sub),
        grid=(t // tm,),
        in_specs=[pl.BlockSpec((tm, d), lambda i: (i, 0)),
                  pl.BlockSpec((d, d), lambda i: (0, 0), pipeline_mode=pl.Buffered(1)),
                  pl.BlockSpec((tm, d), lambda i: (i, 0)),
                  mod_spec(2), pl.BlockSpec((1, d), lambda i: (0, 0)), mod_spec(4), mod_spec(3)],
        out_specs=[pl.BlockSpec((tm, d), lambda i: (i, 0)), pl.BlockSpec((tm, d), lambda i: (i, 0))],
        out_shape=[jax.ShapeDtypeStruct((t, d), F32), jax.ShapeDtypeStruct((t, d), BF16)],
        compiler_params=_params("arbitrary"),
        name="mixout",
    )(m, w_mix_o_bf, x2d, mod3, norm_g.reshape(1, d), mod3, mod3)


def _ffn_kernel(xn_ref, wu_ref, wg_ref, wfc_ref, bfc_ref, wd_ref, h_hbm, g2_ref, fg_ref,
                o_ref, halo_ref, h_sem, *, tiles_per_seq, sub_width, last_width):
    i = pl.program_id(0)
    f = pl.program_id(1)
    last = pl.num_programs(1) - 1
    tm = o_ref.shape[0]

    @pl.when(jnp.logical_and(i == 0, f == 0))
    def _():
        halo_ref[...] = jnp.zeros_like(halo_ref)

    def h1_copy():
        return pltpu.make_async_copy(h_hbm.at[pl.ds(i * tm, tm), :], o_ref, h_sem)

    def step(first, width):
        if first:
            h1_copy().start()
        xn = xn_ref[...]
        not_seq_start = i % tiles_per_seq != 0
        col_blocks = [pl.ds(s * sub_width, sub_width) for s in range(width // sub_width)]
        up_gate = [(jnp.dot(xn, wu_ref[:, cols], preferred_element_type=F32),
                    jnp.dot(xn, wg_ref[:, cols], preferred_element_type=F32)) for cols in col_blocks]
        acts = []
        for cols, (up, gpre) in zip(col_blocks, up_gate):
            prev = jnp.where(not_seq_start, halo_ref[f, :, cols], 0.0)
            half_gt = _causal_conv3(gpre, prev, wfc_ref[:, cols]) + 0.5 * bfc_ref[:, cols]
            halo_ref[f, :, cols] = gpre[gpre.shape[0] - HALO_ROWS:, :]
            acts.append((_silu_of_half(half_gt) * up).astype(BF16))
        n_head = len(acts) - 1
        head = None
        if n_head:
            head = jnp.dot(jnp.concatenate(acts[:-1], axis=1), wd_ref[pl.ds(0, n_head * sub_width), :],
                           preferred_element_type=F32)
        part = jnp.dot(acts[-1], wd_ref[col_blocks[-1], :], preferred_element_type=F32)
        if head is not None:
            part = head + part
        if first:
            h1_copy().wait()
        o_ref[...] += g2_ref[...] * part

    tile_width = wu_ref.shape[1]
    pl.when(f == 0)(functools.partial(step, True, tile_width))
    pl.when(jnp.logical_and(f > 0, f < last))(functools.partial(step, False, tile_width))
    pl.when(f == last)(functools.partial(step, False, last_width))

    @pl.when(f == last)
    def _():
        r = lax.rsqrt(jnp.mean(jnp.square(o_ref[...]), axis=-1, keepdims=True) + EPS)
        o_ref[...] = (o_ref[...] * r) * fg_ref[...]


def _ffn(xn2, w_up_bf, w_gate_half_bf, w_ffconv, b_ffconv, w_down_bf, h1, mod3, final_g, seq,
         tm=1024, tf=768, sub_width=256):
    t, d = xn2.shape
    dff = w_up_bf.shape[1]
    nf = pl.cdiv(dff, tf)
    tps = seq // tm
    assert nf >= 2 and tf % sub_width == 0 and dff % sub_width == 0
    kern = functools.partial(_ffn_kernel, tiles_per_seq=tps, sub_width=sub_width,
                             last_width=dff - (nf - 1) * tf)
    return pl.pallas_call(
        kern,
        grid=(t // tm, nf),
        in_specs=[pl.BlockSpec((tm, d), lambda i, f: (i, 0)),
                  pl.BlockSpec((d, tf), lambda i, f: (0, f)),
                  pl.BlockSpec((d, tf), lambda i, f: (0, f)),
                  pl.BlockSpec((CONV_WIDTH, tf), lambda i, f: (0, f)),
                  pl.BlockSpec((1, tf), lambda i, f: (0, f)),
                  pl.BlockSpec((tf, d), lambda i, f: (f, 0)),
                  pl.BlockSpec(memory_space=pl.ANY),
                  pl.BlockSpec((None, 1, d), lambda i, f: (i // tps, 0, 5)),
                  pl.BlockSpec((1, d), lambda i, f: (0, 0))],
        out_specs=pl.BlockSpec((tm, d), lambda i, f: (i, 0)),
        out_shape=jax.ShapeDtypeStruct((t, d), F32),
        scratch_shapes=[pltpu.VMEM((nf, HALO_ROWS, tf), F32), pltpu.SemaphoreType.DMA(())],
        compiler_params=_params("arbitrary", "arbitrary"),
        name="ffn",
    )(xn2, w_up_bf, w_gate_half_bf, w_ffconv, b_ffconv.reshape(1, dff), w_down_bf, h1, mod3,
      final_g.reshape(1, d))


def kernel(x, c, positions, norm1_g, norm2_g, w_ada, b_ada, w_in, b_gate, w_sc, w_ret_o, w_conv_o,
           w_mix_o, w_up, w_gate, w_ffconv, b_ffconv, w_down, final_g):
    batch, seq, d = x.shape
    depth = w_in.shape[0]
    assert depth == 1, "final-norm fusion assumes a single layer"
    assert d == D_RET and w_in.shape[2] == 4 * D_RET + 5 * d, (d, w_in.shape)
    assert seq % 2048 == 0 and seq % RET_CHUNK == 0 and batch % 2 == 0, (batch, seq)
    t = batch * seq
    h = x.reshape(t, d)
    for l in range(depth):
        mod, cos, sin = _ada_rope(c, w_ada[l], b_ada[l], positions)
        mod3 = mod.reshape(batch, 1, 6 * d)
        xn, cos, sin = _norm1(h, norm1_g[l], mod3, positions, cos, sin, seq)
        qkvg, w_ret_o_bf, w_conv_o_bf, w_mix_o_bf = _inproj_a(
            xn, w_in[l], cos.reshape(t, ROT_HALF), sin.reshape(t, ROT_HALF),
            [w_ret_o[l], w_conv_o[l], w_mix_o[l]])
        yc, w_gate_half_bf = _inproj_c(xn, w_in[l], w_sc[l], w_gate[l], 0.5, seq)
        gates, w_up_bf = _inproj_g(xn, w_in[l], b_gate[l], w_up[l])
        yr = _retention(qkvg, batch, seq)
        merged, w_down_bf = _merge(yr, qkvg, yc, gates, w_ret_o_bf, w_conv_o_bf, w_down[l])
        h1, xn2 = _mixout(merged, w_mix_o_bf, h, mod3, norm2_g[l], seq)
        h = _ffn(xn2, w_up_bf, w_gate_half_bf, w_ffconv[l], b_ffconv[l], w_down_bf, h1, mod3, final_g, seq)
    return h.reshape(batch, seq, d)
```

```python
import functools

import jax
import jax.numpy as jnp
from jax import lax
from jax.experimental import pallas as pl
from jax.experimental.pallas import tpu as pltpu

F32 = jnp.float32
BF16 = jnp.bfloat16

RET_HEADS = 8
RET_HEAD_DIM = 256
D_RET = RET_HEADS * RET_HEAD_DIM
ROT_HALF = RET_HEAD_DIM // 2
CONV_WIDTH = 3
ROPE_BASE = 10000.0
EPS = 1e-6
RET_CHUNK = 256
HALO_ROWS = 8

V7X_VMEM_LIMIT_BYTES = 56 * 1024 * 1024


def _params(*semantics):
    return pltpu.CompilerParams(dimension_semantics=semantics,
                                vmem_limit_bytes=V7X_VMEM_LIMIT_BYTES)


def _sigmoid_of_half(h):
    return 0.5 + 0.5 * jnp.tanh(h)


def _silu_of_half(h):
    return h + h * jnp.tanh(h)


def _silu(v):
    return _silu_of_half(0.5 * v)


def _mod_rmsnorm(src_ref, rows, gain, scale, shift):
    x = src_ref[rows, :]
    r = lax.rsqrt(jnp.mean(x * x, axis=-1, keepdims=True) + EPS)
    return (src_ref[rows, :] * r) * (gain * (1.0 + scale)) + shift


def _causal_conv3(u, prev, w):
    def taps(a):
        return w[0:1, :] * pltpu.roll(a, 2, 0) + w[1:2, :] * pltpu.roll(a, 1, 0) + w[2:3, :] * a

    head = taps(jnp.concatenate([prev, u[:HALO_ROWS, :]], axis=0))[HALO_ROWS:, :]
    return jnp.concatenate([head, taps(u)[HALO_ROWS:, :]], axis=0)


def _side_cast_specs(weights, steps, step_index):
    specs = [pl.BlockSpec((w.shape[0] // steps, w.shape[1]), lambda *g: (step_index(*g), 0)) for w in weights]
    shapes = [jax.ShapeDtypeStruct(w.shape, BF16) for w in weights]
    return specs, shapes


def _side_cast(in_refs, out_refs, scale=None):
    for src, dst in zip(in_refs, out_refs):
        w = src[...] if scale is None else src[...] * scale
        dst[...] = w.astype(BF16)


def _rope_angles(positions, tr):
    inv_freq = ROPE_BASE ** (-jnp.arange(ROT_HALF, dtype=F32) / ROT_HALF)
    return positions.reshape(positions.size // tr, 1, tr), inv_freq.reshape(ROT_HALF, 1)


def _rope_tables(pos_ref, inv_ref):
    ang_t = inv_ref[...] * pos_ref[...].astype(F32)
    return jnp.cos(ang_t).T, jnp.sin(ang_t).T


def _ada_rope_kernel(c_ref, w_ref, b_ref, pos_ref, inv_ref, mod_ref, cos_ref, sin_ref):
    @pl.when(pl.program_id(0) == 0)
    def _():
        mod_ref[...] = jnp.broadcast_to(b_ref[...], mod_ref.shape)

    ca = _silu(c_ref[...]).astype(BF16)
    mod_ref[...] += jnp.dot(ca, w_ref[...].astype(BF16), preferred_element_type=F32)
    cos, sin = _rope_tables(pos_ref, inv_ref)
    cos_ref[0] = cos
    sin_ref[0] = sin
    cos_ref[1] = jnp.zeros_like(cos)
    sin_ref[1] = jnp.zeros_like(sin)


def _ada_rope(c, w_ada, b_ada, positions, steps=8):
    b, d = c.shape
    n = w_ada.shape[1]
    half = positions.size // 2
    tk, tr = d // steps, half // steps
    pos, inv_freq = _rope_angles(positions, tr)
    rope_spec = pl.BlockSpec((2, tr, ROT_HALF), lambda j: (0, j, 0))
    return pl.pallas_call(
        _ada_rope_kernel,
        grid=(steps,),
        in_specs=[pl.BlockSpec((b, tk), lambda j: (0, j)),
                  pl.BlockSpec((tk, n), lambda j: (j, 0)),
                  pl.BlockSpec((1, n), lambda j: (0, 0)),
                  pl.BlockSpec((None, 1, tr), lambda j: (j, 0, 0)),
                  pl.BlockSpec((ROT_HALF, 1), lambda j: (0, 0))],
        out_specs=[pl.BlockSpec((b, n), lambda j: (0, 0)), rope_spec, rope_spec],
        out_shape=[jax.ShapeDtypeStruct((b, n), F32),
                   jax.ShapeDtypeStruct((2, half, ROT_HALF), F32),
                   jax.ShapeDtypeStruct((2, half, ROT_HALF), F32)],
        compiler_params=_params("arbitrary"),
        name="ada_rope",
    )(c, w_ada, b_ada.reshape(1, n), pos, inv_freq)


def _norm1_kernel(x_ref, g_ref, sc_ref, sh_ref, pos_ref, inv_ref, cos_in, sin_in, o_ref, cos_ref, sin_ref):
    del cos_in, sin_in
    o_ref[...] = _mod_rmsnorm(x_ref, slice(None), g_ref[...], sc_ref[...], sh_ref[...]).astype(BF16)
    cos_ref[...], sin_ref[...] = _rope_tables(pos_ref, inv_ref)


def _norm1(x2d, norm_g, mod3, positions, cos, sin, seq, tm=2048):
    t, d = x2d.shape
    tps = seq // tm
    steps = t // tm
    tr = cos.shape[1] // steps
    pos, inv_freq = _rope_angles(positions, tr)
    rope_spec = pl.BlockSpec((None, tr, ROT_HALF), lambda i: (1, i, 0))
    return pl.pallas_call(
        _norm1_kernel,
        grid=(steps,),
        in_specs=[pl.BlockSpec((tm, d), lambda i: (i, 0)),
                  pl.BlockSpec((1, d), lambda i: (0, 0)),
                  pl.BlockSpec((None, 1, d), lambda i: (i // tps, 0, 1)),
                  pl.BlockSpec((None, 1, d), lambda i: (i // tps, 0, 0)),
                  pl.BlockSpec((None, 1, tr), lambda i: (steps + i, 0, 0)),
                  pl.BlockSpec((ROT_HALF, 1), lambda i: (0, 0)),
                  pl.BlockSpec(memory_space=pl.ANY),
                  pl.BlockSpec(memory_space=pl.ANY)],
        out_specs=[pl.BlockSpec((tm, d), lambda i: (i, 0)), rope_spec, rope_spec],
        out_shape=[jax.ShapeDtypeStruct((t, d), BF16),
                   jax.ShapeDtypeStruct(cos.shape, F32), jax.ShapeDtypeStruct(sin.shape, F32)],
        input_output_aliases={6: 1, 7: 2},
        compiler_params=_params("arbitrary"),
        name="norm1",
    )(x2d, norm_g.reshape(1, d), mod3, mod3, pos, inv_freq, cos, sin)


def _inproj_a_kernel(xn_ref, w_ref, cos_ref, sin_ref, *rest, q_tiles, rot_tiles, gate_tile0, n_sub, n_side):
    side_in, (o_ref, *side_out, wbf_ref) = rest[:n_side], rest[n_side:]
    j = pl.program_id(0)

    @pl.when(pl.program_id(1) == 0)
    def _():
        scale = jnp.where(j >= gate_tile0, 0.5, 1.0).astype(F32)
        wbf_ref[...] = (w_ref[...] * scale).astype(BF16)

    _side_cast(side_in, side_out)
    heads = o_ref.shape[0]
    tr = xn_ref.shape[0] // n_sub

    def proj(rows):
        return jnp.dot(xn_ref[rows, :], wbf_ref[...], preferred_element_type=F32)

    @pl.when(j < rot_tiles)
    def _():
        scale = jnp.where(j < q_tiles, RET_HEAD_DIM ** -0.5, 1.0).astype(F32)
        for s in range(n_sub):
            rows = pl.ds(s * tr, tr)
            acc = proj(rows)
            cos = cos_ref[rows, :] * scale
            sin = sin_ref[rows, :] * scale
            for hh in range(heads):
                lo = hh * RET_HEAD_DIM
                x1 = acc[:, lo:lo + ROT_HALF]
                x2 = acc[:, lo + ROT_HALF:lo + RET_HEAD_DIM]
                o_ref[hh, rows, 0:ROT_HALF] = (x1 * cos - x2 * sin).astype(BF16)
                o_ref[hh, rows, ROT_HALF:RET_HEAD_DIM] = (x2 * cos + x1 * sin).astype(BF16)

    def plain(fn):
        for s in range(n_sub):
            rows = pl.ds(s * tr, tr)
            res = fn(proj(rows)).astype(BF16)
            for hh in range(heads):
                o_ref[hh, rows, :] = res[:, hh * RET_HEAD_DIM:(hh + 1) * RET_HEAD_DIM]

    pl.when(jnp.logical_and(j >= rot_tiles, j < gate_tile0))(functools.partial(plain, lambda a: a))
    pl.when(j >= gate_tile0)(functools.partial(plain, _silu_of_half))


def _inproj_a(xn, w_in, cos, sin, side_weights, tm=2048, tn=1024, n_sub=4):
    t, d = xn.shape
    n = 4 * D_RET
    heads = tn // RET_HEAD_DIM
    n_i = t // tm
    side_specs, side_shapes = _side_cast_specs(side_weights, (n // tn) * n_i, lambda j, i: j * n_i + i)
    kern = functools.partial(_inproj_a_kernel, q_tiles=D_RET // tn, rot_tiles=2 * D_RET // tn,
                             gate_tile0=3 * D_RET // tn, n_sub=n_sub, n_side=len(side_weights))
    return pl.pallas_call(
        kern,
        grid=(n // tn, n_i),
        in_specs=[pl.BlockSpec((tm, d), lambda j, i: (i, 0)),
                  pl.BlockSpec((d, tn), lambda j, i: (0, j)),
                  pl.BlockSpec((tm, ROT_HALF), lambda j, i: (i, 0)),
                  pl.BlockSpec((tm, ROT_HALF), lambda j, i: (i, 0))] + side_specs,
        out_specs=[pl.BlockSpec((heads, tm, RET_HEAD_DIM), lambda j, i: (j, i, 0))] + side_specs,
        out_shape=[jax.ShapeDtypeStruct((n // RET_HEAD_DIM, t, RET_HEAD_DIM), BF16)] + side_shapes,
        scratch_shapes=[pltpu.VMEM((d, tn), BF16)],
        compiler_params=_params("arbitrary", "arbitrary"),
        name="inproj_a",
    )(xn, w_in, cos, sin, *side_weights)


def _inproj_c_kernel(xn_ref, wb_ref, wc_ref, wx_ref, wsc_ref, side_ref, o_ref, side_out_ref,
                     wb_bf, wc_bf, wx_bf, halo_ref, *, tiles_per_seq, side_scale):
    j = pl.program_id(0)
    i = pl.program_id(1)

    @pl.when(jnp.logical_and(i == 0, j == 0))
    def _():
        halo_ref[...] = jnp.zeros_like(halo_ref)

    @pl.when(i == 0)
    def _():
        wb_bf[...] = wb_ref[...].astype(BF16)
        wc_bf[...] = wc_ref[...].astype(BF16)
        wx_bf[...] = wx_ref[...].astype(BF16)

    _side_cast([side_ref], [side_out_ref], scale=side_scale)
    xn = xn_ref[...]
    cg = jnp.dot(xn, wc_bf[...], preferred_element_type=F32)
    xc = jnp.dot(xn, wx_bf[...], preferred_element_type=F32)
    u = cg * xc
    prev = jnp.where(i % tiles_per_seq == 0, 0.0, halo_ref[...])
    conv = _causal_conv3(u, prev, wsc_ref[...])
    halo_ref[...] = u[u.shape[0] - HALO_ROWS:, :]
    bg = jnp.dot(xn, wb_bf[...], preferred_element_type=F32)
    o_ref[...] = (bg * conv).astype(BF16)


def _inproj_c(xn, w_in, w_sc, side_weight, side_scale, seq, tm=1024, tc=512):
    t, d = xn.shape
    nc = d // tc
    n_i = t // tm
    first = 4 * D_RET // tc
    kern = functools.partial(_inproj_c_kernel, tiles_per_seq=seq // tm, side_scale=side_scale)
    wspec = lambda sec: pl.BlockSpec((d, tc), lambda j, i: (0, first + sec * nc + j))
    side_specs, side_shapes = _side_cast_specs([side_weight], nc * n_i, lambda j, i: j * n_i + i)
    return pl.pallas_call(
        kern,
        grid=(nc, n_i),
        in_specs=[pl.BlockSpec((tm, d), lambda j, i: (i, 0)),
                  wspec(0), wspec(1), wspec(2),
                  pl.BlockSpec((CONV_WIDTH, tc), lambda j, i: (0, j))] + side_specs,
        out_specs=[pl.BlockSpec((tm, tc), lambda j, i: (i, j))] + side_specs,
        out_shape=[jax.ShapeDtypeStruct((t, d), BF16)] + side_shapes,
        scratch_shapes=[pltpu.VMEM((d, tc), BF16)] * 3 + [pltpu.VMEM((HALO_ROWS, tc), F32)],
        compiler_params=_params("arbitrary", "arbitrary"),
        name="inproj_c",
    )(xn, w_in, w_in, w_in, w_sc, side_weight)


def _inproj_g_kernel(xn_ref, w_ref, b_ref, side_ref, o_ref, side_out_ref, wbf_ref, *, n_sub):
    @pl.when(pl.program_id(1) == 0)
    def _():
        wbf_ref[...] = (w_ref[...] * 0.5).astype(BF16)

    _side_cast([side_ref], [side_out_ref])
    tr = xn_ref.shape[0] // n_sub
    half_bias = 0.5 * b_ref[...]
    for s in range(n_sub):
        rows = pl.ds(s * tr, tr)
        half_logits = jnp.dot(xn_ref[rows, :], wbf_ref[...], preferred_element_type=F32) + half_bias
        o_ref[rows, :] = _sigmoid_of_half(half_logits).astype(BF16)


def _inproj_g(xn, w_in, b_gate, side_weight, tm=2048, tn=1024, n_sub=4):
    t, d = xn.shape
    n = b_gate.shape[0]
    n_i = t // tm
    first = (w_in.shape[1] - n) // tn
    side_specs, side_shapes = _side_cast_specs([side_weight], (n // tn) * n_i, lambda j, i: j * n_i + i)
    return pl.pallas_call(
        functools.partial(_inproj_g_kernel, n_sub=n_sub),
        grid=(n // tn, n_i),
        in_specs=[pl.BlockSpec((tm, d), lambda j, i: (i, 0)),
                  pl.BlockSpec((d, tn), lambda j, i: (0, first + j)),
                  pl.BlockSpec((1, tn), lambda j, i: (0, j))] + side_specs,
        out_specs=[pl.BlockSpec((tm, tn), lambda j, i: (i, j))] + side_specs,
        out_shape=[jax.ShapeDtypeStruct((t, n), BF16)] + side_shapes,
        scratch_shapes=[pltpu.VMEM((d, tn), BF16)],
        compiler_params=_params("arbitrary", "arbitrary"),
        name="inproj_g",
    )(xn, w_in, b_gate.reshape(1, n), side_weight)


def _ret_kernel(q_ref, k_ref, v_ref, o_ref):
    heads = q_ref.shape[0]
    first = pl.program_id(1) * heads

    def prepare(hh):
        return _ret_scores_and_states(first + hh, q_ref.at[hh], k_ref.at[hh], v_ref.at[hh])

    prep = prepare(0)
    for hh in range(heads):
        next_prep = prepare(hh + 1) if hh + 1 < heads else None
        cols = pl.ds(hh * RET_HEAD_DIM, RET_HEAD_DIM)
        _ret_outputs(prep, q_ref.at[hh], v_ref.at[hh], o_ref.at[:, cols])
        prep = next_prep


def _ret_scores_and_states(head_index, q_ref, k_ref, v_ref):
    c = RET_CHUNK
    n_chunks = q_ref.shape[0] // c
    head = jnp.full((1, RET_HEAD_DIM), head_index, jnp.int32).astype(F32)
    log_gamma = jnp.log(1.0 - jnp.exp2(-5.0 - head))
    row = lax.broadcasted_iota(jnp.int32, (c, c), 0).astype(F32)
    col = lax.broadcasted_iota(jnp.int32, (c, c), 1).astype(F32)
    diff = row - col
    decay = jnp.where(diff >= 0, jnp.exp(log_gamma * jnp.maximum(diff, 0.0)), 0.0)
    q_dec = jnp.exp(log_gamma * (row + 1.0))
    k_dec = jnp.exp(log_gamma * (c - 1.0 - row))
    chunk_dec = jnp.exp(log_gamma * float(c))
    chunks = [pl.ds(n * c, c) for n in range(n_chunks)]

    kvs = []
    for rows in chunks[:-1]:
        kd = (k_ref[rows, :].astype(F32) * k_dec).astype(BF16)
        kvs.append(lax.dot_general(kd, v_ref[rows, :], (((0,), (0,)), ((), ())),
                                   preferred_element_type=F32))
    scores = [lax.dot_general(q_ref[rows, :], k_ref[rows, :], (((1,), (1,)), ((), ())),
                              preferred_element_type=F32) for rows in chunks]
    states = [None]
    for kv in kvs:
        states.append(kv if states[-1] is None else chunk_dec * states[-1] + kv)
    return chunks, scores, states, decay, q_dec


def _ret_outputs(prep, q_ref, v_ref, o_ref):
    chunks, scores, states, decay, q_dec = prep
    for rows, s, state in zip(chunks, scores, states):
        y = jnp.dot((s * decay).astype(BF16), v_ref[rows, :], preferred_element_type=F32)
        if state is not None:
            y = y + q_dec * jnp.dot(q_ref[rows, :], state.astype(BF16), preferred_element_type=F32)
        yn = y * lax.rsqrt(jnp.mean(y * y, axis=-1, keepdims=True) + EPS)
        o_ref[rows, :] = yn.astype(BF16)


def _retention(qkvg, batch, seq, heads_per_step=4):
    t = qkvg.shape[1]
    groups = RET_HEADS // heads_per_step
    spec = lambda sec: pl.BlockSpec((heads_per_step, seq, RET_HEAD_DIM),
                                    lambda b, g: (sec * groups + g, b, 0))
    return pl.pallas_call(
        _ret_kernel,
        grid=(batch, groups),
        in_specs=[spec(0), spec(1), spec(2)],
        out_specs=pl.BlockSpec((seq, heads_per_step * RET_HEAD_DIM), lambda b, g: (b, g)),
        out_shape=jax.ShapeDtypeStruct((t, D_RET), BF16),
        compiler_params=_params("arbitrary", "arbitrary"),
        name="ret",
    )(qkvg, qkvg, qkvg)


def _merge_kernel(yr_ref, yc_ref, ga_ref, gb_ref, wr_ref, wc_ref, side_ref, *rest):
    gs_refs, (o_ref, side_out_ref) = rest[:RET_HEADS], rest[RET_HEADS:]
    _side_cast([side_ref], [side_out_ref])
    yb = jnp.dot(yc_ref[...], wc_ref[...], preferred_element_type=F32)
    y_ret = yr_ref[...] * jnp.concatenate([g[...] for g in gs_refs], axis=1)
    ya = jnp.dot(y_ret, wr_ref[...], preferred_element_type=F32)
    o_ref[...] = (ga_ref[...].astype(F32) * ya + gb_ref[...].astype(F32) * yb).astype(BF16)


def _merge(yr, qkvg, yc, gates, w_ret_o_bf, w_conv_o_bf, side_weight, tm=512):
    t, d = yr.shape
    resident = lambda: pl.BlockSpec((d, d), lambda i: (0, 0), pipeline_mode=pl.Buffered(1))
    side_specs, side_shapes = _side_cast_specs([side_weight], t // tm, lambda i: i)
    gate0 = qkvg.shape[0] - RET_HEADS
    gs_specs = [pl.BlockSpec((None, tm, RET_HEAD_DIM), lambda i, hh=hh: (gate0 + hh, i, 0))
                for hh in range(RET_HEADS)]
    return pl.pallas_call(
        _merge_kernel,
        grid=(t // tm,),
        in_specs=[pl.BlockSpec((tm, d), lambda i: (i, 0)),
                  pl.BlockSpec((tm, d), lambda i: (i, 0)),
                  pl.BlockSpec((tm, d), lambda i: (i, 0)),
                  pl.BlockSpec((tm, d), lambda i: (i, 1)),
                  resident(), resident()] + side_specs + gs_specs,
        out_specs=[pl.BlockSpec((tm, d), lambda i: (i, 0))] + side_specs,
        out_shape=[jax.ShapeDtypeStruct((t, d), BF16)] + side_shapes,
        compiler_params=_params("arbitrary"),
        name="merge",
    )(yr, yc, gates, gates, w_ret_o_bf, w_conv_o_bf, side_weight, *([qkvg] * RET_HEADS))


def _mixout_kernel(m_ref, w_ref, x_ref, g1_ref, g_ref, sc_ref, sh_ref, h_ref, xn_ref, *, n_sub):
    tr = m_ref.shape[0] // n_sub
    for s in range(n_sub):
        rows = pl.ds(s * tr, tr)
        mixed = jnp.dot(m_ref[rows, :], w_ref[...], preferred_element_type=F32)
        h = x_ref[rows, :] + g1_ref[...] * mixed
        h_ref[rows, :] = h
        xn_ref[rows, :] = _mod_rmsnorm(h_ref, rows, g_ref[...], sc_ref[...], sh_ref[...]).astype(BF16)


def _mixout(m, w_mix_o_bf, x2d, mod3, norm_g, seq, tm=512, n_sub=2):
    t, d = x2d.shape
    tps = seq // tm
    mod_spec = lambda k: pl.BlockSpec((None, 1, d), lambda i: (i // tps, 0, k))
    return pl.pallas_call(
        functools.partial(_mixout_kernel, n_sub=n_sub),
        grid=(t // tm,),
        in_specs=[pl.BlockSpec((tm, d), lambda i: (i, 0)),
                  pl.BlockSpec((d, d), lambda i: (0, 0), pipeline_mode=pl.Buffered(1)),
                  pl.BlockSpec((tm, d), lambda i: (i, 0)),
                  mod_spec(2), pl.BlockSpec((1, d), lambda i: (0, 0)), mod_spec(4), mod_spec(3)],
        out_specs=[pl.BlockSpec((tm, d), lambda i: (i, 0)), pl.BlockSpec((tm, d), lambda i: (i, 0))],
        out_shape=[jax.ShapeDtypeStruct((t, d), F32), jax.ShapeDtypeStruct((t, d), BF16)],
        compiler_params=_params("arbitrary"),
        name="mixout",
    )(m, w_mix_o_bf, x2d, mod3, norm_g.reshape(1, d), mod3, mod3)


def _ffn_kernel(xn_ref, wu_ref, wg_ref, wfc_ref, bfc_ref, wd_ref, h_hbm, g2_ref, fg_ref,
                o_ref, halo_ref, h_sem, *, tiles_per_seq, sub_width, last_width):
    i = pl.program_id(0)
    f = pl.program_id(1)
    last = pl.num_programs(1) - 1
    tm = o_ref.shape[0]

    @pl.when(jnp.logical_and(i == 0, f == 0))
    def _():
        halo_ref[...] = jnp.zeros_like(halo_ref)

    def h1_copy():
        return pltpu.make_async_copy(h_hbm.at[pl.ds(i * tm, tm), :], o_ref, h_sem)

    def step(first, width):
        if first:
            h1_copy().start()
        xn = xn_ref[...]
        not_seq_start = i % tiles_per_seq != 0
        col_blocks = [pl.ds(s * sub_width, sub_width) for s in range(width // sub_width)]
        up_gate = [(jnp.dot(xn, wu_ref[:, cols], preferred_element_type=F32),
                    jnp.dot(xn, wg_ref[:, cols], preferred_element_type=F32)) for cols in col_blocks]
        acts = []
        for cols, (up, gpre) in zip(col_blocks, up_gate):
            prev = jnp.where(not_seq_start, halo_ref[f, :, cols], 0.0)
            half_gt = _causal_conv3(gpre, prev, wfc_ref[:, cols]) + 0.5 * bfc_ref[:, cols]
            halo_ref[f, :, cols] = gpre[gpre.shape[0] - HALO_ROWS:, :]
            acts.append((_silu_of_half(half_gt) * up).astype(BF16))
        n_head = len(acts) - 1
        head = None
        if n_head:
            head = jnp.dot(jnp.concatenate(acts[:-1], axis=1), wd_ref[pl.ds(0, n_head * sub_width), :],
                           preferred_element_type=F32)
        part = jnp.dot(acts[-1], wd_ref[col_blocks[-1], :], preferred_element_type=F32)
        if head is not None:
            part = head + part
        if first:
            h1_copy().wait()
        o_ref[...] += g2_ref[...] * part

    tile_width = wu_ref.shape[1]
    pl.when(f == 0)(functools.partial(step, True, tile_width))
    pl.when(jnp.logical_and(f > 0, f < last))(functools.partial(step, False, tile_width))
    pl.when(f == last)(functools.partial(step, False, last_width))

    @pl.when(f == last)
    def _():
        r = lax.rsqrt(jnp.mean(jnp.square(o_ref[...]), axis=-1, keepdims=True) + EPS)
        o_ref[...] = (o_ref[...] * r) * fg_ref[...]


def _ffn(xn2, w_up_bf, w_gate_half_bf, w_ffconv, b_ffconv, w_down_bf, h1, mod3, final_g, seq,
         tm=1024, tf=768, sub_width=256):
    t, d = xn2.shape
    dff = w_up_bf.shape[1]
    nf = pl.cdiv(dff, tf)
    tps = seq // tm
    assert nf >= 2 and tf % sub_width == 0 and dff % sub_width == 0
    kern = functools.partial(_ffn_kernel, tiles_per_seq=tps, sub_width=sub_width,
                             last_width=dff - (nf - 1) * tf)
    return pl.pallas_call(
        kern,
        grid=(t // tm, nf),
        in_specs=[pl.BlockSpec((tm, d), lambda i, f: (i, 0)),
                  pl.BlockSpec((d, tf), lambda i, f: (0, f)),
                  pl.BlockSpec((d, tf), lambda i, f: (0, f)),
                  pl.BlockSpec((CONV_WIDTH, tf), lambda i, f: (0, f)),
                  pl.BlockSpec((1, tf), lambda i, f: (0, f)),
                  pl.BlockSpec((tf, d), lambda i, f: (f, 0)),
                  pl.BlockSpec(memory_space=pl.ANY),
                  pl.BlockSpec((None, 1, d), lambda i, f: (i // tps, 0, 5)),
                  pl.BlockSpec((1, d), lambda i, f: (0, 0))],
        out_specs=pl.BlockSpec((tm, d), lambda i, f: (i, 0)),
        out_shape=jax.ShapeDtypeStruct((t, d), F32),
        scratch_shapes=[pltpu.VMEM((nf, HALO_ROWS, tf), F32), pltpu.SemaphoreType.DMA(())],
        compiler_params=_params("arbitrary", "arbitrary"),
        name="ffn",
    )(xn2, w_up_bf, w_gate_half_bf, w_ffconv, b_ffconv.reshape(1, dff), w_down_bf, h1, mod3,
      final_g.reshape(1, d))


def kernel(x, c, positions, norm1_g, norm2_g, w_ada, b_ada, w_in, b_gate, w_sc, w_ret_o, w_conv_o,
           w_mix_o, w_up, w_gate, w_ffconv, b_ffconv, w_down, final_g):
    batch, seq, d = x.shape
    depth = w_in.shape[0]
    assert depth == 1, "final-norm fusion assumes a single layer"
    assert d == D_RET and w_in.shape[2] == 4 * D_RET + 5 * d, (d, w_in.shape)
    assert seq % 2048 == 0 and seq % RET_CHUNK == 0 and batch % 2 == 0, (batch, seq)
    t = batch * seq
    h = x.reshape(t, d)
    for l in range(depth):
        mod, cos, sin = _ada_rope(c, w_ada[l], b_ada[l], positions)
        mod3 = mod.reshape(batch, 1, 6 * d)
        xn, cos, sin = _norm1(h, norm1_g[l], mod3, positions, cos, sin, seq)
        qkvg, w_ret_o_bf, w_conv_o_bf, w_mix_o_bf = _inproj_a(
            xn, w_in[l], cos.reshape(t, ROT_HALF), sin.reshape(t, ROT_HALF),
            [w_ret_o[l], w_conv_o[l], w_mix_o[l]])
        yc, w_gate_half_bf = _inproj_c(xn, w_in[l], w_sc[l], w_gate[l], 0.5, seq)
        gates, w_up_bf = _inproj_g(xn, w_in[l], b_gate[l], w_up[l])
        yr = _retention(qkvg, batch, seq)
        merged, w_down_bf = _merge(yr, qkvg, yc, gates, w_ret_o_bf, w_conv_o_bf, w_down[l])
        h1, xn2 = _mixout(merged, w_mix_o_bf, h, mod3, norm2_g[l], seq)
        h = _ffn(xn2, w_up_bf, w_gate_half_bf, w_ffconv[l], b_ffconv[l], w_down_bf, h1, mod3, final_g, seq)
    return h.reshape(batch, seq, d)
```
